```python
import math
import jax, jax.numpy as jnp
from jax import lax
import numpy as np

D_MODEL = 2048
BATCH = 8
SEQ = 8192
DEPTH = 2

N_EVEN = (DEPTH + 1) // 2
N_ODD = DEPTH // 2
RMS_EPS = 1e-6

HG_HEADS = 8
HG_KDIM = 128
HG_VDIM = D_MODEL // 2 // HG_HEADS
HG_CHUNK = 64
HG_K_TOTAL = HG_HEADS * HG_KDIM
HG_V_TOTAL = HG_HEADS * HG_VDIM

SB_HEADS = 8
SB_HEAD_DIM = D_MODEL // 2 // SB_HEADS
SB_BLOCK = 128
SB_TOTAL = SB_HEADS * SB_HEAD_DIM

AB_SIZES = (HG_K_TOTAL, HG_K_TOTAL, HG_V_TOTAL, HG_V_TOTAL, SB_TOTAL, SB_TOTAL, SB_TOTAL)
AB_IN_DIM = sum(AB_SIZES)
AB_SPLITS = tuple(int(v) for v in np.cumsum(AB_SIZES)[:-1])
MIX_WIDTH = HG_V_TOTAL + SB_TOTAL

POOL_WINDOWS = (2, 4, 8, 16)
N_POOL_GROUPS = len(POOL_WINDOWS)
POOL_GROUP = D_MODEL // N_POOL_GROUPS

FFN_HIDDEN = -(-8 * D_MODEL // (3 * 256)) * 256

kernel_name = 'hgrn2_stickbreak_pool_hybrid'


def rms_norm(x, gain):
    x32 = x.astype(jnp.float32)
    y = x32 * lax.rsqrt(jnp.mean(x32 * x32, axis=-1, keepdims=True) + RMS_EPS)
    return (y * gain.astype(jnp.float32)).astype(x.dtype)


def swiglu(h, w_gate, w_up, w_down):
    return (jax.nn.silu(h @ w_gate) * (h @ w_up)) @ w_down


def hgrn2_chunked(q, log_f, k, v):
    b, s, h, dk = q.shape
    dv = v.shape[-1]
    nc = s // HG_CHUNK

    def to_chunks(t):
        return t.reshape(b, nc, HG_CHUNK, h, t.shape[-1]).transpose(1, 0, 3, 2, 4)

    causal = jnp.tril(jnp.ones((HG_CHUNK, HG_CHUNK), dtype=bool))

    def step(state, inp):
        qc, gc, kc, vc = inp
        g_cum = jnp.cumsum(gc, axis=-2)
        o_inter = jnp.einsum('bhtk,bhkv->bhtv', qc * jnp.exp(g_cum), state)
        diff = g_cum[:, :, :, None, :] - g_cum[:, :, None, :, :]
        decay = jnp.exp(jnp.where(causal[:, :, None], diff, -jnp.inf))
        att = jnp.einsum('bhtk,bhtsk,bhsk->bhts', qc, decay, kc)
        o_intra = jnp.einsum('bhts,bhsv->bhtv', att, vc)
        g_last = g_cum[:, :, -1:, :]
        new_state = (jnp.exp(g_last[:, :, 0, :])[..., None] * state
                     + jnp.einsum('bhsk,bhsv->bhkv', kc * jnp.exp(g_last - g_cum), vc))
        return new_state, o_inter + o_intra

    s0 = jnp.zeros((b, h, dk, dv), jnp.float32)
    _, o = lax.scan(step, s0, (to_chunks(q), to_chunks(log_f), to_chunks(k), to_chunks(v)))
    return o.transpose(1, 0, 3, 2, 4).reshape(b, s, h, dv)


def stick_breaking_attention(q, k, v):
    b, h, s, d = q.shape
    nb = s // SB_BLOCK
    scale = 1.0 / math.sqrt(d)
    q_blocks = q.reshape(b, h, nb, SB_BLOCK, d).transpose(2, 0, 1, 3, 4)
    key_pos = jnp.arange(s)

    def block(args):
        qb, start = args
        z = jnp.einsum('bhqd,bhsd->bhqs', qb, k).astype(jnp.float32) * scale
        query_pos = start + jnp.arange(SB_BLOCK)
        mask = key_pos[None, :] < query_pos[:, None]
        log_keep = jnp.where(mask, jax.nn.log_sigmoid(-z), 0.0)
        later = lax.cumsum(log_keep, axis=3, reverse=True) - log_keep
        weights = jnp.where(mask, jnp.exp(jax.nn.log_sigmoid(z) + later), 0.0)
        return jnp.einsum('bhqs,bhsd->bhqd', weights.astype(v.dtype), v)

    starts = jnp.arange(nb, dtype=jnp.int32) * SB_BLOCK
    o = lax.map(block, (q_blocks, starts))
    return o.transpose(1, 2, 0, 3, 4).reshape(b, h, s, d)


def hybrid_ab_mixer(h, w_in, lower_bound, hg_out_norm, w_out):
    b, s, _ = h.shape
    proj = h @ w_in
    qa, fa, ia, ga, qb, kb, vb = jnp.split(proj, AB_SPLITS, axis=-1)

    f = lower_bound + (1.0 - lower_bound) * jax.nn.sigmoid(fa.astype(jnp.float32))
    heads_k = lambda t: t.reshape(b, s, HG_HEADS, HG_KDIM)
    o_a = hgrn2_chunked(heads_k(jax.nn.silu(qa.astype(jnp.float32))),
                        heads_k(jnp.log(f)),
                        heads_k(1.0 - f),
                        ia.astype(jnp.float32).reshape(b, s, HG_HEADS, HG_VDIM))
    o_a = rms_norm(o_a, hg_out_norm) * jax.nn.silu(ga.astype(jnp.float32)).reshape(b, s, HG_HEADS, HG_VDIM)
    o_a = o_a.reshape(b, s, HG_V_TOTAL).astype(h.dtype)

    heads_b = lambda t: t.reshape(b, s, SB_HEADS, SB_HEAD_DIM).transpose(0, 2, 1, 3)
    o_b = stick_breaking_attention(heads_b(qb), heads_b(kb), heads_b(vb))
    o_b = o_b.transpose(0, 2, 1, 3).reshape(b, s, SB_TOTAL).astype(h.dtype)

    return jnp.concatenate([o_a, o_b], axis=-1) @ w_out


def multiscale_pool_mixer(h, w_groups, scale):
    b, s, d = h.shape
    hg = h.astype(jnp.float32).reshape(b, s, N_POOL_GROUPS, POOL_GROUP)
    prefix = jnp.concatenate([jnp.zeros((b, 1, N_POOL_GROUPS, POOL_GROUP), jnp.float32),
                              jnp.cumsum(hg, axis=1)], axis=1)
    pos = jnp.arange(s)
    outs = []
    for gi, w in enumerate(POOL_WINDOWS):
        lo = jnp.maximum(pos + 1 - w, 0)
        window_sum = prefix[:, 1:, gi] - prefix[:, lo, gi]
        count = jnp.minimum(pos + 1, w).astype(jnp.float32)
        outs.append(window_sum / count[None, :, None] - hg[:, :, gi])
    pooled = jnp.stack(outs, axis=2).astype(h.dtype)
    mixed = jnp.einsum('bsgc,gcd->bsgd', pooled, w_groups)
    return mixed.reshape(b, s, d) * scale


def _fwd_setup_inputs(seed: int = 0) -> dict:
    key = jax.random.key(seed)
    ks = jax.random.split(key, 14)
    f32 = jnp.float32
    nrm = lambda k, shape, fan_in: jax.random.normal(k, shape, f32) * (fan_in ** -0.5)
    return {
        'x': jax.random.normal(ks[0], (BATCH, SEQ, D_MODEL), f32),
        'mix_norm': 1.0 + 0.02 * jax.random.normal(ks[1], (DEPTH, D_MODEL), f32),
        'ffn_norm': 1.0 + 0.02 * jax.random.normal(ks[2], (DEPTH, D_MODEL), f32),
        'final_norm': 1.0 + 0.02 * jax.random.normal(ks[3], (D_MODEL,), f32),
        'ab_w_in': nrm(ks[4], (N_EVEN, D_MODEL, AB_IN_DIM), D_MODEL),
        'lb_logits': 0.1 * jax.random.normal(ks[5], (N_EVEN + 1, HG_K_TOTAL), f32),
        'hg_out_norm': 1.0 + 0.02 * jax.random.normal(ks[6], (N_EVEN, HG_VDIM), f32),
        'ab_w_out': nrm(ks[7], (N_EVEN, MIX_WIDTH, D_MODEL), MIX_WIDTH),
        'pool_w': nrm(ks[8], (N_ODD, N_POOL_GROUPS, POOL_GROUP, POOL_GROUP), POOL_GROUP),
        'pool_scale': 1.0 + 0.02 * jax.random.normal(ks[9], (N_ODD, D_MODEL), f32),
        'ffn_w_gate': nrm(ks[10], (DEPTH, D_MODEL, FFN_HIDDEN), D_MODEL),
        'ffn_w_up': nrm(ks[11], (DEPTH, D_MODEL, FFN_HIDDEN), D_MODEL),
        'ffn_w_down': nrm(ks[12], (DEPTH, FFN_HIDDEN, D_MODEL), FFN_HIDDEN),
    }


def _fwd_reference(x, mix_norm, ffn_norm, final_norm, ab_w_in, lb_logits, hg_out_norm, ab_w_out,
              pool_w, pool_scale, ffn_w_gate, ffn_w_up, ffn_w_down):
    lower_bounds = jnp.cumsum(jax.nn.softmax(lb_logits.astype(jnp.float32), axis=0), axis=0)
    for layer in range(DEPTH):
        h = rms_norm(x, mix_norm[layer])
        i = layer // 2
        if layer % 2 == 0:
            mix = hybrid_ab_mixer(h, ab_w_in[i], lower_bounds[i], hg_out_norm[i], ab_w_out[i])
        else:
            mix = multiscale_pool_mixer(h, pool_w[i], pool_scale[i])
        x = x + mix.astype(x.dtype)
        h = rms_norm(x, ffn_norm[layer])
        x = x + swiglu(h, ffn_w_gate[layer], ffn_w_up[layer], ffn_w_down[layer]).astype(x.dtype)
    return rms_norm(x, final_norm)


import jax as _jax
import jax.numpy as _jnp

TWIN_FORMAT = 'train_step'
FWD_PARAMS = ['x', 'mix_norm', 'ffn_norm', 'final_norm', 'ab_w_in', 'lb_logits', 'hg_out_norm', 'ab_w_out', 'pool_w', 'pool_scale', 'ffn_w_gate', 'ffn_w_up', 'ffn_w_down']
TWIN_WEIGHTS = ['mix_norm', 'ffn_norm', 'final_norm', 'ab_w_in', 'lb_logits', 'hg_out_norm', 'ab_w_out', 'pool_w', 'pool_scale', 'ffn_w_gate', 'ffn_w_up', 'ffn_w_down']
TWIN_DIFF_INPUT = 'x'
TWIN_INPUTS = ['x', 'mix_norm', 'ffn_norm', 'final_norm', 'ab_w_in', 'lb_logits', 'hg_out_norm', 'ab_w_out', 'pool_w', 'pool_scale', 'ffn_w_gate', 'ffn_w_up', 'ffn_w_down', 'loss_target', 'm_mix_norm', 'm_ffn_norm', 'm_final_norm', 'm_ab_w_in', 'm_lb_logits', 'm_hg_out_norm', 'm_ab_w_out', 'm_pool_w', 'm_pool_scale', 'm_ffn_w_gate', 'm_ffn_w_up', 'm_ffn_w_down', 'v_mix_norm', 'v_ffn_norm', 'v_final_norm', 'v_ab_w_in', 'v_lb_logits', 'v_hg_out_norm', 'v_ab_w_out', 'v_pool_w', 'v_pool_scale', 'v_ffn_w_gate', 'v_ffn_w_up', 'v_ffn_w_down']
TWIN_OUTPUTS = ['loss', 'grad_x', 'grad_mix_norm', 'grad_ffn_norm', 'grad_final_norm', 'grad_ab_w_in', 'grad_lb_logits', 'grad_hg_out_norm', 'grad_ab_w_out', 'grad_pool_w', 'grad_pool_scale', 'grad_ffn_w_gate', 'grad_ffn_w_up', 'grad_ffn_w_down', 'delta_mix_norm', 'delta_ffn_norm', 'delta_final_norm', 'delta_ab_w_in', 'delta_lb_logits', 'delta_hg_out_norm', 'delta_ab_w_out', 'delta_pool_w', 'delta_pool_scale', 'delta_ffn_w_gate', 'delta_ffn_w_up', 'delta_ffn_w_down', 'new_m_mix_norm', 'new_m_ffn_norm', 'new_m_final_norm', 'new_m_ab_w_in', 'new_m_lb_logits', 'new_m_hg_out_norm', 'new_m_ab_w_out', 'new_m_pool_w', 'new_m_pool_scale', 'new_m_ffn_w_gate', 'new_m_ffn_w_up', 'new_m_ffn_w_down', 'new_v_mix_norm', 'new_v_ffn_norm', 'new_v_final_norm', 'new_v_ab_w_in', 'new_v_lb_logits', 'new_v_hg_out_norm', 'new_v_ab_w_out', 'new_v_pool_w', 'new_v_pool_scale', 'new_v_ffn_w_gate', 'new_v_ffn_w_up', 'new_v_ffn_w_down']
TWIN_LEAF_KINDS = {'loss': 'loss', 'grad_x': 'grad_x', 'grad_mix_norm': 'grad_w', 'grad_ffn_norm': 'grad_w', 'grad_final_norm': 'grad_w', 'grad_ab_w_in': 'grad_w', 'grad_lb_logits': 'grad_w', 'grad_hg_out_norm': 'grad_w', 'grad_ab_w_out': 'grad_w', 'grad_pool_w': 'grad_w', 'grad_pool_scale': 'grad_w', 'grad_ffn_w_gate': 'grad_w', 'grad_ffn_w_up': 'grad_w', 'grad_ffn_w_down': 'grad_w', 'delta_mix_norm': 'delta_w', 'delta_ffn_norm': 'delta_w', 'delta_final_norm': 'delta_w', 'delta_ab_w_in': 'delta_w', 'delta_lb_logits': 'delta_w', 'delta_hg_out_norm': 'delta_w', 'delta_ab_w_out': 'delta_w', 'delta_pool_w': 'delta_w', 'delta_pool_scale': 'delta_w', 'delta_ffn_w_gate': 'delta_w', 'delta_ffn_w_up': 'delta_w', 'delta_ffn_w_down': 'delta_w', 'new_m_mix_norm': 'new_m', 'new_m_ffn_norm': 'new_m', 'new_m_final_norm': 'new_m', 'new_m_ab_w_in': 'new_m', 'new_m_lb_logits': 'new_m', 'new_m_hg_out_norm': 'new_m', 'new_m_ab_w_out': 'new_m', 'new_m_pool_w': 'new_m', 'new_m_pool_scale': 'new_m', 'new_m_ffn_w_gate': 'new_m', 'new_m_ffn_w_up': 'new_m', 'new_m_ffn_w_down': 'new_m', 'new_v_mix_norm': 'new_v', 'new_v_ffn_norm': 'new_v', 'new_v_final_norm': 'new_v', 'new_v_ab_w_in': 'new_v', 'new_v_lb_logits': 'new_v', 'new_v_hg_out_norm': 'new_v', 'new_v_ab_w_out': 'new_v', 'new_v_pool_w': 'new_v', 'new_v_pool_scale': 'new_v', 'new_v_ffn_w_gate': 'new_v', 'new_v_ffn_w_up': 'new_v', 'new_v_ffn_w_down': 'new_v'}


def _forward(args):
    return _fwd_reference(*[args[k] for k in FWD_PARAMS])


def _output_shape():
    def fwd():
        inp = _fwd_setup_inputs(0)
        return _fwd_reference(*[inp[k] for k in FWD_PARAMS])
    out = _jax.eval_shape(fwd)
    return out.shape, out.dtype

N_MICROBATCH = 1
ADAM_LR = 0.001
ADAM_B1 = 0.9
ADAM_B2 = 0.999
ADAM_EPS = 1e-08
ADAM_WD = 0.01
ADAM_STEP = 10
PER_EXAMPLE_BATCH_AXIS = {'x': 0, 'loss_target': 0}
SHARED_INPUTS = []
_WEIGHT_DTYPES = {'mix_norm': _jnp.float32, 'ffn_norm': _jnp.float32, 'final_norm': _jnp.float32, 'ab_w_in': _jnp.float32, 'lb_logits': _jnp.float32, 'hg_out_norm': _jnp.float32, 'ab_w_out': _jnp.float32, 'pool_w': _jnp.float32, 'pool_scale': _jnp.float32, 'ffn_w_gate': _jnp.float32, 'ffn_w_up': _jnp.float32, 'ffn_w_down': _jnp.float32}
MOMENT_SCALE = {'mix_norm': 9.451027e-02, 'ffn_norm': 7.602384e-02, 'final_norm': 3.204288e+01, 'ab_w_in': 5.466755e-02, 'lb_logits': 6.989878e-03, 'hg_out_norm': 2.063747e-01, 'ab_w_out': 7.731737e-02, 'pool_w': 7.451480e-02, 'pool_scale': 2.554693e-01, 'ffn_w_gate': 3.301782e-02, 'ffn_w_up': 3.195655e-02, 'ffn_w_down': 5.296284e-02}


def _to_microbatches(a, axis):
    t = _jnp.moveaxis(a, axis, 0)
    t = t.reshape((N_MICROBATCH, t.shape[0] // N_MICROBATCH) + t.shape[1:])
    return _jnp.moveaxis(t, 1, axis + 1)


def setup_inputs(seed: int = 0) -> dict:
    inp = _fwd_setup_inputs(seed)
    key = _jax.random.fold_in(_jax.random.key(seed), 7919)
    shape, _ = _output_shape()
    out = dict(inp)
    out["loss_target"] = _jax.random.normal(_jax.random.fold_in(key, 0), shape, _jnp.float32)
    for i, name in enumerate(TWIN_WEIGHTS):
        w = inp[name].astype(_jnp.float32)
        if MOMENT_SCALE is None:
            s = _jnp.sqrt(_jnp.mean(_jnp.square(w)) + 1e-30)
        else:
            s = MOMENT_SCALE[name]
        km, kv = _jax.random.split(_jax.random.fold_in(key, i + 1))
        out[name] = w
        out["m_" + name] = s * _jax.random.normal(km, w.shape, _jnp.float32)
        out["v_" + name] = (s * s) * _jax.random.uniform(kv, w.shape, _jnp.float32, 0.5, 1.5)
    if N_MICROBATCH > 1:
        for name, axis in PER_EXAMPLE_BATCH_AXIS.items():
            out[name] = _to_microbatches(out[name], axis)
    return {'x': out['x'], 'mix_norm': out['mix_norm'], 'ffn_norm': out['ffn_norm'], 'final_norm': out['final_norm'], 'ab_w_in': out['ab_w_in'], 'lb_logits': out['lb_logits'], 'hg_out_norm': out['hg_out_norm'], 'ab_w_out': out['ab_w_out'], 'pool_w': out['pool_w'], 'pool_scale': out['pool_scale'], 'ffn_w_gate': out['ffn_w_gate'], 'ffn_w_up': out['ffn_w_up'], 'ffn_w_down': out['ffn_w_down'], 'loss_target': out['loss_target'], 'm_mix_norm': out['m_mix_norm'], 'm_ffn_norm': out['m_ffn_norm'], 'm_final_norm': out['m_final_norm'], 'm_ab_w_in': out['m_ab_w_in'], 'm_lb_logits': out['m_lb_logits'], 'm_hg_out_norm': out['m_hg_out_norm'], 'm_ab_w_out': out['m_ab_w_out'], 'm_pool_w': out['m_pool_w'], 'm_pool_scale': out['m_pool_scale'], 'm_ffn_w_gate': out['m_ffn_w_gate'], 'm_ffn_w_up': out['m_ffn_w_up'], 'm_ffn_w_down': out['m_ffn_w_down'], 'v_mix_norm': out['v_mix_norm'], 'v_ffn_norm': out['v_ffn_norm'], 'v_final_norm': out['v_final_norm'], 'v_ab_w_in': out['v_ab_w_in'], 'v_lb_logits': out['v_lb_logits'], 'v_hg_out_norm': out['v_hg_out_norm'], 'v_ab_w_out': out['v_ab_w_out'], 'v_pool_w': out['v_pool_w'], 'v_pool_scale': out['v_pool_scale'], 'v_ffn_w_gate': out['v_ffn_w_gate'], 'v_ffn_w_up': out['v_ffn_w_up'], 'v_ffn_w_down': out['v_ffn_w_down']}


def _loss(weights, diff, rest, loss_target):
    with _jax.named_scope("forward"):
        args = {**rest, TWIN_DIFF_INPUT: diff, **{k: w.astype(_WEIGHT_DTYPES[k]) for k, w in weights.items()}}
        y = _forward(args)
    with _jax.named_scope("loss_head"):
        err = _jnp.square(y.astype(_jnp.float32) - loss_target)
        return 0.5 * _jnp.sum(_jnp.mean(err, axis=-1)) if err.ndim else 0.5 * err


def _adamw(w, g, m, v):
    m = ADAM_B1 * m + (1.0 - ADAM_B1) * g
    v = ADAM_B2 * v + (1.0 - ADAM_B2) * _jnp.square(g)
    m_hat = m / (1.0 - ADAM_B1 ** ADAM_STEP)
    v_hat = v / (1.0 - ADAM_B2 ** ADAM_STEP)
    delta = -ADAM_LR * (m_hat / (_jnp.sqrt(v_hat) + ADAM_EPS) + ADAM_WD * w)
    return delta, m, v


def reference(x, mix_norm, ffn_norm, final_norm, ab_w_in, lb_logits, hg_out_norm, ab_w_out, pool_w, pool_scale, ffn_w_gate, ffn_w_up, ffn_w_down, loss_target, m_mix_norm, m_ffn_norm, m_final_norm, m_ab_w_in, m_lb_logits, m_hg_out_norm, m_ab_w_out, m_pool_w, m_pool_scale, m_ffn_w_gate, m_ffn_w_up, m_ffn_w_down, v_mix_norm, v_ffn_norm, v_final_norm, v_ab_w_in, v_lb_logits, v_hg_out_norm, v_ab_w_out, v_pool_w, v_pool_scale, v_ffn_w_gate, v_ffn_w_up, v_ffn_w_down):
    given = dict(x=x, mix_norm=mix_norm, ffn_norm=ffn_norm, final_norm=final_norm, ab_w_in=ab_w_in, lb_logits=lb_logits, hg_out_norm=hg_out_norm, ab_w_out=ab_w_out, pool_w=pool_w, pool_scale=pool_scale, ffn_w_gate=ffn_w_gate, ffn_w_up=ffn_w_up, ffn_w_down=ffn_w_down, loss_target=loss_target, m_mix_norm=m_mix_norm, m_ffn_norm=m_ffn_norm, m_final_norm=m_final_norm, m_ab_w_in=m_ab_w_in, m_lb_logits=m_lb_logits, m_hg_out_norm=m_hg_out_norm, m_ab_w_out=m_ab_w_out, m_pool_w=m_pool_w, m_pool_scale=m_pool_scale, m_ffn_w_gate=m_ffn_w_gate, m_ffn_w_up=m_ffn_w_up, m_ffn_w_down=m_ffn_w_down, v_mix_norm=v_mix_norm, v_ffn_norm=v_ffn_norm, v_final_norm=v_final_norm, v_ab_w_in=v_ab_w_in, v_lb_logits=v_lb_logits, v_hg_out_norm=v_hg_out_norm, v_ab_w_out=v_ab_w_out, v_pool_w=v_pool_w, v_pool_scale=v_pool_scale, v_ffn_w_gate=v_ffn_w_gate, v_ffn_w_up=v_ffn_w_up, v_ffn_w_down=v_ffn_w_down)
    weights = {n: given[n] for n in TWIN_WEIGHTS}
    shared = {n: given[n] for n in SHARED_INPUTS}
    per_example = {n: given[n] for n in ['x']}
    grad_fn = _jax.value_and_grad(_loss, argnums=(0, 1))

    def one_microbatch(ex, loss_target):
        ex = dict(ex)
        diff = ex.pop(TWIN_DIFF_INPUT)
        return grad_fn(weights, diff, {**shared, **ex}, loss_target)

    if N_MICROBATCH == 1:
        loss, (grad_w, grad_x) = one_microbatch(per_example, given["loss_target"])
    else:
        def body(carry, xs):
            loss_sum, grad_sum = carry
            l_k, (gw_k, gx_k) = one_microbatch(xs[0], xs[1])
            with _jax.named_scope("update"):
                return (loss_sum + l_k, _jax.tree.map(_jnp.add, grad_sum, gw_k)), gx_k

        init = (_jnp.zeros((), _jnp.float32), _jax.tree.map(_jnp.zeros_like, weights))
        (loss, grad_w), grad_x = _jax.lax.scan(body, init, (per_example, given["loss_target"]))
    with _jax.named_scope("update"):
        delta_w, new_m, new_v = {}, {}, {}
        for n in TWIN_WEIGHTS:
            delta_w[n], new_m[n], new_v[n] = _adamw(weights[n], grad_w[n], given["m_" + n], given["v_" + n])
    return (loss, grad_x, *[grad_w[n] for n in TWIN_WEIGHTS], *[delta_w[n] for n in TWIN_WEIGHTS],
            *[new_m[n] for n in TWIN_WEIGHTS], *[new_v[n] for n in TWIN_WEIGHTS])
```

```python
import functools
import math

import jax
import jax.numpy as jnp
from jax import lax
from jax.experimental import pallas as pl
from jax.experimental.pallas import tpu as pltpu

F32 = jnp.float32
BF16 = jnp.bfloat16
HIGHEST = lax.Precision.HIGHEST
MESH = pl.DeviceIdType.MESH
ANY = pl.BlockSpec(memory_space=pl.ANY)

RMS_EPS = 1e-6
HEAD = 128
HG_CHUNK = 64
HG_MID = HG_CHUNK // 2 - 1
HG_BLOCK = 512
ATT_BLOCK = 256
POOL_WINDOWS = (2, 4, 8, 16)
POOL_HALO = 16
N_CHIPS = 4
ADAM_LR, ADAM_B1, ADAM_B2, ADAM_EPS, ADAM_WD, ADAM_STEP = 0.001, 0.9, 0.999, 1e-08, 0.01, 10
VMEM_LIMIT = 56 * 1024 * 1024

NT_DIMS = (((1,), (1,)), ((), ()))
TN_DIMS = (((0,), (0,)), ((), ()))


def _params(*sem):
    return pltpu.CompilerParams(dimension_semantics=sem, vmem_limit_bytes=VMEM_LIMIT)


def _pick(dim, pref, unit=128):
    best = None
    for t in range(unit, min(dim, pref) + 1, unit):
        if dim % t == 0:
            best = t
    return dim if best is None else best


def _sigmoid(z):
    return 1.0 / (1.0 + jnp.exp(-z))


def _dsilu(a, sg):
    return sg * (1.0 + a * (1.0 - sg))


def _mm_nn(name, a, bs, extras, epilogue, out_dtypes, *, tm=512, tn=512, tk=2048):
    g_n, k_n, n_n = bs[0].shape
    m_n = a.shape[0]
    tm, tn, tk = _pick(m_n, tm, 8), _pick(n_n, tn), _pick(k_n, tk)
    i_n, j_n, kt = m_n // tm, n_n // tn, k_n // tk
    nb, ne, no = len(bs), len(extras), len(out_dtypes)

    def body(*refs):
        a_ref, b_refs, e_refs = refs[0], refs[1:1 + nb], refs[1 + nb:1 + nb + ne]
        o_refs, acc_refs = refs[1 + nb + ne:1 + nb + ne + no], refs[1 + nb + ne + no:]
        k = pl.program_id(3)
        av = a_ref[...]
        prods = [jnp.dot(av, b_ref[...], preferred_element_type=F32) for b_ref in b_refs]

        def finish(accs):
            outs = epilogue(accs, [e[...] for e in e_refs])
            for o_ref, o in zip(o_refs, outs):
                o_ref[...] = o.astype(o_ref.dtype)

        if kt == 1:
            finish(prods)
        else:
            @pl.when(k == 0)
            def _():
                for acc, p in zip(acc_refs, prods):
                    acc[...] = p

            @pl.when(k > 0)
            def _():
                for acc, p in zip(acc_refs, prods):
                    acc[...] += p

            @pl.when(k == kt - 1)
            def _():
                finish([acc[...] for acc in acc_refs])

    in_specs = [pl.BlockSpec((tm, tk), lambda g, i, j, k: (i, g * kt + k))]
    in_specs += [pl.BlockSpec((None, tk, tn), lambda g, i, j, k: (g, k, j)) for _ in bs]
    for e in extras:
        if e.shape[0] == 1:
            in_specs.append(pl.BlockSpec((1, tn), lambda g, i, j, k: (0, g * j_n + j)))
        else:
            in_specs.append(pl.BlockSpec((tm, tn), lambda g, i, j, k: (i, g * j_n + j)))
    out_specs = [pl.BlockSpec((tm, tn), lambda g, i, j, k: (i, g * j_n + j)) for _ in out_dtypes]
    out_shape = [jax.ShapeDtypeStruct((m_n, g_n * n_n), dt) for dt in out_dtypes]
    scratch = [] if kt == 1 else [pltpu.VMEM((tm, tn), F32) for _ in bs]
    return pl.pallas_call(
        body, name=name, grid=(g_n, i_n, j_n, kt), in_specs=in_specs, out_specs=out_specs, out_shape=out_shape,
        scratch_shapes=scratch, compiler_params=_params("parallel", "parallel", "parallel", "arbitrary"),
    )(a, *bs, *extras)


def _mm_nt(name, pairs, extras, epilogue, out_dtypes, *, tm=512, to=512, tr=2048):
    g_n, kd, n_n = pairs[0][1].shape
    m_n = pairs[0][0].shape[0]
    tm, to, tr = _pick(m_n, tm, 8), _pick(kd, to), _pick(n_n, tr)
    i_n, j_n, rt = m_n // tm, kd // to, n_n // tr
    npairs, ne, no = len(pairs), len(extras), len(out_dtypes)

    def body(*refs):
        ab_refs, e_refs = refs[:2 * npairs], refs[2 * npairs:2 * npairs + ne]
        o_refs, acc_refs = refs[2 * npairs + ne:2 * npairs + ne + no], refs[2 * npairs + ne + no:]
        r = pl.program_id(3)
        prod = None
        for p in range(npairs):
            t = lax.dot_general(ab_refs[2 * p][...], ab_refs[2 * p + 1][...], NT_DIMS, preferred_element_type=F32)
            prod = t if prod is None else prod + t

        def finish(acc):
            outs = epilogue(acc, [e[...] for e in e_refs])
            for o_ref, o in zip(o_refs, outs):
                o_ref[...] = o.astype(o_ref.dtype)

        if rt == 1:
            finish(prod)
        else:
            acc = acc_refs[0]

            @pl.when(r == 0)
            def _():
                acc[...] = prod

            @pl.when(r > 0)
            def _():
                acc[...] += prod

            @pl.when(r == rt - 1)
            def _():
                finish(acc[...])

    in_specs, args = [], []
    for a, b in pairs:
        in_specs.append(pl.BlockSpec((tm, tr), lambda g, i, j, r: (i, g * rt + r)))
        in_specs.append(pl.BlockSpec((None, to, tr), lambda g, i, j, r: (g, j, r)))
        args += [a, b]
    in_specs += [pl.BlockSpec((tm, to), lambda g, i, j, r: (i, g * j_n + j)) for _ in extras]
    out_specs = [pl.BlockSpec((tm, to), lambda g, i, j, r: (i, g * j_n + j)) for _ in out_dtypes]
    out_shape = [jax.ShapeDtypeStruct((m_n, g_n * kd), dt) for dt in out_dtypes]
    scratch = [] if rt == 1 else [pltpu.VMEM((tm, to), F32)]
    return pl.pallas_call(
        body, name=name, grid=(g_n, i_n, j_n, rt), in_specs=in_specs, out_specs=out_specs, out_shape=out_shape,
        scratch_shapes=scratch, compiler_params=_params("parallel", "parallel", "parallel", "arbitrary"),
    )(*args, *extras)


def _mm_tn(name, a, b, g_n, out_shape, out_block, out_index, *, tki, tn, tm=512, into=None):
    m_n = a.shape[0]
    k_n, n_n = a.shape[1] // g_n, b.shape[1] // g_n
    tm = _pick(m_n, tm, 8)
    i_n, j_n, mt = k_n // tki, n_n // tn, m_n // tm
    assert k_n % tki == 0 and n_n % tn == 0

    def body(*refs):
        a_ref, b_ref = refs[0], refs[1]
        o_ref, acc = refs[-2], refs[-1]
        m = pl.program_id(3)
        prod = lax.dot_general(a_ref[...], b_ref[...], TN_DIMS, preferred_element_type=F32)

        @pl.when(m == 0)
        def _():
            acc[...] = prod

        @pl.when(m > 0)
        def _():
            acc[...] += prod

        @pl.when(m == mt - 1)
        def _():
            o_ref[...] = acc[...].reshape(o_ref.shape)

    in_specs = [pl.BlockSpec((tm, tki), lambda g, i, j, m: (m, g * i_n + i)),
                pl.BlockSpec((tm, tn), lambda g, i, j, m: (m, g * j_n + j))]
    args = [a, b]
    aliases = {}
    if into is not None:
        in_specs.append(ANY)
        args.append(into)
        aliases = {2: 0}
    return pl.pallas_call(
        body, name=name, grid=(g_n, i_n, j_n, mt), in_specs=in_specs,
        out_specs=pl.BlockSpec(out_block, lambda g, i, j, m: out_index(g, i, j)),
        out_shape=jax.ShapeDtypeStruct(out_shape, F32), scratch_shapes=[pltpu.VMEM((tki, tn), F32)],
        input_output_aliases=aliases,
        compiler_params=_params("parallel", "parallel", "parallel", "arbitrary"),
    )(*args)


def _rstd(xv):
    return lax.rsqrt(jnp.mean(xv * xv, axis=-1, keepdims=True) + RMS_EPS)


def _rms_bwd_rows(dh, xv, gain, r):
    dy = dh * gain
    c = jnp.mean(dy * xv, axis=-1, keepdims=True)
    return r * dy - xv * (r * r * r) * c, dh * xv * r


def _fold8(t):
    return t.reshape(t.shape[0] // 8, 8, t.shape[1]).sum(axis=0)


def _rms_fwd(name, x, gain, tm=256):
    s_n, d_n = x.shape
    tm = _pick(s_n, tm, 8)

    def body(x_ref, g_ref, h_ref):
        xv = x_ref[...]
        h_ref[...] = (xv * _rstd(xv) * g_ref[...]).astype(h_ref.dtype)

    return pl.pallas_call(
        body, name=name, grid=(s_n // tm,),
        in_specs=[pl.BlockSpec((tm, d_n), lambda i: (i, 0)), pl.BlockSpec((1, d_n), lambda i: (0, 0))],
        out_specs=pl.BlockSpec((tm, d_n), lambda i: (i, 0)), out_shape=jax.ShapeDtypeStruct((s_n, d_n), BF16),
        compiler_params=_params("parallel"),
    )(x, gain)


def _rms_bwd(name, dh, x, gain, dres, tm=256):
    s_n, d_n = x.shape
    tm = _pick(s_n, tm, 8)
    nblk = s_n // tm

    def body(dh_ref, x_ref, g_ref, dres_ref, dx_ref, dg_ref, acc):
        i = pl.program_id(0)

        @pl.when(i == 0)
        def _():
            acc[...] = jnp.zeros_like(acc)

        xv = x_ref[...]
        dxv, dgt = _rms_bwd_rows(dh_ref[...].astype(F32), xv, g_ref[...], _rstd(xv))
        dx_ref[...] = dres_ref[...] + dxv
        acc[...] += _fold8(dgt)

        @pl.when(i == nblk - 1)
        def _():
            dg_ref[...] = jnp.sum(acc[...], axis=0, keepdims=True)

    row = pl.BlockSpec((tm, d_n), lambda i: (i, 0))
    vec = pl.BlockSpec((1, d_n), lambda i: (0, 0))
    return pl.pallas_call(
        body, name=name, grid=(nblk,), in_specs=[row, row, vec, row], out_specs=[row, vec],
        out_shape=[jax.ShapeDtypeStruct((s_n, d_n), F32), jax.ShapeDtypeStruct((1, d_n), F32)],
        scratch_shapes=[pltpu.VMEM((8, d_n), F32)], compiler_params=_params("arbitrary"),
    )(dh, x, gain, dres)


def _loss_bwd(name, x, target, gain, tm=256):
    s_n, d_n = x.shape
    tm = _pick(s_n, tm, 8)
    nblk = s_n // tm

    def body(x_ref, t_ref, g_ref, dx_ref, dg_ref, loss_ref, acc, lacc):
        i = pl.program_id(0)

        @pl.when(i == 0)
        def _():
            acc[...] = jnp.zeros_like(acc)
            lacc[...] = jnp.zeros_like(lacc)

        xv = x_ref[...]
        gain = g_ref[...]
        r = _rstd(xv)
        diff = xv * r * gain - t_ref[...]
        lacc[...] += _fold8(diff * diff)
        dxv, dgt = _rms_bwd_rows(diff * (1.0 / d_n), xv, gain, r)
        dx_ref[...] = dxv
        acc[...] += _fold8(dgt)

        @pl.when(i == nblk - 1)
        def _():
            dg_ref[...] = jnp.sum(acc[...], axis=0, keepdims=True)
            loss_ref[...] = jnp.sum(lacc[...], keepdims=True) * (0.5 / d_n)

    row = pl.BlockSpec((tm, d_n), lambda i: (i, 0))
    vec = pl.BlockSpec((1, d_n), lambda i: (0, 0))
    return pl.pallas_call(
        body, name=name, grid=(nblk,), in_specs=[row, row, vec],
        out_specs=[row, vec, pl.BlockSpec((1, 1), lambda i: (0, 0))],
        out_shape=[jax.ShapeDtypeStruct((s_n, d_n), F32), jax.ShapeDtypeStruct((1, d_n), F32),
                   jax.ShapeDtypeStruct((1, 1), F32)],
        scratch_shapes=[pltpu.VMEM((8, d_n), F32), pltpu.VMEM((8, d_n), F32)], compiler_params=_params("arbitrary"),
    )(x, target, gain)


def _pool_counts(t_idx, d_n):
    grp = d_n // len(POOL_WINDOWS)
    lane = lax.broadcasted_iota(jnp.int32, (1, d_n), 1) // grp
    win = jnp.zeros((1, d_n), jnp.int32)
    for gi, w in enumerate(POOL_WINDOWS):
        win = jnp.where(lane == gi, w, win)
    return jnp.minimum(t_idx + 1, win).astype(F32), lane


def _window_sums(rows, lane, backward):
    n = rows.shape[0]
    out = rows
    acc = rows
    width = 1
    for gi in range(len(POOL_WINDOWS)):
        shift = (n - width) if backward else width
        acc = acc + pltpu.roll(acc, shift, 0)
        width *= 2
        out = jnp.where(lane >= gi, acc, out)
    return out


def _pool_fwd(name, x, gain, tm=256):
    s_n, d_n = x.shape
    tm = _pick(s_n, tm, POOL_HALO)
    per = tm // POOL_HALO

    def body(x_ref, halo_ref, g_ref, o_ref):
        i = pl.program_id(0)
        halo = jnp.where(i == 0, 0.0, halo_ref[...])
        rows = jnp.concatenate([halo, x_ref[...]], axis=0)
        h = rows * _rstd(rows) * g_ref[...]
        t_idx = i * tm - POOL_HALO + lax.broadcasted_iota(jnp.int32, (tm + POOL_HALO, 1), 0)
        cnt, lane = _pool_counts(t_idx, d_n)
        pooled = _window_sums(h, lane, False) / cnt - h
        o_ref[...] = pooled[POOL_HALO:, :].astype(o_ref.dtype)

    return pl.pallas_call(
        body, name=name, grid=(s_n // tm,),
        in_specs=[pl.BlockSpec((tm, d_n), lambda i: (i, 0)),
                  pl.BlockSpec((POOL_HALO, d_n), lambda i: (jnp.maximum(i * per - 1, 0), 0)),
                  pl.BlockSpec((1, d_n), lambda i: (0, 0))],
        out_specs=pl.BlockSpec((tm, d_n), lambda i: (i, 0)), out_shape=jax.ShapeDtypeStruct((s_n, d_n), BF16),
        compiler_params=_params("parallel"),
    )(x, x, gain)


def _pool_bwd(name, dpooled, x, gain, dres, tm=256):
    s_n, d_n = x.shape
    tm = _pick(s_n, tm, POOL_HALO)
    per = tm // POOL_HALO
    nblk = s_n // tm
    last_halo = s_n // POOL_HALO - 1

    def body(dp_ref, halo_ref, x_ref, g_ref, dres_ref, dx_ref, dg_ref, acc):
        i = pl.program_id(0)

        @pl.when(i == 0)
        def _():
            acc[...] = jnp.zeros_like(acc)

        halo = jnp.where(i == nblk - 1, 0.0, halo_ref[...])
        rows = jnp.concatenate([dp_ref[...], halo], axis=0)
        t_idx = i * tm + lax.broadcasted_iota(jnp.int32, (tm + POOL_HALO, 1), 0)
        cnt, lane = _pool_counts(t_idx, d_n)
        dh = (_window_sums(rows / cnt, lane, True) - rows)[:tm, :]
        xv = x_ref[...]
        dxv, dgt = _rms_bwd_rows(dh, xv, g_ref[...], _rstd(xv))
        dx_ref[...] = dres_ref[...] + dxv
        acc[...] += _fold8(dgt)

        @pl.when(i == nblk - 1)
        def _():
            dg_ref[...] = jnp.sum(acc[...], axis=0, keepdims=True)

    row = pl.BlockSpec((tm, d_n), lambda i: (i, 0))
    vec = pl.BlockSpec((1, d_n), lambda i: (0, 0))
    return pl.pallas_call(
        body, name=name, grid=(nblk,),
        in_specs=[row, pl.BlockSpec((POOL_HALO, d_n), lambda i: (jnp.minimum((i + 1) * per, last_halo), 0)),
                  row, vec, row],
        out_specs=[row, vec],
        out_shape=[jax.ShapeDtypeStruct((s_n, d_n), F32), jax.ShapeDtypeStruct((1, d_n), F32)],
        scratch_shapes=[pltpu.VMEM((8, d_n), F32)], compiler_params=_params("arbitrary"),
    )(dpooled, dpooled, x, gain, dres)


def _scale_bwd(name, dx, mixed, scale, tm=256):
    s_n, d_n = dx.shape
    tm = _pick(s_n, tm, 8)
    nblk = s_n // tm

    def body(dx_ref, mx_ref, sc_ref, dm_ref, ds_ref, acc):
        i = pl.program_id(0)

        @pl.when(i == 0)
        def _():
            acc[...] = jnp.zeros_like(acc)

        dxv = dx_ref[...]
        dm_ref[...] = (dxv * sc_ref[...]).astype(dm_ref.dtype)
        acc[...] += _fold8(dxv * mx_ref[...])

        @pl.when(i == nblk - 1)
        def _():
            ds_ref[...] = jnp.sum(acc[...], axis=0, keepdims=True)

    row = pl.BlockSpec((tm, d_n), lambda i: (i, 0))
    vec = pl.BlockSpec((1, d_n), lambda i: (0, 0))
    return pl.pallas_call(
        body, name=name, grid=(nblk,), in_specs=[row, row, vec], out_specs=[row, vec],
        out_shape=[jax.ShapeDtypeStruct((s_n, d_n), BF16), jax.ShapeDtypeStruct((1, d_n), F32)],
        scratch_shapes=[pltpu.VMEM((8, d_n), F32)], compiler_params=_params("arbitrary"),
    )(dx, mixed, scale)


def _hg_gates(qa, fa, lb):
    sig = _sigmoid(fa)
    f = lb + (1.0 - lb) * sig
    sq = _sigmoid(qa)
    return sig, f, jnp.log(f), 1.0 - f, sq, qa * sq


def _hg_chunk_terms(q, k, g, lincl):
    gc = jnp.dot(lincl, g, precision=HIGHEST, preferred_element_type=F32)
    glast = gc[HG_CHUNK - 1:HG_CHUNK, :]
    gm = gc[HG_MID:HG_MID + 1, :]
    e_q, e_l = jnp.exp(gc), jnp.exp(glast - gc)
    e_m, e_mi = jnp.exp(gc - gm), jnp.exp(gm - gc)
    return glast, (e_q, e_l, e_m, e_mi), (q * e_q, k * e_l, q * e_m, k * e_mi)


def _hg_fwd(name, proj, lb_logits, hgain, n_heads):
    s_n = proj.shape[0]
    tb = _pick(s_n, HG_BLOCK, HG_CHUNK)
    nblk, ncb = s_n // tb, tb // HG_CHUNK
    c_n = HG_CHUNK

    def body(qa_ref, fa_ref, ia_ref, ga_ref, l_ref, gn_ref, oraw_ref, oa_ref, st_ref, state):
        @pl.when(pl.program_id(1) == 0)
        def _():
            state[...] = jnp.zeros_like(state)

        lv = l_ref[...]
        lb = _sigmoid(lv[0:1, :] - lv[1:2, :])
        row = lax.broadcasted_iota(jnp.int32, (c_n, c_n), 0)
        col = lax.broadcasted_iota(jnp.int32, (c_n, c_n), 1)
        causal = col <= row
        lincl = causal.astype(F32)
        gn = gn_ref[...]

        def chunk(ci, carry):
            sl = pl.ds(pl.multiple_of(ci * c_n, c_n), c_n)
            _, _, g, k, _, q = _hg_gates(qa_ref[sl, :], fa_ref[sl, :], lb)
            glast, _, (qe, kl, qm, km) = _hg_chunk_terms(q, k, g, lincl)
            vb = ia_ref[sl, :].astype(BF16)
            att = lax.dot_general(qm.astype(BF16), km.astype(BF16), NT_DIMS, preferred_element_type=F32)
            att = jnp.where(causal, att, 0.0).astype(BF16)
            st = state[...]
            o = lax.dot_general(qe.astype(BF16), st.astype(BF16), NT_DIMS, preferred_element_type=F32)
            o = o + jnp.dot(att, vb, preferred_element_type=F32)
            st_ref[ci] = st
            state[...] = st * jnp.exp(glast) + lax.dot_general(vb, kl.astype(BF16), TN_DIMS,
                                                               preferred_element_type=F32)
            oraw_ref[sl, :] = o
            ga = ga_ref[sl, :]
            oa_ref[sl, :] = (o * _rstd(o) * gn * (ga * _sigmoid(ga))).astype(oa_ref.dtype)
            return carry

        lax.fori_loop(0, ncb, chunk, 0)

    h_n = n_heads
    blk = lambda off: pl.BlockSpec((tb, HEAD), lambda h, c: (c, off + h))
    return pl.pallas_call(
        body, name=name, grid=(h_n, nblk),
        in_specs=[blk(0), blk(h_n), blk(2 * h_n), blk(3 * h_n),
                  pl.BlockSpec((2, HEAD), lambda h, c: (0, h)), pl.BlockSpec((1, HEAD), lambda h, c: (0, 0))],
        out_specs=[blk(0), blk(0), pl.BlockSpec((None, ncb, HEAD, HEAD), lambda h, c: (h, c, 0, 0))],
        out_shape=[jax.ShapeDtypeStruct((s_n, h_n * HEAD), F32), jax.ShapeDtypeStruct((s_n, h_n * HEAD), BF16),
                   jax.ShapeDtypeStruct((h_n, s_n // c_n, HEAD, HEAD), F32)],
        scratch_shapes=[pltpu.VMEM((HEAD, HEAD), F32)], compiler_params=_params("parallel", "arbitrary"),
    )(proj, proj, proj, proj, lb_logits, hgain)


def _hg_bwd(name, proj, dcat, oraw, states, lb_logits, hgain, n_heads):
    s_n = proj.shape[0]
    tb = _pick(s_n, HG_BLOCK, HG_CHUNK)
    nblk, ncb = s_n // tb, tb // HG_CHUNK
    c_n = HG_CHUNK
    h_n = n_heads

    def body(qa_ref, fa_ref, ia_ref, ga_ref, doa_ref, oraw_ref, st_ref, l_ref, gn_ref,
             dqa_ref, dfa_ref, dia_ref, dga_ref, dl_ref, dgn_ref, dstate, dlb_acc, dgn_acc):
        h, c = pl.program_id(0), pl.program_id(1)

        @pl.when(c == 0)
        def _():
            dstate[...] = jnp.zeros_like(dstate)
            dlb_acc[...] = jnp.zeros_like(dlb_acc)

        @pl.when((c == 0) & (h == 0))
        def _():
            dgn_acc[...] = jnp.zeros_like(dgn_acc)

        lv = l_ref[...]
        lb = _sigmoid(lv[0:1, :] - lv[1:2, :])
        row = lax.broadcasted_iota(jnp.int32, (c_n, c_n), 0)
        col = lax.broadcasted_iota(jnp.int32, (c_n, c_n), 1)
        causal = col <= row
        lincl = causal.astype(F32)
        uincl = (col >= row).astype(F32)
        is_last = lax.broadcasted_iota(jnp.int32, (c_n, 1), 0) == c_n - 1
        gn = gn_ref[...]

        def chunk(idx, carry):
            ci = ncb - 1 - idx
            sl = pl.ds(pl.multiple_of(ci * c_n, c_n), c_n)
            qa = qa_ref[sl, :]
            sig, f, g, k, sq, q = _hg_gates(qa, fa_ref[sl, :], lb)
            glast, (e_q, e_l, e_m, e_mi), (qe, kl, qm, km) = _hg_chunk_terms(q, k, g, lincl)
            v = ia_ref[sl, :]
            vb = v.astype(BF16)
            qmb, kmb, qeb, klb = qm.astype(BF16), km.astype(BF16), qe.astype(BF16), kl.astype(BF16)
            att = lax.dot_general(qmb, kmb, NT_DIMS, preferred_element_type=F32)
            attb = jnp.where(causal, att, 0.0).astype(BF16)

            o = oraw_ref[sl, :]
            ga = ga_ref[sl, :]
            sg = _sigmoid(ga)
            r = _rstd(o)
            doa = doa_ref[sl, :]
            dn = doa * (ga * sg)
            dga_ref[sl, :] = (doa * (o * r * gn) * _dsilu(ga, sg)).astype(dga_ref.dtype)
            do, dgt = _rms_bwd_rows(dn, o, gn, r)
            dgn_acc[...] += dgt
            dob = do.astype(BF16)

            st0 = st_ref[ci]
            ds1 = dstate[...]
            st0b, ds1b = st0.astype(BF16), ds1.astype(BF16)
            datt = lax.dot_general(dob, vb, NT_DIMS, preferred_element_type=F32)
            dattb = jnp.where(causal, datt, 0.0).astype(BF16)
            dv = lax.dot_general(attb, dob, TN_DIMS, preferred_element_type=F32)
            dv = dv + lax.dot_general(klb, ds1b, NT_DIMS, preferred_element_type=F32)
            dqm = jnp.dot(dattb, kmb, preferred_element_type=F32)
            dkm = lax.dot_general(dattb, qmb, TN_DIMS, preferred_element_type=F32)
            dqe = jnp.dot(dob, st0b, preferred_element_type=F32)
            dkl = jnp.dot(vb, ds1b, preferred_element_type=F32)
            eg = jnp.exp(glast)
            dstate[...] = ds1 * eg + lax.dot_general(dob, qeb, TN_DIMS, preferred_element_type=F32)
            dq = dqm * e_m + dqe * e_q
            dk = dkm * e_mi + dkl * e_l
            dgc = dqm * qmb.astype(F32) - dkm * kmb.astype(F32) + dqe * qe - dkl * kl
            dglast = jnp.sum(dkl * kl, axis=0, keepdims=True) + eg * jnp.sum(ds1 * st0, axis=0, keepdims=True)
            dgc = dgc + jnp.where(is_last, dglast, 0.0)
            dg = jnp.dot(uincl, dgc, precision=HIGHEST, preferred_element_type=F32)
            df = dg / f - dk
            dfa_ref[sl, :] = (df * (1.0 - lb) * sig * (1.0 - sig)).astype(dfa_ref.dtype)
            dlb_acc[...] += df * (1.0 - sig)
            dqa_ref[sl, :] = (dq * _dsilu(qa, sq)).astype(dqa_ref.dtype)
            dia_ref[sl, :] = dv.astype(dia_ref.dtype)
            return carry

        lax.fori_loop(0, ncb, chunk, 0)

        @pl.when(c == nblk - 1)
        def _():
            dl0 = jnp.sum(dlb_acc[...], axis=0, keepdims=True) * lb * (1.0 - lb)
            first = lax.broadcasted_iota(jnp.int32, (2, HEAD), 0) == 0
            dl_ref[...] = jnp.where(first, dl0, -dl0)

        @pl.when((c == nblk - 1) & (h == h_n - 1))
        def _():
            dgn_ref[...] = jnp.sum(dgn_acc[...], axis=0, keepdims=True)

    blk = lambda off: pl.BlockSpec((tb, HEAD), lambda h, c: (nblk - 1 - c, off + h))
    out_act = jax.ShapeDtypeStruct((s_n, h_n * HEAD), BF16)
    return pl.pallas_call(
        body, name=name, grid=(h_n, nblk),
        in_specs=[blk(0), blk(h_n), blk(2 * h_n), blk(3 * h_n), blk(0), blk(0),
                  pl.BlockSpec((None, ncb, HEAD, HEAD), lambda h, c: (h, nblk - 1 - c, 0, 0)),
                  pl.BlockSpec((2, HEAD), lambda h, c: (0, h)), pl.BlockSpec((1, HEAD), lambda h, c: (0, 0))],
        out_specs=[blk(0), blk(0), blk(0), blk(0), pl.BlockSpec((2, HEAD), lambda h, c: (0, h)),
                   pl.BlockSpec((1, HEAD), lambda h, c: (0, 0))],
        out_shape=[out_act, out_act, out_act, out_act, jax.ShapeDtypeStruct((2, h_n * HEAD), F32),
                   jax.ShapeDtypeStruct((1, HEAD), F32)],
        scratch_shapes=[pltpu.VMEM((HEAD, HEAD), F32), pltpu.VMEM((c_n, HEAD), F32), pltpu.VMEM((c_n, HEAD), F32)],
        compiler_params=_params("arbitrary", "arbitrary"),
    )(proj, proj, proj, proj, dcat, oraw, states, lb_logits, hgain)


def _split_dot(t, ones_b):
    hi = t.astype(BF16)
    lo = (t - hi.astype(F32)).astype(BF16)
    return jnp.dot(hi, ones_b, preferred_element_type=F32) + jnp.dot(lo, ones_b, preferred_element_type=F32)


def _sb_logits(q, kb, scale):
    z = lax.dot_general(q, kb, NT_DIMS, preferred_element_type=F32) * scale
    sp = jnp.maximum(z, 0.0) + jnp.log(1.0 + jnp.exp(-jnp.abs(z)))
    return z - sp, -sp


def _att_fwd(name, projb, n_heads):
    s_n = projb.shape[0]
    t_n = _pick(s_n, ATT_BLOCK, 8)
    h_n = n_heads
    scale = 1.0 / math.sqrt(HEAD)

    def body(q_ref, k_ref, v_ref, o_ref, lt_ref):
        i = pl.program_id(1)
        q = q_ref[...]
        row = lax.broadcasted_iota(jnp.int32, (t_n, t_n), 0)
        col = lax.broadcasted_iota(jnp.int32, (t_n, t_n), 1)
        after = (row > col).astype(BF16)

        def step(jj, carry):
            acc, run = carry
            j = i - jj
            sl = pl.ds(pl.multiple_of(j * t_n, t_n), t_n)
            ls, lk = _sb_logits(q, k_ref[sl, :], scale)
            valid = (j * t_n + col) < (i * t_n + row)
            lk = jnp.where(valid, lk, 0.0)
            later = run + _split_dot(lk, after)
            w = jnp.where(valid, jnp.exp(ls + later), 0.0)
            acc = acc + jnp.dot(w.astype(BF16), v_ref[sl, :], preferred_element_type=F32)
            return acc, run + jnp.sum(lk, axis=1, keepdims=True)

        acc, run = lax.fori_loop(0, i + 1, step, (jnp.zeros((t_n, HEAD), F32), jnp.zeros((t_n, 1), F32)))
        o_ref[...] = acc.astype(o_ref.dtype)
        lt_ref[...] = jnp.broadcast_to(run, (t_n, HEAD))

    full = lambda off: pl.BlockSpec((s_n, HEAD), lambda h, i: (0, off + h))
    tile = lambda off: pl.BlockSpec((t_n, HEAD), lambda h, i: (i, off + h))
    return pl.pallas_call(
        body, name=name, grid=(h_n, s_n // t_n), in_specs=[tile(4 * h_n), full(5 * h_n), full(6 * h_n)],
        out_specs=[tile(0), tile(0)],
        out_shape=[jax.ShapeDtypeStruct((s_n, h_n * HEAD), BF16), jax.ShapeDtypeStruct((s_n, h_n * HEAD), F32)],
        compiler_params=_params("parallel", "arbitrary"),
    )(projb, projb, projb)


def _att_bwd(name, projb, dcat, ltot, n_heads):
    s_n = projb.shape[0]
    t_n = _pick(s_n, ATT_BLOCK, 8)
    h_n = n_heads
    scale = 1.0 / math.sqrt(HEAD)

    def body(q_ref, k_ref, v_ref, do_ref, lt_ref, dq_ref, dk_ref, dv_ref):
        i = pl.program_id(1)

        @pl.when(i == 0)
        def _():
            dk_ref[...] = jnp.zeros_like(dk_ref)
            dv_ref[...] = jnp.zeros_like(dv_ref)

        q = q_ref[...]
        dob = do_ref[...].astype(BF16)
        ltv = lt_ref[...][:, 0:1]
        row = lax.broadcasted_iota(jnp.int32, (t_n, t_n), 0)
        col = lax.broadcasted_iota(jnp.int32, (t_n, t_n), 1)
        upto = (row <= col).astype(BF16)
        before = (row < col).astype(BF16)

        def step(j, carry):
            dq, psum, esum = carry
            sl = pl.ds(pl.multiple_of(j * t_n, t_n), t_n)
            kb, vb = k_ref[sl, :], v_ref[sl, :]
            ls, lk = _sb_logits(q, kb, scale)
            valid = (j * t_n + col) < (i * t_n + row)
            lk = jnp.where(valid, lk, 0.0)
            later = ltv - psum - _split_dot(lk, upto)
            w = jnp.where(valid, jnp.exp(ls + later), 0.0)
            dw = lax.dot_general(dob, vb, NT_DIMS, preferred_element_type=F32)
            e = dw * w
            earlier = esum + _split_dot(e, before)
            sg = jnp.exp(ls)
            dz = jnp.where(valid, (e * (1.0 - sg) - earlier * sg) * scale, 0.0).astype(BF16)
            dq = dq + jnp.dot(dz, kb, preferred_element_type=F32)
            dk_ref[sl, :] += lax.dot_general(dz, q, TN_DIMS, preferred_element_type=F32)
            dv_ref[sl, :] += lax.dot_general(w.astype(BF16), dob, TN_DIMS, preferred_element_type=F32)
            return dq, psum + jnp.sum(lk, axis=1, keepdims=True), esum + jnp.sum(e, axis=1, keepdims=True)

        zero = jnp.zeros((t_n, 1), F32)
        dq, _, _ = lax.fori_loop(0, i + 1, step, (jnp.zeros((t_n, HEAD), F32), zero, zero))
        dq_ref[...] = dq.astype(dq_ref.dtype)

    full = lambda off: pl.BlockSpec((s_n, HEAD), lambda h, i: (0, off + h))
    tile = lambda off: pl.BlockSpec((t_n, HEAD), lambda h, i: (i, off + h))
    return pl.pallas_call(
        body, name=name, grid=(h_n, s_n // t_n),
        in_specs=[tile(4 * h_n), full(5 * h_n), full(6 * h_n), tile(h_n), tile(0)],
        out_specs=[tile(0), full(0), full(0)],
        out_shape=[jax.ShapeDtypeStruct((s_n, h_n * HEAD), BF16), jax.ShapeDtypeStruct((s_n, h_n * HEAD), F32),
                   jax.ShapeDtypeStruct((s_n, h_n * HEAD), F32)],
        compiler_params=_params("arbitrary", "arbitrary"),
    )(projb, projb, projb, dcat, ltot)


def _adamw(name, w, g, m, v, tr=256):
    r_n, c_n = w.shape
    tr = _pick(r_n, tr, 8)
    c1 = 1.0 - ADAM_B1 ** ADAM_STEP
    c2 = 1.0 - ADAM_B2 ** ADAM_STEP

    def body(w_ref, g_ref, m_ref, v_ref, d_ref, nm_ref, nv_ref):
        gv = g_ref[...]
        nm = ADAM_B1 * m_ref[...] + (1.0 - ADAM_B1) * gv
        nv = ADAM_B2 * v_ref[...] + (1.0 - ADAM_B2) * (gv * gv)
        d_ref[...] = -ADAM_LR * ((nm / c1) / (jnp.sqrt(nv / c2) + ADAM_EPS) + ADAM_WD * w_ref[...])
        nm_ref[...] = nm
        nv_ref[...] = nv

    blk = pl.BlockSpec((tr, c_n), lambda i: (i, 0))
    sds = jax.ShapeDtypeStruct((r_n, c_n), F32)
    return pl.pallas_call(
        body, name=name, grid=(r_n // tr,), in_specs=[blk] * 4, out_specs=[blk] * 3, out_shape=[sds] * 3,
        compiler_params=_params("parallel"),
    )(w, g, m, v)


def _adamw_nd(name, w, g, m, v):
    shape = w.shape
    flat = lambda t: t.reshape(-1, shape[-1])
    return tuple(t.reshape(shape) for t in _adamw(name, flat(w), flat(g.reshape(shape)), flat(m), flat(v)))


def _mesh_pos():
    x, y, c = lax.axis_index("x"), lax.axis_index("y"), lax.axis_index("c")
    chips = [(1 - x, y), (x, 1 - y), (1 - x, 1 - y)]
    return x, y, c, chips, 2 * x + y, [2 * cx + cy for cx, cy in chips]


class _Unit:
    def __init__(self, shard_shape, axis, half_axis):
        self.shard_shape = tuple(shard_shape)
        self.axis = axis
        self.half_axis = half_axis
        self.full_shape = tuple(n * N_CHIPS if a == axis else n for a, n in enumerate(shard_shape))
        self.half_shape = tuple(n // 2 if a == half_axis else n for a, n in enumerate(shard_shape))

    def _window(self, ref, k, c, with_slab):
        idx = []
        for a, n in enumerate(self.shard_shape):
            start, size = 0, n
            if a == self.half_axis:
                size = n // 2
                start = c * size
            if with_slab and a == self.axis:
                start = start + k * n
            idx.append(pl.ds(start, size))
        return ref.at[tuple(idx)]

    def shard_half(self, ref, c):
        return self._window(ref, 0, c, False)

    def full_half(self, ref, k, c):
        return self._window(ref, k, c, True)

    def full_slab(self, ref, k):
        return ref.at[tuple(pl.ds(k * n, n) if a == self.axis else pl.ds(0, n)
                            for a, n in enumerate(self.shard_shape))]


def _gather_weights(units, shards, scale_shard):
    nu = len(units)
    ps = scale_shard.shape[1]

    def body(*refs):
        ins, sc_in = refs[:nu], refs[nu]
        outs, sc_out = refs[nu + 1:2 * nu + 1], refs[2 * nu + 1]
        send1, recv1, send2, recv2, send3, recv3, lsem = refs[2 * nu + 2:]
        x, y, c, chips, me, others = _mesh_pos()
        local = [pltpu.make_async_copy(ins[u], units[u].full_slab(outs[u], me), lsem.at[u]) for u in range(nu)]
        local.append(pltpu.make_async_copy(sc_in, sc_out.at[:, pl.ds(me * ps, ps)], lsem.at[nu]))
        for cp in local:
            cp.start()
        sends = []
        for u in range(nu):
            for j, chip in enumerate(chips):
                sends.append(pltpu.make_async_remote_copy(
                    src_ref=units[u].shard_half(ins[u], c), dst_ref=units[u].full_half(outs[u], me, c),
                    send_sem=send1.at[3 * u + j], recv_sem=recv1.at[3 * u + j],
                    device_id=(*chip, c), device_id_type=MESH))
        for j, chip in enumerate(chips):
            sends.append(pltpu.make_async_remote_copy(
                src_ref=sc_in, dst_ref=sc_out.at[:, pl.ds(me * ps, ps)], send_sem=send3.at[j], recv_sem=recv3.at[j],
                device_id=(*chip, c), device_id_type=MESH))
        for cp in sends:
            cp.start()
        for u in range(nu):
            for j in range(3):
                landed = units[u].full_half(outs[u], others[j], c)
                pltpu.make_async_remote_copy(
                    src_ref=landed, dst_ref=landed, send_sem=send1.at[3 * u + j], recv_sem=recv1.at[3 * u + j],
                    device_id=(x, y, c), device_id_type=MESH).wait_recv()
                fwd = pltpu.make_async_remote_copy(
                    src_ref=landed, dst_ref=landed, send_sem=send2.at[3 * u + j], recv_sem=recv2.at[3 * u + j],
                    device_id=(x, y, 1 - c), device_id_type=MESH)
                fwd.start()
                sends.append(fwd)
        for u in range(nu):
            for j in range(3):
                theirs = units[u].full_half(outs[u], others[j], 1 - c)
                pltpu.make_async_remote_copy(
                    src_ref=theirs, dst_ref=theirs, send_sem=send2.at[3 * u + j], recv_sem=recv2.at[3 * u + j],
                    device_id=(x, y, c), device_id_type=MESH).wait_recv()
        for j in range(3):
            dst = sc_out.at[:, pl.ds(others[j] * ps, ps)]
            pltpu.make_async_remote_copy(src_ref=dst, dst_ref=dst, send_sem=send3.at[j], recv_sem=recv3.at[j],
                                         device_id=(x, y, c), device_id_type=MESH).wait_recv()
        for cp in sends:
            cp.wait_send()
        for cp in local:
            cp.wait()

    out_shape = [jax.ShapeDtypeStruct(units[u].full_shape, shards[u].dtype) for u in range(nu)]
    out_shape.append(jax.ShapeDtypeStruct((1, N_CHIPS * ps), scale_shard.dtype))
    dma = pltpu.SemaphoreType.DMA
    return pl.pallas_call(
        body, name="gather_weights", in_specs=[ANY] * (nu + 1), out_specs=[ANY] * (nu + 1), out_shape=out_shape,
        scratch_shapes=[dma((3 * nu,)), dma((3 * nu,)), dma((3 * nu,)), dma((3 * nu,)), dma((3,)), dma((3,)),
                        dma((nu + 1,))],
    )(*shards, scale_shard)


def _to_sibling(grads):
    nu = len(grads)

    def body(*refs):
        ins, outs = refs[:nu], refs[nu:2 * nu]
        send, recv = refs[2 * nu:]
        x, y, c, _, _, _ = _mesh_pos()
        cps = [pltpu.make_async_remote_copy(
            src_ref=ins[u].at[:, 1 - c], dst_ref=outs[u], send_sem=send.at[u], recv_sem=recv.at[u],
            device_id=(x, y, 1 - c), device_id_type=MESH) for u in range(nu)]
        for cp in cps:
            cp.start()
        for cp in cps:
            cp.wait()

    out_shape = [jax.ShapeDtypeStruct((g.shape[0],) + g.shape[2:], g.dtype) for g in grads]
    dma = pltpu.SemaphoreType.DMA
    return pl.pallas_call(
        body, name="grads_to_sibling", in_specs=[ANY] * nu, out_specs=[ANY] * nu, out_shape=out_shape,
        scratch_shapes=[dma((nu,)), dma((nu,))],
    )(*grads)


def _to_owners(partials):
    nu = len(partials)

    def body(*refs):
        ins, outs = refs[:nu], refs[nu:2 * nu]
        send, recv, lsem = refs[2 * nu:]
        x, y, c, chips, me, others = _mesh_pos()
        local = [pltpu.make_async_copy(ins[u].at[me], outs[u].at[me], lsem.at[u]) for u in range(nu)]
        cps = [pltpu.make_async_remote_copy(
            src_ref=ins[u].at[others[j]], dst_ref=outs[u].at[me], send_sem=send.at[3 * u + j],
            recv_sem=recv.at[3 * u + j], device_id=(*chips[j], c), device_id_type=MESH)
            for u in range(nu) for j in range(3)]
        for cp in local + cps:
            cp.start()
        for u in range(nu):
            for j in range(3):
                slot = outs[u].at[others[j]]
                pltpu.make_async_remote_copy(
                    src_ref=slot, dst_ref=slot, send_sem=send.at[3 * u + j], recv_sem=recv.at[3 * u + j],
                    device_id=(x, y, c), device_id_type=MESH).wait_recv()
        for cp in cps:
            cp.wait_send()
        for cp in local:
            cp.wait()

    out_shape = [jax.ShapeDtypeStruct(p.shape, p.dtype) for p in partials]
    dma = pltpu.SemaphoreType.DMA
    return pl.pallas_call(
        body, name="partials_to_owners", in_specs=[ANY] * nu, out_specs=[ANY] * nu, out_shape=out_shape,
        scratch_shapes=[dma((3 * nu,)), dma((3 * nu,)), dma((nu,))],
    )(*partials)


def _share_halves(units, halves):
    nu = len(units)

    def body(*refs):
        ins, outs = refs[:nu], refs[nu:2 * nu]
        send, recv, lsem = refs[2 * nu:]
        x, y, c, _, _, _ = _mesh_pos()
        local = [pltpu.make_async_copy(ins[u], units[u].shard_half(outs[u], c), lsem.at[u]) for u in range(nu)]
        cps = [pltpu.make_async_remote_copy(
            src_ref=ins[u], dst_ref=units[u].shard_half(outs[u], c), send_sem=send.at[u], recv_sem=recv.at[u],
            device_id=(x, y, 1 - c), device_id_type=MESH) for u in range(nu)]
        for cp in local + cps:
            cp.start()
        for u in range(nu):
            theirs = units[u].shard_half(outs[u], 1 - c)
            pltpu.make_async_remote_copy(src_ref=theirs, dst_ref=theirs, send_sem=send.at[u], recv_sem=recv.at[u],
                                         device_id=(x, y, c), device_id_type=MESH).wait_recv()
        for cp in cps:
            cp.wait_send()
        for cp in local:
            cp.wait()

    out_shape = [jax.ShapeDtypeStruct(units[u].shard_shape, halves[u].dtype) for u in range(nu)]
    dma = pltpu.SemaphoreType.DMA
    return pl.pallas_call(
        body, name="share_halves", in_specs=[ANY] * nu, out_specs=[ANY] * nu, out_shape=out_shape,
        scratch_shapes=[dma((nu,)), dma((nu,)), dma((nu,))],
    )(*halves)


def _add_mine(name, grad, recv, c_idx):
    _, _, r_n, c_n = grad.shape
    tr = _pick(r_n, 256, 8)

    def body(c_ref, g_ref, r_ref, o_ref):
        o_ref[...] = g_ref[...] + r_ref[...]

    return pl.pallas_call(
        body, name=name,
        grid_spec=pltpu.PrefetchScalarGridSpec(
            num_scalar_prefetch=1, grid=(N_CHIPS, r_n // tr),
            in_specs=[pl.BlockSpec((None, None, tr, c_n), lambda k, i, c_ref: (k, c_ref[0], i, 0)),
                      pl.BlockSpec((None, tr, c_n), lambda k, i, c_ref: (k, i, 0))],
            out_specs=pl.BlockSpec((None, tr, c_n), lambda k, i, c_ref: (k, i, 0))),
        out_shape=jax.ShapeDtypeStruct(recv.shape, F32), compiler_params=_params("parallel", "parallel"),
    )(c_idx, grad, recv)


def _add_slots(name, slots):
    _, r_n, c_n = slots.shape
    tr = _pick(r_n, 256, 8)

    def body(s_ref, o_ref):
        o_ref[...] = ((s_ref[0] + s_ref[1]) + s_ref[2]) + s_ref[3]

    return pl.pallas_call(
        body, name=name, grid=(r_n // tr,), in_specs=[pl.BlockSpec((N_CHIPS, tr, c_n), lambda i: (0, i, 0))],
        out_specs=pl.BlockSpec((tr, c_n), lambda i: (i, 0)), out_shape=jax.ShapeDtypeStruct((r_n, c_n), F32),
        compiler_params=_params("parallel"),
    )(slots)


def _allreduce_small(block):
    r_n, c_n = block.shape

    def body(in_ref, out_ref, slots, send, recv):
        x, y, c = lax.axis_index("x"), lax.axis_index("y"), lax.axis_index("c")
        me = 4 * x + 2 * y + c
        slots[me] = in_ref[...]
        flips = [(fx, fy, fc) for fx in (0, 1) for fy in (0, 1) for fc in (0, 1)][1:]
        peers = [(x ^ fx, y ^ fy, c ^ fc) for fx, fy, fc in flips]
        cps = [pltpu.make_async_remote_copy(src_ref=in_ref, dst_ref=slots.at[me], send_sem=send.at[j],
                                            recv_sem=recv.at[j], device_id=peers[j], device_id_type=MESH)
               for j in range(7)]
        for cp in cps:
            cp.start()
        for j, (px, py, pc) in enumerate(peers):
            slot = slots.at[4 * px + 2 * py + pc]
            pltpu.make_async_remote_copy(src_ref=slot, dst_ref=slot, send_sem=send.at[j], recv_sem=recv.at[j],
                                         device_id=(x, y, c), device_id_type=MESH).wait_recv()
        for cp in cps:
            cp.wait_send()
        total = slots[0]
        for d in range(1, 8):
            total = total + slots[d]
        out_ref[...] = total

    vmem = pl.BlockSpec(memory_space=pltpu.VMEM)
    return pl.pallas_call(
        body, name="allreduce_small", in_specs=[vmem], out_specs=vmem,
        out_shape=jax.ShapeDtypeStruct((r_n, c_n), F32),
        scratch_shapes=[pltpu.VMEM((8, r_n, c_n), F32), pltpu.SemaphoreType.DMA((7,)), pltpu.SemaphoreType.DMA((7,))],
    )(block)


def _first(accs, extras):
    return [accs[0]] if isinstance(accs, list) else [accs]


def _ffn_fwd(tag, x_in, h, wg, wu, wd):
    def act(accs, extras):
        a, b = accs
        return [a, b, a * _sigmoid(a) * b]

    a, b, s = _mm_nn(f"ffn_up_{tag}", h, [wg, wu], [], act, [F32, F32, BF16])
    x_out, = _mm_nn(f"ffn_down_{tag}", s, [wd], [x_in], lambda accs, ex: [ex[0] + accs[0]], [F32])
    return x_out, a, b, s


def _ffn_bwd(tag, layer, dx_out, h, a, b, s, wg, wu, wd, into):
    def mid(acc, extras):
        av, bv = extras
        sg = _sigmoid(av)
        return [acc * bv * _dsilu(av, sg), acc * (av * sg)]

    dxb = dx_out.astype(BF16)
    da, db = _mm_nt(f"ffn_dact_{tag}", [(dxb, wd)], [a, b], mid, [BF16, BF16])
    dh, = _mm_nt(f"ffn_dh_{tag}", [(da, wg), (db, wu)], [], _first, [F32])
    d_n, f_n = wg.shape[1], wg.shape[2]
    ns = f_n // N_CHIPS
    tki = _pick(d_n // 2, 512)
    ih = (d_n // 2) // tki
    col_shape = (N_CHIPS, 2, 2, d_n // 2, ns)
    col_block = (None, None, None, tki, ns)
    col_index = lambda g, i, j: (j, i // ih, layer, i % ih, 0)
    dwg = _mm_tn(f"ffn_dwg_{tag}", h, da, 1, col_shape, col_block, col_index, tki=tki, tn=ns, into=into[0])
    dwu = _mm_tn(f"ffn_dwu_{tag}", h, db, 1, col_shape, col_block, col_index, tki=tki, tn=ns, into=into[1])
    tn = _pick(d_n // 2, 512)
    jh = (d_n // 2) // tn
    dwd = _mm_tn(f"ffn_dwd_{tag}", s, dxb, 1, (N_CHIPS, 2, 2, ns, d_n // 2), (None, None, None, ns, tn),
                 lambda g, i, j: (i, j // jh, layer, 0, j % jh), tki=ns, tn=tn, into=into[2])
    return dh, (dwg, dwu, dwd)


def kernel(x, mix_norm, ffn_norm, final_norm, ab_w_in, lb_logits, hg_out_norm, ab_w_out, pool_w, pool_scale, ffn_w_gate, ffn_w_up, ffn_w_down, loss_target, m_mix_norm, m_ffn_norm, m_final_norm, m_ab_w_in, m_lb_logits, m_hg_out_norm, m_ab_w_out, m_pool_w, m_pool_scale, m_ffn_w_gate, m_ffn_w_up, m_ffn_w_down, v_mix_norm, v_ffn_norm, v_final_norm, v_ab_w_in, v_lb_logits, v_hg_out_norm, v_ab_w_out, v_pool_w, v_pool_scale, v_ffn_w_gate, v_ffn_w_up, v_ffn_w_down):
    xs, target = x[0], loss_target[0]
    s_n, d_n = xs.shape
    h_n = d_n // 2 // HEAD
    hw = h_n * HEAD
    n_grp = len(POOL_WINDOWS)
    grp = d_n // n_grp
    c_idx = lax.axis_index("c").astype(jnp.int32).reshape(1)
    chip = 2 * lax.axis_index("x") + lax.axis_index("y")

    units = [
        _Unit(ab_w_in.shape[1:], 1, 0),
        _Unit(ab_w_out.shape[1:], 0, 0),
        _Unit(pool_w.shape[1:], 1, 0),
        _Unit(ffn_w_gate.shape, 2, 1),
        _Unit(ffn_w_up.shape, 2, 1),
        _Unit(ffn_w_down.shape, 1, 2),
    ]
    shards = [ab_w_in[0], ab_w_out[0], pool_w[0], ffn_w_gate, ffn_w_up, ffn_w_down]
    w_in, w_out, w_pool, w_gate, w_up, w_down, scale_full = _gather_weights(
        units, [t.astype(BF16) for t in shards], pool_scale)
    w_in3, w_out3 = w_in[None], w_out[None]
    row = lambda t: t.reshape(1, -1)

    h0 = _rms_fwd("norm_mix0", xs, row(mix_norm[0]))
    proj, projb = _mm_nn("proj_in", h0, [w_in3], [], lambda accs, ex: [accs[0], accs[0]], [F32, BF16])
    oraw, o_a, states = _hg_fwd("hgrn_fwd", proj, lb_logits, hg_out_norm, h_n)
    o_b, ltot = _att_fwd("attn_fwd", projb, h_n)
    cat = jnp.concatenate([o_a, o_b], axis=1)
    x1, = _mm_nn("proj_out", cat, [w_out3], [xs], lambda accs, ex: [ex[0] + accs[0]], [F32])
    h1 = _rms_fwd("norm_ffn0", x1, row(ffn_norm[0]))
    x2, a0, b0, s0 = _ffn_fwd("l0", x1, h1, w_gate[0:1], w_up[0:1], w_down[0:1])
    pooled = _pool_fwd("pool_fwd", x2, row(mix_norm[1]))
    x3, mixed = _mm_nn("pool_mix", pooled, [w_pool], [x2, scale_full],
                       lambda accs, ex: [ex[0] + accs[0] * ex[1], accs[0]], [F32, F32], tk=grp, tn=grp)
    h3 = _rms_fwd("norm_ffn1", x3, row(ffn_norm[1]))
    x4, a1, b1, s1 = _ffn_fwd("l1", x3, h3, w_gate[1:2], w_up[1:2], w_down[1:2])

    dx4, d_final, loss = _loss_bwd("loss_bwd", x4, target, row(final_norm))
    dh3, ffn_grads = _ffn_bwd("l1", 1, dx4, h3, a1, b1, s1, w_gate[1:2], w_up[1:2], w_down[1:2], (None, None, None))
    dx3, d_ffn1 = _rms_bwd("norm_ffn1_bwd", dh3, x3, row(ffn_norm[1]), dx4)
    dmixed, d_scale = _scale_bwd("pool_scale_bwd", dx3, mixed, scale_full)
    dpooled, = _mm_nt("pool_dpooled", [(dmixed, w_pool)], [], _first, [F32], to=grp, tr=grp)
    slab_rows = grp // N_CHIPS
    d_pool = _mm_tn("pool_dw", pooled, dmixed, n_grp, (N_CHIPS, 2, n_grp // 2, slab_rows, grp),
                    (None, None, None, slab_rows, grp), lambda g, i, j: (i, g // 2, g % 2, 0, 0),
                    tki=slab_rows, tn=grp)
    dx2, d_mix1 = _pool_bwd("pool_bwd", dpooled, x2, row(mix_norm[1]), dx3)
    dh1, ffn_grads = _ffn_bwd("l0", 0, dx2, h1, a0, b0, s0, w_gate[0:1], w_up[0:1], w_down[0:1], ffn_grads)
    dx1, d_ffn0 = _rms_bwd("norm_ffn0_bwd", dh1, x1, row(ffn_norm[0]), dx2)
    dx1b = dx1.astype(BF16)
    dcat, = _mm_nt("proj_out_dcat", [(dx1b, w_out3)], [], _first, [F32])
    d_wout = _mm_tn("proj_out_dw", cat, dx1b, 1, (1, 2 * hw, d_n), (None, _pick(2 * hw, 512), _pick(d_n, 512)),
                    lambda g, i, j: (g, i, j), tki=_pick(2 * hw, 512), tn=_pick(d_n, 512))
    dqb, dkb, dvb = _att_bwd("attn_bwd", projb, dcat, ltot, h_n)
    dqa, dfa, dia, dga, d_lb, d_hgn = _hg_bwd("hgrn_bwd", proj, dcat, oraw, states, lb_logits, hg_out_norm, h_n)
    dproj = jnp.concatenate([dqa, dfa, dia, dga, dqb, dkb.astype(BF16), dvb.astype(BF16)], axis=1)
    dh0, = _mm_nt("proj_in_dh", [(dproj, w_in3)], [], _first, [F32])
    ns_in = 7 * hw // N_CHIPS
    tki_in, tn_in = _pick(d_n // 2, 512), _pick(ns_in, 896)
    ih_in, jps_in = (d_n // 2) // tki_in, ns_in // tn_in
    d_win = _mm_tn("proj_in_dw", h0, dproj, 1, (N_CHIPS, 2, d_n // 2, ns_in), (None, None, tki_in, tn_in),
                   lambda g, i, j: (j // jps_in, i // ih_in, i % ih_in, j % jps_in), tki=tki_in, tn=tn_in)
    dx0, d_mix0 = _rms_bwd("norm_mix0_bwd", dh0, xs, row(mix_norm[0]), dx1)

    full_grads = [d_win, d_wout, d_pool, *ffn_grads]
    as4 = lambda g, u: g.reshape(N_CHIPS, 2, -1, u.half_shape[-1])
    grads4 = [as4(g, u) for g, u in zip(full_grads, units)]
    from_sibling = _to_sibling(grads4)
    partials = [_add_mine(f"add_sibling_{n}", g, r, c_idx) for n, (g, r) in enumerate(zip(grads4, from_sibling))]
    slots = _to_owners(partials)
    halves = [_add_slots(f"add_chips_{n}", s).reshape(u.half_shape) for n, (s, u) in enumerate(zip(slots, units))]
    g_win, g_wout, g_pool, g_gate, g_up, g_down = _share_halves(units, halves)

    lanes = 2 * d_n
    pad = lambda t: jnp.pad(t.reshape(1, -1), ((0, 0), (0, lanes - t.size)))
    small = jnp.concatenate([
        pad(jnp.concatenate([d_mix0, d_mix1], axis=0)), pad(jnp.concatenate([d_ffn0, d_ffn1], axis=0)),
        pad(d_final), pad(d_lb), pad(d_hgn), pad(d_scale), jnp.zeros((2, lanes), F32)], axis=0)
    small = _allreduce_small(small)
    g_mix = small[0, :2 * d_n].reshape(2, d_n)
    g_ffn = small[1, :2 * d_n].reshape(2, d_n)
    g_final = small[2, :d_n]
    g_lb = small[3, :2 * hw].reshape(2, hw)
    g_hgn = small[4, :HEAD].reshape(1, HEAD)
    g_scale = lax.dynamic_slice(small[5, :d_n], (chip * grp,), (grp,)).reshape(1, grp)
    loss = lax.psum(loss[0, 0], ("x", "y", "c"))

    grads = [g_mix, g_ffn, g_final, g_win[None], g_lb, g_hgn, g_wout[None], g_pool[None], g_scale,
             g_gate, g_up, g_down]
    weights = [mix_norm, ffn_norm, final_norm, ab_w_in, lb_logits, hg_out_norm, ab_w_out, pool_w, pool_scale,
               ffn_w_gate, ffn_w_up, ffn_w_down]
    ms = [m_mix_norm, m_ffn_norm, m_final_norm, m_ab_w_in, m_lb_logits, m_hg_out_norm, m_ab_w_out, m_pool_w,
          m_pool_scale, m_ffn_w_gate, m_ffn_w_up, m_ffn_w_down]
    vs = [v_mix_norm, v_ffn_norm, v_final_norm, v_ab_w_in, v_lb_logits, v_hg_out_norm, v_ab_w_out, v_pool_w,
          v_pool_scale, v_ffn_w_gate, v_ffn_w_up, v_ffn_w_down]
    deltas, new_ms, new_vs = [], [], []
    for n, (w, g, m, v) in enumerate(zip(weights, grads, ms, vs)):
        w2 = w.reshape(1, -1) if w.ndim == 1 else w
        d, nm, nv = _adamw_nd(f"adamw_{n}", w2, g.reshape(w2.shape), m.reshape(w2.shape), v.reshape(w2.shape))
        deltas.append(d.reshape(w.shape))
        new_ms.append(nm.reshape(w.shape))
        new_vs.append(nv.reshape(w.shape))
    grads = [g.reshape(w.shape) for g, w in zip(grads, weights)]
    return (loss, dx0[None], *grads, *deltas, *new_ms, *new_vs)
```

```python
import functools
import math

import jax
import jax.numpy as jnp
from jax import lax
from jax.experimental import pallas as pl
from jax.experimental.pallas import tpu as pltpu

F32 = jnp.float32
BF16 = jnp.bfloat16
HIGHEST = lax.Precision.HIGHEST
MESH = pl.DeviceIdType.MESH
ANY = pl.BlockSpec(memory_space=pl.ANY)

RMS_EPS = 1e-6
LOG2_E = 1.4426950408889634
HEAD = 128
HG_CHUNK = 64
HG_MID = HG_CHUNK // 2 - 1
HG_BLOCK = 512
ATT_BLOCK = 256
ATT_PAR = 2
POOL_WINDOWS = (2, 4, 8, 16)
POOL_HALO = 16
N_CHIPS = 4
ADAM_LR, ADAM_B1, ADAM_B2, ADAM_EPS, ADAM_WD, ADAM_STEP = 0.001, 0.9, 0.999, 1e-08, 0.01, 10
VMEM_LIMIT = 56 * 1024 * 1024

NT_DIMS = (((1,), (1,)), ((), ()))
TN_DIMS = (((0,), (0,)), ((), ()))


def _params(*sem):
    return pltpu.CompilerParams(dimension_semantics=sem, vmem_limit_bytes=VMEM_LIMIT)


def _pick(dim, pref, unit=128):
    best = None
    for t in range(unit, min(dim, pref) + 1, unit):
        if dim % t == 0:
            best = t
    return dim if best is None else best


def _sigmoid(z):
    return 1.0 / (1.0 + jnp.exp(-z))


def _dsilu(a, sg):
    return sg * (1.0 + a * (1.0 - sg))


def _mm_nn(name, a, bs, extras, epilogue, out_dtypes, *, tm=512, tn=512, tk=2048):
    g_n, k_n, n_n = bs[0].shape
    m_n = a.shape[0]
    tm, tn, tk = _pick(m_n, tm, 8), _pick(n_n, tn), _pick(k_n, tk)
    i_n, j_n, kt = m_n // tm, n_n // tn, k_n // tk
    nb, ne, no = len(bs), len(extras), len(out_dtypes)

    def body(*refs):
        a_ref, b_refs, e_refs = refs[0], refs[1:1 + nb], refs[1 + nb:1 + nb + ne]
        o_refs, acc_refs = refs[1 + nb + ne:1 + nb + ne + no], refs[1 + nb + ne + no:]
        k = pl.program_id(3)
        av = a_ref[...]
        prods = [jnp.dot(av, b_ref[...], preferred_element_type=F32) for b_ref in b_refs]

        def finish(accs):
            outs = epilogue(accs, [e[...] for e in e_refs])
            for o_ref, o in zip(o_refs, outs):
                o_ref[...] = o.astype(o_ref.dtype)

        if kt == 1:
            finish(prods)
        else:
            @pl.when(k == 0)
            def _():
                for acc, p in zip(acc_refs, prods):
                    acc[...] = p

            @pl.when(k > 0)
            def _():
                for acc, p in zip(acc_refs, prods):
                    acc[...] += p

            @pl.when(k == kt - 1)
            def _():
                finish([acc[...] for acc in acc_refs])

    in_specs = [pl.BlockSpec((tm, tk), lambda g, i, j, k: (i, g * kt + k))]
    in_specs += [pl.BlockSpec((None, tk, tn), lambda g, i, j, k: (g, k, j)) for _ in bs]
    for e in extras:
        if e.shape[0] == 1:
            in_specs.append(pl.BlockSpec((1, tn), lambda g, i, j, k: (0, g * j_n + j)))
        else:
            in_specs.append(pl.BlockSpec((tm, tn), lambda g, i, j, k: (i, g * j_n + j)))
    out_specs = [pl.BlockSpec((tm, tn), lambda g, i, j, k: (i, g * j_n + j)) for _ in out_dtypes]
    out_shape = [jax.ShapeDtypeStruct((m_n, g_n * n_n), dt) for dt in out_dtypes]
    scratch = [] if kt == 1 else [pltpu.VMEM((tm, tn), F32) for _ in bs]
    return pl.pallas_call(
        body, name=name, grid=(g_n, i_n, j_n, kt), in_specs=in_specs, out_specs=out_specs, out_shape=out_shape,
        scratch_shapes=scratch, compiler_params=_params("parallel", "parallel", "parallel", "arbitrary"),
    )(a, *bs, *extras)


def _mm_nt(name, pairs, extras, epilogue, out_dtypes, *, tm=512, to=512, tr=2048):
    g_n, kd, n_n = pairs[0][1].shape
    m_n = pairs[0][0].shape[0]
    tm, to, tr = _pick(m_n, tm, 8), _pick(kd, to), _pick(n_n, tr)
    i_n, j_n, rt = m_n // tm, kd // to, n_n // tr
    npairs, ne, no = len(pairs), len(extras), len(out_dtypes)

    def body(*refs):
        ab_refs, e_refs = refs[:2 * npairs], refs[2 * npairs:2 * npairs + ne]
        o_refs, acc_refs = refs[2 * npairs + ne:2 * npairs + ne + no], refs[2 * npairs + ne + no:]
        r = pl.program_id(3)
        prod = None
        for p in range(npairs):
            t = lax.dot_general(ab_refs[2 * p][...], ab_refs[2 * p + 1][...], NT_DIMS, preferred_element_type=F32)
            prod = t if prod is None else prod + t

        def finish(acc):
            outs = epilogue(acc, [e[...] for e in e_refs])
            for o_ref, o in zip(o_refs, outs):
                o_ref[...] = o.astype(o_ref.dtype)

        if rt == 1:
            finish(prod)
        else:
            acc = acc_refs[0]

            @pl.when(r == 0)
            def _():
                acc[...] = prod

            @pl.when(r > 0)
            def _():
                acc[...] += prod

            @pl.when(r == rt - 1)
            def _():
                finish(acc[...])

    in_specs, args = [], []
    for a, b in pairs:
        in_specs.append(pl.BlockSpec((tm, tr), lambda g, i, j, r: (i, g * rt + r)))
        in_specs.append(pl.BlockSpec((None, to, tr), lambda g, i, j, r: (g, j, r)))
        args += [a, b]
    in_specs += [pl.BlockSpec((tm, to), lambda g, i, j, r: (i, g * j_n + j)) for _ in extras]
    out_specs = [pl.BlockSpec((tm, to), lambda g, i, j, r: (i, g * j_n + j)) for _ in out_dtypes]
    out_shape = [jax.ShapeDtypeStruct((m_n, g_n * kd), dt) for dt in out_dtypes]
    scratch = [] if rt == 1 else [pltpu.VMEM((tm, to), F32)]
    return pl.pallas_call(
        body, name=name, grid=(g_n, i_n, j_n, rt), in_specs=in_specs, out_specs=out_specs, out_shape=out_shape,
        scratch_shapes=scratch, compiler_params=_params("parallel", "parallel", "parallel", "arbitrary"),
    )(*args, *extras)


def _mm_tn(name, a, b, g_n, out_shape, out_block, out_index, *, tki, tn, tm=512, into=None):
    m_n = a.shape[0]
    k_n, n_n = a.shape[1] // g_n, b.shape[1] // g_n
    tm = _pick(m_n, tm, 8)
    i_n, j_n, mt = k_n // tki, n_n // tn, m_n // tm
    assert k_n % tki == 0 and n_n % tn == 0

    def body(*refs):
        a_ref, b_ref = refs[0], refs[1]
        o_ref, acc = refs[-2], refs[-1]
        m = pl.program_id(3)
        prod = lax.dot_general(a_ref[...], b_ref[...], TN_DIMS, preferred_element_type=F32)

        @pl.when(m == 0)
        def _():
            acc[...] = prod

        @pl.when(m > 0)
        def _():
            acc[...] += prod

        @pl.when(m == mt - 1)
        def _():
            o_ref[...] = acc[...].reshape(o_ref.shape)

    in_specs = [pl.BlockSpec((tm, tki), lambda g, i, j, m: (m, g * i_n + i)),
                pl.BlockSpec((tm, tn), lambda g, i, j, m: (m, g * j_n + j))]
    args = [a, b]
    aliases = {}
    if into is not None:
        in_specs.append(ANY)
        args.append(into)
        aliases = {2: 0}
    return pl.pallas_call(
        body, name=name, grid=(g_n, i_n, j_n, mt), in_specs=in_specs,
        out_specs=pl.BlockSpec(out_block, lambda g, i, j, m: out_index(g, i, j)),
        out_shape=jax.ShapeDtypeStruct(out_shape, F32), scratch_shapes=[pltpu.VMEM((tki, tn), F32)],
        input_output_aliases=aliases,
        compiler_params=_params("parallel", "parallel", "parallel", "arbitrary"),
    )(*args)


def _rstd(xv):
    return lax.rsqrt(jnp.mean(xv * xv, axis=-1, keepdims=True) + RMS_EPS)


def _rms_bwd_rows(dh, xv, gain, r):
    dy = dh * gain
    c = jnp.mean(dy * xv, axis=-1, keepdims=True)
    return r * dy - xv * (r * r * r) * c, dh * xv * r


def _fold8(t):
    return t.reshape(t.shape[0] // 8, 8, t.shape[1]).sum(axis=0)


def _rms_fwd(name, x, gain, tm=256):
    s_n, d_n = x.shape
    tm = _pick(s_n, tm, 8)

    def body(x_ref, g_ref, h_ref):
        xv = x_ref[...]
        h_ref[...] = (xv * _rstd(xv) * g_ref[...]).astype(h_ref.dtype)

    return pl.pallas_call(
        body, name=name, grid=(s_n // tm,),
        in_specs=[pl.BlockSpec((tm, d_n), lambda i: (i, 0)), pl.BlockSpec((1, d_n), lambda i: (0, 0))],
        out_specs=pl.BlockSpec((tm, d_n), lambda i: (i, 0)), out_shape=jax.ShapeDtypeStruct((s_n, d_n), BF16),
        compiler_params=_params("parallel"),
    )(x, gain)


def _rms_bwd(name, dh, x, gain, dres, tm=256):
    s_n, d_n = x.shape
    tm = _pick(s_n, tm, 8)
    nblk = s_n // tm

    def body(dh_ref, x_ref, g_ref, dres_ref, dx_ref, dg_ref, acc):
        i = pl.program_id(0)

        @pl.when(i == 0)
        def _():
            acc[...] = jnp.zeros_like(acc)

        xv = x_ref[...]
        dxv, dgt = _rms_bwd_rows(dh_ref[...].astype(F32), xv, g_ref[...], _rstd(xv))
        dx_ref[...] = dres_ref[...] + dxv
        acc[...] += _fold8(dgt)

        @pl.when(i == nblk - 1)
        def _():
            dg_ref[...] = jnp.sum(acc[...], axis=0, keepdims=True)

    row = pl.BlockSpec((tm, d_n), lambda i: (i, 0))
    vec = pl.BlockSpec((1, d_n), lambda i: (0, 0))
    return pl.pallas_call(
        body, name=name, grid=(nblk,), in_specs=[row, row, vec, row], out_specs=[row, vec],
        out_shape=[jax.ShapeDtypeStruct((s_n, d_n), F32), jax.ShapeDtypeStruct((1, d_n), F32)],
        scratch_shapes=[pltpu.VMEM((8, d_n), F32)], compiler_params=_params("arbitrary"),
    )(dh, x, gain, dres)


def _loss_bwd(name, x, target, gain, tm=256):
    s_n, d_n = x.shape
    tm = _pick(s_n, tm, 8)
    nblk = s_n // tm

    def body(x_ref, t_ref, g_ref, dx_ref, dg_ref, loss_ref, acc, lacc):
        i = pl.program_id(0)

        @pl.when(i == 0)
        def _():
            acc[...] = jnp.zeros_like(acc)
            lacc[...] = jnp.zeros_like(lacc)

        xv = x_ref[...]
        gain = g_ref[...]
        r = _rstd(xv)
        diff = xv * r * gain - t_ref[...]
        lacc[...] += _fold8(diff * diff)
        dxv, dgt = _rms_bwd_rows(diff * (1.0 / d_n), xv, gain, r)
        dx_ref[...] = dxv
        acc[...] += _fold8(dgt)

        @pl.when(i == nblk - 1)
        def _():
            dg_ref[...] = jnp.sum(acc[...], axis=0, keepdims=True)
            loss_ref[...] = jnp.sum(lacc[...], keepdims=True) * (0.5 / d_n)

    row = pl.BlockSpec((tm, d_n), lambda i: (i, 0))
    vec = pl.BlockSpec((1, d_n), lambda i: (0, 0))
    return pl.pallas_call(
        body, name=name, grid=(nblk,), in_specs=[row, row, vec],
        out_specs=[row, vec, pl.BlockSpec((1, 1), lambda i: (0, 0))],
        out_shape=[jax.ShapeDtypeStruct((s_n, d_n), F32), jax.ShapeDtypeStruct((1, d_n), F32),
                   jax.ShapeDtypeStruct((1, 1), F32)],
        scratch_shapes=[pltpu.VMEM((8, d_n), F32), pltpu.VMEM((8, d_n), F32)], compiler_params=_params("arbitrary"),
    )(x, target, gain)


def _pool_counts(t_idx, d_n):
    grp = d_n // len(POOL_WINDOWS)
    lane = lax.broadcasted_iota(jnp.int32, (1, d_n), 1) // grp
    win = jnp.zeros((1, d_n), jnp.int32)
    for gi, w in enumerate(POOL_WINDOWS):
        win = jnp.where(lane == gi, w, win)
    return jnp.minimum(t_idx + 1, win).astype(F32), lane


def _window_sums(rows, lane, backward):
    n = rows.shape[0]
    out = rows
    acc = rows
    width = 1
    for gi in range(len(POOL_WINDOWS)):
        shift = (n - width) if backward else width
        acc = acc + pltpu.roll(acc, shift, 0)
        width *= 2
        out = jnp.where(lane >= gi, acc, out)
    return out


def _pool_fwd(name, x, gain, tm=256):
    s_n, d_n = x.shape
    tm = _pick(s_n, tm, POOL_HALO)
    per = tm // POOL_HALO

    def body(x_ref, halo_ref, g_ref, o_ref):
        i = pl.program_id(0)
        halo = jnp.where(i == 0, 0.0, halo_ref[...])
        rows = jnp.concatenate([halo, x_ref[...]], axis=0)
        h = rows * _rstd(rows) * g_ref[...]
        t_idx = i * tm - POOL_HALO + lax.broadcasted_iota(jnp.int32, (tm + POOL_HALO, 1), 0)
        cnt, lane = _pool_counts(t_idx, d_n)
        pooled = _window_sums(h, lane, False) / cnt - h
        o_ref[...] = pooled[POOL_HALO:, :].astype(o_ref.dtype)

    return pl.pallas_call(
        body, name=name, grid=(s_n // tm,),
        in_specs=[pl.BlockSpec((tm, d_n), lambda i: (i, 0)),
                  pl.BlockSpec((POOL_HALO, d_n), lambda i: (jnp.maximum(i * per - 1, 0), 0)),
                  pl.BlockSpec((1, d_n), lambda i: (0, 0))],
        out_specs=pl.BlockSpec((tm, d_n), lambda i: (i, 0)), out_shape=jax.ShapeDtypeStruct((s_n, d_n), BF16),
        compiler_params=_params("parallel"),
    )(x, x, gain)


def _pool_bwd(name, dpooled, x, gain, dres, tm=256):
    s_n, d_n = x.shape
    tm = _pick(s_n, tm, POOL_HALO)
    per = tm // POOL_HALO
    nblk = s_n // tm
    last_halo = s_n // POOL_HALO - 1

    def body(dp_ref, halo_ref, x_ref, g_ref, dres_ref, dx_ref, dg_ref, acc):
        i = pl.program_id(0)

        @pl.when(i == 0)
        def _():
            acc[...] = jnp.zeros_like(acc)

        halo = jnp.where(i == nblk - 1, 0.0, halo_ref[...])
        rows = jnp.concatenate([dp_ref[...], halo], axis=0)
        t_idx = i * tm + lax.broadcasted_iota(jnp.int32, (tm + POOL_HALO, 1), 0)
        cnt, lane = _pool_counts(t_idx, d_n)
        dh = (_window_sums(rows / cnt, lane, True) - rows)[:tm, :]
        xv = x_ref[...]
        dxv, dgt = _rms_bwd_rows(dh, xv, g_ref[...], _rstd(xv))
        dx_ref[...] = dres_ref[...] + dxv
        acc[...] += _fold8(dgt)

        @pl.when(i == nblk - 1)
        def _():
            dg_ref[...] = jnp.sum(acc[...], axis=0, keepdims=True)

    row = pl.BlockSpec((tm, d_n), lambda i: (i, 0))
    vec = pl.BlockSpec((1, d_n), lambda i: (0, 0))
    return pl.pallas_call(
        body, name=name, grid=(nblk,),
        in_specs=[row, pl.BlockSpec((POOL_HALO, d_n), lambda i: (jnp.minimum((i + 1) * per, last_halo), 0)),
                  row, vec, row],
        out_specs=[row, vec],
        out_shape=[jax.ShapeDtypeStruct((s_n, d_n), F32), jax.ShapeDtypeStruct((1, d_n), F32)],
        scratch_shapes=[pltpu.VMEM((8, d_n), F32)], compiler_params=_params("arbitrary"),
    )(dpooled, dpooled, x, gain, dres)


def _scale_bwd(name, dx, mixed, scale, tm=256):
    s_n, d_n = dx.shape
    tm = _pick(s_n, tm, 8)
    nblk = s_n // tm

    def body(dx_ref, mx_ref, sc_ref, dm_ref, ds_ref, acc):
        i = pl.program_id(0)

        @pl.when(i == 0)
        def _():
            acc[...] = jnp.zeros_like(acc)

        dxv = dx_ref[...]
        dm_ref[...] = (dxv * sc_ref[...]).astype(dm_ref.dtype)
        acc[...] += _fold8(dxv * mx_ref[...])

        @pl.when(i == nblk - 1)
        def _():
            ds_ref[...] = jnp.sum(acc[...], axis=0, keepdims=True)

    row = pl.BlockSpec((tm, d_n), lambda i: (i, 0))
    vec = pl.BlockSpec((1, d_n), lambda i: (0, 0))
    return pl.pallas_call(
        body, name=name, grid=(nblk,), in_specs=[row, row, vec], out_specs=[row, vec],
        out_shape=[jax.ShapeDtypeStruct((s_n, d_n), BF16), jax.ShapeDtypeStruct((1, d_n), F32)],
        scratch_shapes=[pltpu.VMEM((8, d_n), F32)], compiler_params=_params("arbitrary"),
    )(dx, mixed, scale)


def _hg_gates(qa, fa, lb):
    sig = _sigmoid(fa)
    f = lb + (1.0 - lb) * sig
    sq = _sigmoid(qa)
    return sig, f, jnp.log(f), 1.0 - f, sq, qa * sq


def _hg_chunk_terms(q, k, g, lincl):
    gc = jnp.dot(lincl, g, precision=HIGHEST, preferred_element_type=F32)
    glast = gc[HG_CHUNK - 1:HG_CHUNK, :]
    gm = gc[HG_MID:HG_MID + 1, :]
    e_q, e_l = jnp.exp(gc), jnp.exp(glast - gc)
    e_m, e_mi = jnp.exp(gc - gm), jnp.exp(gm - gc)
    return glast, (e_q, e_l, e_m, e_mi), (q * e_q, k * e_l, q * e_m, k * e_mi)


def _hg_fwd(name, proj, lb_logits, hgain, n_heads):
    s_n = proj.shape[0]
    tb = _pick(s_n, HG_BLOCK, HG_CHUNK)
    nblk, ncb = s_n // tb, tb // HG_CHUNK
    c_n = HG_CHUNK

    def body(qa_ref, fa_ref, ia_ref, ga_ref, l_ref, gn_ref, oraw_ref, oa_ref, st_ref, state):
        @pl.when(pl.program_id(1) == 0)
        def _():
            state[...] = jnp.zeros_like(state)

        lv = l_ref[...]
        lb = _sigmoid(lv[0:1, :] - lv[1:2, :])
        row = lax.broadcasted_iota(jnp.int32, (c_n, c_n), 0)
        col = lax.broadcasted_iota(jnp.int32, (c_n, c_n), 1)
        causal = col <= row
        lincl = causal.astype(F32)
        gn = gn_ref[...]

        def chunk(ci, carry):
            sl = pl.ds(pl.multiple_of(ci * c_n, c_n), c_n)
            _, _, g, k, _, q = _hg_gates(qa_ref[sl, :], fa_ref[sl, :], lb)
            glast, _, (qe, kl, qm, km) = _hg_chunk_terms(q, k, g, lincl)
            vb = ia_ref[sl, :].astype(BF16)
            att = lax.dot_general(qm.astype(BF16), km.astype(BF16), NT_DIMS, preferred_element_type=F32)
            att = jnp.where(causal, att, 0.0).astype(BF16)
            st = state[...]
            o = lax.dot_general(qe.astype(BF16), st.astype(BF16), NT_DIMS, preferred_element_type=F32)
            o = o + jnp.dot(att, vb, preferred_element_type=F32)
            st_ref[ci] = st
            state[...] = st * jnp.exp(glast) + lax.dot_general(vb, kl.astype(BF16), TN_DIMS,
                                                               preferred_element_type=F32)
            oraw_ref[sl, :] = o
            ga = ga_ref[sl, :]
            oa_ref[sl, :] = (o * _rstd(o) * gn * (ga * _sigmoid(ga))).astype(oa_ref.dtype)
            return carry

        lax.fori_loop(0, ncb, chunk, 0)

    h_n = n_heads
    blk = lambda off: pl.BlockSpec((tb, HEAD), lambda h, c: (c, off + h))
    return pl.pallas_call(
        body, name=name, grid=(h_n, nblk),
        in_specs=[blk(0), blk(h_n), blk(2 * h_n), blk(3 * h_n),
                  pl.BlockSpec((2, HEAD), lambda h, c: (0, h)), pl.BlockSpec((1, HEAD), lambda h, c: (0, 0))],
        out_specs=[blk(0), blk(0), pl.BlockSpec((None, ncb, HEAD, HEAD), lambda h, c: (h, c, 0, 0))],
        out_shape=[jax.ShapeDtypeStruct((s_n, h_n * HEAD), F32), jax.ShapeDtypeStruct((s_n, h_n * HEAD), BF16),
                   jax.ShapeDtypeStruct((h_n, s_n // c_n, HEAD, HEAD), F32)],
        scratch_shapes=[pltpu.VMEM((HEAD, HEAD), F32)], compiler_params=_params("parallel", "arbitrary"),
    )(proj, proj, proj, proj, lb_logits, hgain)


def _hg_bwd(name, proj, dcat, oraw, states, lb_logits, hgain, n_heads):
    s_n = proj.shape[0]
    tb = _pick(s_n, HG_BLOCK, HG_CHUNK)
    nblk, ncb = s_n // tb, tb // HG_CHUNK
    c_n = HG_CHUNK
    h_n = n_heads

    def body(qa_ref, fa_ref, ia_ref, ga_ref, doa_ref, oraw_ref, st_ref, l_ref, gn_ref,
             dqa_ref, dfa_ref, dia_ref, dga_ref, dl_ref, dgn_ref, dstate, dlb_acc, dgn_acc):
        h, c = pl.program_id(0), pl.program_id(1)

        @pl.when(c == 0)
        def _():
            dstate[...] = jnp.zeros_like(dstate)
            dlb_acc[...] = jnp.zeros_like(dlb_acc)

        @pl.when((c == 0) & (h == 0))
        def _():
            dgn_acc[...] = jnp.zeros_like(dgn_acc)

        lv = l_ref[...]
        lb = _sigmoid(lv[0:1, :] - lv[1:2, :])
        row = lax.broadcasted_iota(jnp.int32, (c_n, c_n), 0)
        col = lax.broadcasted_iota(jnp.int32, (c_n, c_n), 1)
        causal = col <= row
        lincl = causal.astype(F32)
        uincl = (col >= row).astype(F32)
        is_last = lax.broadcasted_iota(jnp.int32, (c_n, 1), 0) == c_n - 1
        gn = gn_ref[...]

        def chunk(idx, carry):
            ci = ncb - 1 - idx
            sl = pl.ds(pl.multiple_of(ci * c_n, c_n), c_n)
            qa = qa_ref[sl, :]
            sig, f, g, k, sq, q = _hg_gates(qa, fa_ref[sl, :], lb)
            glast, (e_q, e_l, e_m, e_mi), (qe, kl, qm, km) = _hg_chunk_terms(q, k, g, lincl)
            v = ia_ref[sl, :]
            vb = v.astype(BF16)
            qmb, kmb, qeb, klb = qm.astype(BF16), km.astype(BF16), qe.astype(BF16), kl.astype(BF16)
            att = lax.dot_general(qmb, kmb, NT_DIMS, preferred_element_type=F32)
            attb = jnp.where(causal, att, 0.0).astype(BF16)

            o = oraw_ref[sl, :]
            ga = ga_ref[sl, :]
            sg = _sigmoid(ga)
            r = _rstd(o)
            doa = doa_ref[sl, :]
            dn = doa * (ga * sg)
            dga_ref[sl, :] = (doa * (o * r * gn) * _dsilu(ga, sg)).astype(dga_ref.dtype)
            do, dgt = _rms_bwd_rows(dn, o, gn, r)
            dgn_acc[...] += dgt
            dob = do.astype(BF16)

            st0 = st_ref[ci]
            ds1 = dstate[...]
            st0b, ds1b = st0.astype(BF16), ds1.astype(BF16)
            datt = lax.dot_general(dob, vb, NT_DIMS, preferred_element_type=F32)
            dattb = jnp.where(causal, datt, 0.0).astype(BF16)
            dv = lax.dot_general(attb, dob, TN_DIMS, preferred_element_type=F32)
            dv = dv + lax.dot_general(klb, ds1b, NT_DIMS, preferred_element_type=F32)
            dqm = jnp.dot(dattb, kmb, preferred_element_type=F32)
            dkm = lax.dot_general(dattb, qmb, TN_DIMS, preferred_element_type=F32)
            dqe = jnp.dot(dob, st0b, preferred_element_type=F32)
            dkl = jnp.dot(vb, ds1b, preferred_element_type=F32)
            eg = jnp.exp(glast)
            dstate[...] = ds1 * eg + lax.dot_general(dob, qeb, TN_DIMS, preferred_element_type=F32)
            dq = dqm * e_m + dqe * e_q
            dk = dkm * e_mi + dkl * e_l
            dgc = dqm * qmb.astype(F32) - dkm * kmb.astype(F32) + dqe * qe - dkl * kl
            dglast = jnp.sum(dkl * kl, axis=0, keepdims=True) + eg * jnp.sum(ds1 * st0, axis=0, keepdims=True)
            dgc = dgc + jnp.where(is_last, dglast, 0.0)
            dg = jnp.dot(uincl, dgc, precision=HIGHEST, preferred_element_type=F32)
            df = dg / f - dk
            dfa_ref[sl, :] = (df * (1.0 - lb) * sig * (1.0 - sig)).astype(dfa_ref.dtype)
            dlb_acc[...] += df * (1.0 - sig)
            dqa_ref[sl, :] = (dq * _dsilu(qa, sq)).astype(dqa_ref.dtype)
            dia_ref[sl, :] = dv.astype(dia_ref.dtype)
            return carry

        lax.fori_loop(0, ncb, chunk, 0)

        @pl.when(c == nblk - 1)
        def _():
            dl0 = jnp.sum(dlb_acc[...], axis=0, keepdims=True) * lb * (1.0 - lb)
            first = lax.broadcasted_iota(jnp.int32, (2, HEAD), 0) == 0
            dl_ref[...] = jnp.where(first, dl0, -dl0)

        @pl.when((c == nblk - 1) & (h == h_n - 1))
        def _():
            dgn_ref[...] = jnp.sum(dgn_acc[...], axis=0, keepdims=True)

    blk = lambda off: pl.BlockSpec((tb, HEAD), lambda h, c: (nblk - 1 - c, off + h))
    out_act = jax.ShapeDtypeStruct((s_n, h_n * HEAD), BF16)
    return pl.pallas_call(
        body, name=name, grid=(h_n, nblk),
        in_specs=[blk(0), blk(h_n), blk(2 * h_n), blk(3 * h_n), blk(0), blk(0),
                  pl.BlockSpec((None, ncb, HEAD, HEAD), lambda h, c: (h, nblk - 1 - c, 0, 0)),
                  pl.BlockSpec((2, HEAD), lambda h, c: (0, h)), pl.BlockSpec((1, HEAD), lambda h, c: (0, 0))],
        out_specs=[blk(0), blk(0), blk(0), blk(0), pl.BlockSpec((2, HEAD), lambda h, c: (0, h)),
                   pl.BlockSpec((1, HEAD), lambda h, c: (0, 0))],
        out_shape=[out_act, out_act, out_act, out_act, jax.ShapeDtypeStruct((2, h_n * HEAD), F32),
                   jax.ShapeDtypeStruct((1, HEAD), F32)],
        scratch_shapes=[pltpu.VMEM((HEAD, HEAD), F32), pltpu.VMEM((c_n, HEAD), F32), pltpu.VMEM((c_n, HEAD), F32)],
        compiler_params=_params("arbitrary", "arbitrary"),
    )(proj, proj, proj, proj, dcat, oraw, states, lb_logits, hgain)


def _split_dot(t, ones_b):
    hi = t.astype(BF16)
    lo = (t - hi.astype(F32)).astype(BF16)
    both = jnp.dot(jnp.concatenate([hi, lo], axis=0), ones_b, preferred_element_type=F32)
    return both[:t.shape[0]] + both[t.shape[0]:]


def _softplus(z):
    return jnp.maximum(z, 0.0) + jnp.log(1.0 + jnp.exp2(jnp.abs(z) * (-LOG2_E)))


def _att_setup(projb, n_heads):
    s_n = projb.shape[0]
    t_n = _pick(s_n, ATT_BLOCK, 8)
    par = ATT_PAR if n_heads % ATT_PAR == 0 else 1
    cols = [slice(p * HEAD, (p + 1) * HEAD) for p in range(par)]
    wide = par * HEAD
    full = lambda off: pl.BlockSpec((s_n, wide), lambda h, i: (0, off // par + h))
    tile = lambda off: pl.BlockSpec((t_n, wide), lambda h, i: (i, off // par + h))
    return s_n, t_n, par, cols, full, tile


def _att_fwd(name, projb, n_heads):
    h_n = n_heads
    s_n, t_n, par, cols, full, tile_spec = _att_setup(projb, h_n)
    scale = 1.0 / math.sqrt(HEAD)

    def body(q_ref, k_ref, v_ref, o_ref, lt_ref):
        i = pl.program_id(1)
        row = lax.broadcasted_iota(jnp.int32, (t_n, t_n), 0)
        col = lax.broadcasted_iota(jnp.int32, (t_n, t_n), 1)
        from_here = (row >= col).astype(BF16)
        tri = col < row
        qs = [q_ref[:, c] for c in cols]

        def tile(j, carry, diagonal):
            sl = pl.ds(pl.multiple_of(j * t_n, t_n), t_n)
            zs = [lax.dot_general(qs[p], k_ref[sl, c], NT_DIMS, preferred_element_type=F32)
                  for p, c in enumerate(cols)]
            mid = []
            for p in range(par):
                z = zs[p] * scale
                sp = _softplus(z)
                if diagonal:
                    sp = jnp.where(tri, sp, 0.0)
                mid.append((z - carry[p][1], _split_dot(sp, from_here)))
            out = []
            for p, c in enumerate(cols):
                zr, spent = mid[p]
                w = jnp.exp(zr - spent)
                if diagonal:
                    w = jnp.where(tri, w, 0.0)
                acc = carry[p][0] + jnp.dot(w.astype(BF16), v_ref[sl, c], preferred_element_type=F32)
                out.append((acc, carry[p][1] + spent[:, 0:1]))
            return tuple(out)

        init = tuple((jnp.zeros((t_n, HEAD), F32), jnp.zeros((t_n, 1), F32)) for _ in cols)
        carry = tile(i, init, True)
        carry = lax.fori_loop(0, i, lambda jj, cr: tile(i - 1 - jj, cr, False), carry)
        for p, c in enumerate(cols):
            o_ref[:, c] = carry[p][0].astype(o_ref.dtype)
            lt_ref[:, c] = jnp.broadcast_to(carry[p][1], (t_n, HEAD))

    return pl.pallas_call(
        body, name=name, grid=(h_n // par, s_n // t_n),
        in_specs=[tile_spec(4 * h_n), full(5 * h_n), full(6 * h_n)], out_specs=[tile_spec(0), tile_spec(0)],
        out_shape=[jax.ShapeDtypeStruct((s_n, h_n * HEAD), BF16), jax.ShapeDtypeStruct((s_n, h_n * HEAD), F32)],
        compiler_params=_params("parallel", "arbitrary"),
    )(projb, projb, projb)


def _att_bwd(name, projb, dcat, spent_all, n_heads):
    h_n = n_heads
    s_n, t_n, par, cols, full, tile_spec = _att_setup(projb, h_n)
    scale = 1.0 / math.sqrt(HEAD)

    def body(q_ref, k_ref, v_ref, do_ref, lt_ref, dq_ref, dk_ref, dv_ref):
        i = pl.program_id(1)

        @pl.when(i == 0)
        def _():
            dk_ref[...] = jnp.zeros_like(dk_ref)
            dv_ref[...] = jnp.zeros_like(dv_ref)

        row = lax.broadcasted_iota(jnp.int32, (t_n, t_n), 0)
        col = lax.broadcasted_iota(jnp.int32, (t_n, t_n), 1)
        before = (row < col).astype(BF16)
        upto = (row <= col).astype(BF16)
        tri = col < row
        qs = [q_ref[:, c] for c in cols]
        dos = [do_ref[:, c].astype(BF16) for c in cols]
        last = slice(t_n - 1, t_n)

        def tile(j, carry, diagonal):
            sl = pl.ds(pl.multiple_of(j * t_n, t_n), t_n)
            zs = [lax.dot_general(qs[p], k_ref[sl, c], NT_DIMS, preferred_element_type=F32)
                  for p, c in enumerate(cols)]
            dws = [lax.dot_general(dos[p], v_ref[sl, c], NT_DIMS, preferred_element_type=F32)
                   for p, c in enumerate(cols)]
            mid1 = []
            for p in range(par):
                z = zs[p] * scale
                sp = _softplus(z)
                sg = jnp.exp(z - sp)
                if diagonal:
                    sp = jnp.where(tri, sp, 0.0)
                prior = _split_dot(sp, before)
                mid1.append((z - carry[p][1], sg, prior, prior[:, last] + sp[:, last]))
            mid2 = []
            for p in range(par):
                zb, sg, prior, sp_sum = mid1[p]
                w = jnp.exp(zb + prior)
                if diagonal:
                    w = jnp.where(tri, w, 0.0)
                e = dws[p] * w
                mid2.append((w.astype(BF16), e, sg, _split_dot(e, upto), sp_sum))
            out = []
            for p, c in enumerate(cols):
                wb, e, sg, e_upto, sp_sum = mid2[p]
                dz = (e - sg * (carry[p][2] + e_upto)) * scale
                if diagonal:
                    dz = jnp.where(tri, dz, 0.0)
                dz = dz.astype(BF16)
                dq = carry[p][0] + jnp.dot(dz, k_ref[sl, c], preferred_element_type=F32)
                dk_ref[sl, c] += lax.dot_general(dz, qs[p], TN_DIMS, preferred_element_type=F32)
                dv_ref[sl, c] += lax.dot_general(wb, dos[p], TN_DIMS, preferred_element_type=F32)
                out.append((dq, carry[p][1] - sp_sum, carry[p][2] + e_upto[:, last]))
            return tuple(out)

        init = tuple((jnp.zeros((t_n, HEAD), F32), lt_ref[:, c][:, 0:1], jnp.zeros((t_n, 1), F32)) for c in cols)
        carry = lax.fori_loop(0, i, lambda j, cr: tile(j, cr, False), init)
        carry = tile(i, carry, True)
        for p, c in enumerate(cols):
            dq_ref[:, c] = carry[p][0].astype(dq_ref.dtype)

    return pl.pallas_call(
        body, name=name, grid=(h_n // par, s_n // t_n),
        in_specs=[tile_spec(4 * h_n), full(5 * h_n), full(6 * h_n), tile_spec(h_n), tile_spec(0)],
        out_specs=[tile_spec(0), full(0), full(0)],
        out_shape=[jax.ShapeDtypeStruct((s_n, h_n * HEAD), BF16), jax.ShapeDtypeStruct((s_n, h_n * HEAD), F32),
                   jax.ShapeDtypeStruct((s_n, h_n * HEAD), F32)],
        compiler_params=_params("arbitrary", "arbitrary"),
    )(projb, projb, projb, dcat, spent_all)


def _adamw(name, w, g, m, v, tr=256):
    r_n, c_n = w.shape
    tr = _pick(r_n, tr, 8)
    c1 = 1.0 - ADAM_B1 ** ADAM_STEP
    c2 = 1.0 - ADAM_B2 ** ADAM_STEP

    def body(w_ref, g_ref, m_ref, v_ref, d_ref, nm_ref, nv_ref):
        gv = g_ref[...]
        nm = ADAM_B1 * m_ref[...] + (1.0 - ADAM_B1) * gv
        nv = ADAM_B2 * v_ref[...] + (1.0 - ADAM_B2) * (gv * gv)
        d_ref[...] = -ADAM_LR * ((nm / c1) / (jnp.sqrt(nv / c2) + ADAM_EPS) + ADAM_WD * w_ref[...])
        nm_ref[...] = nm
        nv_ref[...] = nv

    blk = pl.BlockSpec((tr, c_n), lambda i: (i, 0))
    sds = jax.ShapeDtypeStruct((r_n, c_n), F32)
    return pl.pallas_call(
        body, name=name, grid=(r_n // tr,), in_specs=[blk] * 4, out_specs=[blk] * 3, out_shape=[sds] * 3,
        compiler_params=_params("parallel"),
    )(w, g, m, v)


def _adamw_nd(name, w, g, m, v):
    shape = w.shape
    flat = lambda t: t.reshape(-1, shape[-1])
    return tuple(t.reshape(shape) for t in _adamw(name, flat(w), flat(g.reshape(shape)), flat(m), flat(v)))


def _mesh_pos():
    x, y, c = lax.axis_index("x"), lax.axis_index("y"), lax.axis_index("c")
    chips = [(1 - x, y), (x, 1 - y), (1 - x, 1 - y)]
    return x, y, c, chips, 2 * x + y, [2 * cx + cy for cx, cy in chips]


class _Unit:
    def __init__(self, shard_shape, axis, half_axis):
        self.shard_shape = tuple(shard_shape)
        self.axis = axis
        self.half_axis = half_axis
        self.full_shape = tuple(n * N_CHIPS if a == axis else n for a, n in enumerate(shard_shape))
        self.half_shape = tuple(n // 2 if a == half_axis else n for a, n in enumerate(shard_shape))

    def _window(self, ref, k, c, with_slab):
        idx = []
        for a, n in enumerate(self.shard_shape):
            start, size = 0, n
            if a == self.half_axis:
                size = n // 2
                start = c * size
            if with_slab and a == self.axis:
                start = start + k * n
            idx.append(pl.ds(start, size))
        return ref.at[tuple(idx)]

    def full_half(self, ref, k, c):
        return self._window(ref, k, c, True)

    def place_view(self):
        s = self.shard_shape
        if self.axis == len(s) - 1:
            return math.prod(s[:-2]), s[-2], s[-1], True
        assert self.axis == len(s) - 2
        return math.prod(s[:self.axis]), s[self.axis], s[-1], False

    def half_view(self):
        s, h = self.shard_shape, self.half_axis
        if h == len(s) - 1:
            return math.prod(s[:-2]), s[-2], s[-1] // 2, True
        return math.prod(s[:h]), (s[h] // 2) * math.prod(s[h + 1:-1]), s[-1], False


def _place_shard(name, shard, unit, chip_idx):
    l_n, r_n, c_n, by_cols = unit.place_view()
    tr = _pick(r_n, 256, 16)
    per = r_n // tr

    def body(k_ref, s_ref, o_ref):
        o_ref[...] = s_ref[...].astype(o_ref.dtype)

    if by_cols:
        full3, out_index = (l_n, r_n, N_CHIPS * c_n), (lambda l, i, k_ref: (l, i, k_ref[0]))
    else:
        full3, out_index = (l_n, N_CHIPS * r_n, c_n), (lambda l, i, k_ref: (l, k_ref[0] * per + i, 0))
    out = pl.pallas_call(
        body, name=name,
        grid_spec=pltpu.PrefetchScalarGridSpec(
            num_scalar_prefetch=1, grid=(l_n, per),
            in_specs=[pl.BlockSpec((None, tr, c_n), lambda l, i, k_ref: (l, i, 0))],
            out_specs=pl.BlockSpec((None, tr, c_n), out_index)),
        out_shape=jax.ShapeDtypeStruct(full3, BF16), compiler_params=_params("parallel", "parallel"),
    )(chip_idx, shard.reshape(l_n, r_n, c_n))
    return out.reshape(unit.full_shape)


def _gather_weights(units, fulls, scale_shard):
    nu = len(units)
    ps = scale_shard.shape[1]

    def body(*refs):
        sc_in = refs[nu]
        outs, sc_out = refs[nu + 1:2 * nu + 1], refs[2 * nu + 1]
        send1, recv1, send2, recv2, send3, recv3, lsem = refs[2 * nu + 2:]
        x, y, c, chips, me, others = _mesh_pos()
        local = [pltpu.make_async_copy(sc_in, sc_out.at[:, pl.ds(me * ps, ps)], lsem.at[0])]
        for cp in local:
            cp.start()
        sends = []
        for u in range(nu):
            for j, chip in enumerate(chips):
                mine = units[u].full_half(outs[u], me, c)
                sends.append(pltpu.make_async_remote_copy(
                    src_ref=mine, dst_ref=mine, send_sem=send1.at[3 * u + j], recv_sem=recv1.at[3 * u + j],
                    device_id=(*chip, c), device_id_type=MESH))
        for j, chip in enumerate(chips):
            sends.append(pltpu.make_async_remote_copy(
                src_ref=sc_in, dst_ref=sc_out.at[:, pl.ds(me * ps, ps)], send_sem=send3.at[j], recv_sem=recv3.at[j],
                device_id=(*chip, c), device_id_type=MESH))
        for cp in sends:
            cp.start()
        for u in range(nu):
            for j in range(3):
                landed = units[u].full_half(outs[u], others[j], c)
                pltpu.make_async_remote_copy(
                    src_ref=landed, dst_ref=landed, send_sem=send1.at[3 * u + j], recv_sem=recv1.at[3 * u + j],
                    device_id=(x, y, c), device_id_type=MESH).wait_recv()
                fwd = pltpu.make_async_remote_copy(
                    src_ref=landed, dst_ref=landed, send_sem=send2.at[3 * u + j], recv_sem=recv2.at[3 * u + j],
                    device_id=(x, y, 1 - c), device_id_type=MESH)
                fwd.start()
                sends.append(fwd)
        for u in range(nu):
            for j in range(3):
                theirs = units[u].full_half(outs[u], others[j], 1 - c)
                pltpu.make_async_remote_copy(
                    src_ref=theirs, dst_ref=theirs, send_sem=send2.at[3 * u + j], recv_sem=recv2.at[3 * u + j],
                    device_id=(x, y, c), device_id_type=MESH).wait_recv()
        for j in range(3):
            dst = sc_out.at[:, pl.ds(others[j] * ps, ps)]
            pltpu.make_async_remote_copy(src_ref=dst, dst_ref=dst, send_sem=send3.at[j], recv_sem=recv3.at[j],
                                         device_id=(x, y, c), device_id_type=MESH).wait_recv()
        for cp in sends:
            cp.wait_send()
        for cp in local:
            cp.wait()

    out_shape = [jax.ShapeDtypeStruct(f.shape, f.dtype) for f in fulls]
    out_shape.append(jax.ShapeDtypeStruct((1, N_CHIPS * ps), scale_shard.dtype))
    dma = pltpu.SemaphoreType.DMA
    return pl.pallas_call(
        body, name="gather_weights", in_specs=[ANY] * (nu + 1), out_specs=[ANY] * (nu + 1), out_shape=out_shape,
        input_output_aliases={u: u for u in range(nu)},
        scratch_shapes=[dma((3 * nu,)), dma((3 * nu,)), dma((3 * nu,)), dma((3 * nu,)), dma((3,)), dma((3,)),
                        dma((1,))],
    )(*fulls, scale_shard)


def _to_sibling(grads):
    nu = len(grads)

    def body(*refs):
        ins, outs = refs[:nu], refs[nu:2 * nu]
        send, recv = refs[2 * nu:]
        x, y, c, _, _, _ = _mesh_pos()
        cps = [pltpu.make_async_remote_copy(
            src_ref=ins[u].at[:, 1 - c], dst_ref=outs[u], send_sem=send.at[u], recv_sem=recv.at[u],
            device_id=(x, y, 1 - c), device_id_type=MESH) for u in range(nu)]
        for cp in cps:
            cp.start()
        for cp in cps:
            cp.wait()

    out_shape = [jax.ShapeDtypeStruct((g.shape[0],) + g.shape[2:], g.dtype) for g in grads]
    dma = pltpu.SemaphoreType.DMA
    return pl.pallas_call(
        body, name="grads_to_sibling", in_specs=[ANY] * nu, out_specs=[ANY] * nu, out_shape=out_shape,
        scratch_shapes=[dma((nu,)), dma((nu,))],
    )(*grads)


def _to_owners(partials):
    nu = len(partials)

    def body(*refs):
        ins, outs = refs[:nu], refs[nu:2 * nu]
        send, recv = refs[2 * nu:]
        x, y, c, chips, me, others = _mesh_pos()
        cps = [pltpu.make_async_remote_copy(
            src_ref=ins[u].at[others[j]], dst_ref=outs[u].at[j], send_sem=send.at[3 * u + j],
            recv_sem=recv.at[3 * u + j], device_id=(*chips[j], c), device_id_type=MESH)
            for u in range(nu) for j in range(3)]
        for cp in cps:
            cp.start()
        for cp in cps:
            cp.wait()

    out_shape = [jax.ShapeDtypeStruct((3,) + p.shape[1:], p.dtype) for p in partials]
    dma = pltpu.SemaphoreType.DMA
    return pl.pallas_call(
        body, name="partials_to_owners", in_specs=[ANY] * nu, out_specs=[ANY] * nu, out_shape=out_shape,
        scratch_shapes=[dma((3 * nu,)), dma((3 * nu,))],
    )(*partials)


def _share_halves(halves):
    nu = len(halves)

    def body(*refs):
        ins, outs = refs[:nu], refs[nu:2 * nu]
        send, recv = refs[2 * nu:]
        x, y, c, _, _, _ = _mesh_pos()
        cps = [pltpu.make_async_remote_copy(
            src_ref=ins[u], dst_ref=outs[u], send_sem=send.at[u], recv_sem=recv.at[u],
            device_id=(x, y, 1 - c), device_id_type=MESH) for u in range(nu)]
        for cp in cps:
            cp.start()
        for cp in cps:
            cp.wait()

    out_shape = [jax.ShapeDtypeStruct(h.shape, h.dtype) for h in halves]
    dma = pltpu.SemaphoreType.DMA
    return pl.pallas_call(
        body, name="share_halves", in_specs=[ANY] * nu, out_specs=[ANY] * nu, out_shape=out_shape,
        scratch_shapes=[dma((nu,)), dma((nu,))],
    )(*halves)


def _add_mine(name, grad, recv, c_idx):
    _, _, r_n, c_n = grad.shape
    tr = _pick(r_n, 256, 16)

    def body(c_ref, g_ref, r_ref, o_ref):
        o_ref[...] = (g_ref[...] + r_ref[...]).astype(o_ref.dtype)

    return pl.pallas_call(
        body, name=name,
        grid_spec=pltpu.PrefetchScalarGridSpec(
            num_scalar_prefetch=1, grid=(N_CHIPS, r_n // tr),
            in_specs=[pl.BlockSpec((None, None, tr, c_n), lambda k, i, c_ref: (k, c_ref[0], i, 0)),
                      pl.BlockSpec((None, tr, c_n), lambda k, i, c_ref: (k, i, 0))],
            out_specs=pl.BlockSpec((None, tr, c_n), lambda k, i, c_ref: (k, i, 0))),
        out_shape=jax.ShapeDtypeStruct(recv.shape, BF16), compiler_params=_params("parallel", "parallel"),
    )(c_idx, grad, recv)


def _add_slots(name, partial, slots, chip_idx):
    _, r_n, c_n = slots.shape
    tr = _pick(r_n, 256, 16)

    def body(k_ref, p_ref, s_ref, o_ref):
        own = p_ref[...].astype(F32)
        o_ref[...] = ((own + s_ref[0].astype(F32)) + s_ref[1].astype(F32)) + s_ref[2].astype(F32)

    return pl.pallas_call(
        body, name=name,
        grid_spec=pltpu.PrefetchScalarGridSpec(
            num_scalar_prefetch=1, grid=(r_n // tr,),
            in_specs=[pl.BlockSpec((None, tr, c_n), lambda i, k_ref: (k_ref[0], i, 0)),
                      pl.BlockSpec((3, tr, c_n), lambda i, k_ref: (0, i, 0))],
            out_specs=pl.BlockSpec((tr, c_n), lambda i, k_ref: (i, 0))),
        out_shape=jax.ShapeDtypeStruct((r_n, c_n), F32), compiler_params=_params("parallel"),
    )(chip_idx, partial, slots)


def _adamw_halves(name, unit, w, m, v, mine, theirs, c_idx, tr=256):
    l_n, r_n, c_n, by_cols = unit.half_view()
    tr = _pick(r_n, tr, 8)
    c1 = 1.0 - ADAM_B1 ** ADAM_STEP
    c2 = 1.0 - ADAM_B2 ** ADAM_STEP

    def body(c_ref, w_ref, m_ref, v_ref, mine_ref, theirs_ref, g_ref, d_ref, nm_ref, nv_ref):
        gv = jnp.where(pl.program_id(1) == c_ref[0], mine_ref[...], theirs_ref[...])
        nm = ADAM_B1 * m_ref[...] + (1.0 - ADAM_B1) * gv
        nv = ADAM_B2 * v_ref[...] + (1.0 - ADAM_B2) * (gv * gv)
        d_ref[...] = -ADAM_LR * ((nm / c1) / (jnp.sqrt(nv / c2) + ADAM_EPS) + ADAM_WD * w_ref[...])
        g_ref[...] = gv
        nm_ref[...] = nm
        nv_ref[...] = nv

    if by_cols:
        view = (l_n, r_n, 2 * c_n)
        whole = pl.BlockSpec((None, tr, c_n), lambda l, h, i, c_ref: (l, i, h))
    else:
        view = (l_n, 2, r_n, c_n)
        whole = pl.BlockSpec((None, None, tr, c_n), lambda l, h, i, c_ref: (l, h, i, 0))
    half = pl.BlockSpec((None, tr, c_n), lambda l, h, i, c_ref: (l, i, 0))
    sds = jax.ShapeDtypeStruct(view, F32)
    outs = pl.pallas_call(
        body, name=name,
        grid_spec=pltpu.PrefetchScalarGridSpec(
            num_scalar_prefetch=1, grid=(l_n, 2, r_n // tr),
            in_specs=[whole, whole, whole, half, half], out_specs=[whole] * 4),
        out_shape=[sds] * 4, compiler_params=_params("parallel", "parallel", "parallel"),
    )(c_idx, w.reshape(view), m.reshape(view), v.reshape(view),
      mine.reshape(l_n, r_n, c_n), theirs.reshape(l_n, r_n, c_n))
    return tuple(t.reshape(w.shape) for t in outs)


def _allreduce_small(block):
    r_n, c_n = block.shape

    def body(in_ref, out_ref, slots, send, recv):
        x, y, c = lax.axis_index("x"), lax.axis_index("y"), lax.axis_index("c")
        me = 4 * x + 2 * y + c
        slots[me] = in_ref[...]
        flips = [(fx, fy, fc) for fx in (0, 1) for fy in (0, 1) for fc in (0, 1)][1:]
        peers = [(x ^ fx, y ^ fy, c ^ fc) for fx, fy, fc in flips]
        cps = [pltpu.make_async_remote_copy(src_ref=in_ref, dst_ref=slots.at[me], send_sem=send.at[j],
                                            recv_sem=recv.at[j], device_id=peers[j], device_id_type=MESH)
               for j in range(7)]
        for cp in cps:
            cp.start()
        for j, (px, py, pc) in enumerate(peers):
            slot = slots.at[4 * px + 2 * py + pc]
            pltpu.make_async_remote_copy(src_ref=slot, dst_ref=slot, send_sem=send.at[j], recv_sem=recv.at[j],
                                         device_id=(x, y, c), device_id_type=MESH).wait_recv()
        for cp in cps:
            cp.wait_send()
        total = slots[0]
        for d in range(1, 8):
            total = total + slots[d]
        out_ref[...] = total

    vmem = pl.BlockSpec(memory_space=pltpu.VMEM)
    return pl.pallas_call(
        body, name="allreduce_small", in_specs=[vmem], out_specs=vmem,
        out_shape=jax.ShapeDtypeStruct((r_n, c_n), F32),
        scratch_shapes=[pltpu.VMEM((8, r_n, c_n), F32), pltpu.SemaphoreType.DMA((7,)), pltpu.SemaphoreType.DMA((7,))],
    )(block)


def _first(accs, extras):
    return [accs[0]] if isinstance(accs, list) else [accs]


def _ffn_fwd(tag, x_in, h, wg, wu, wd):
    def act(accs, extras):
        a, b = accs
        return [a, b, a * _sigmoid(a) * b]

    a, b, s = _mm_nn(f"ffn_up_{tag}", h, [wg, wu], [], act, [F32, F32, BF16])
    x_out, = _mm_nn(f"ffn_down_{tag}", s, [wd], [x_in], lambda accs, ex: [ex[0] + accs[0]], [F32])
    return x_out, a, b, s


def _ffn_bwd(tag, layer, dx_out, h, a, b, s, wg, wu, wd, into):
    def mid(acc, extras):
        av, bv = extras
        sg = _sigmoid(av)
        return [acc * bv * _dsilu(av, sg), acc * (av * sg)]

    dxb = dx_out.astype(BF16)
    da, db = _mm_nt(f"ffn_dact_{tag}", [(dxb, wd)], [a, b], mid, [BF16, BF16])
    dh, = _mm_nt(f"ffn_dh_{tag}", [(da, wg), (db, wu)], [], _first, [F32])
    d_n, f_n = wg.shape[1], wg.shape[2]
    ns = f_n // N_CHIPS
    tki = _pick(d_n // 2, 512)
    ih = (d_n // 2) // tki
    col_shape = (N_CHIPS, 2, 2, d_n // 2, ns)
    col_block = (None, None, None, tki, ns)
    col_index = lambda g, i, j: (j, i // ih, layer, i % ih, 0)
    dwg = _mm_tn(f"ffn_dwg_{tag}", h, da, 1, col_shape, col_block, col_index, tki=tki, tn=ns, into=into[0])
    dwu = _mm_tn(f"ffn_dwu_{tag}", h, db, 1, col_shape, col_block, col_index, tki=tki, tn=ns, into=into[1])
    tn = _pick(d_n // 2, 512)
    jh = (d_n // 2) // tn
    dwd = _mm_tn(f"ffn_dwd_{tag}", s, dxb, 1, (N_CHIPS, 2, 2, ns, d_n // 2), (None, None, None, ns, tn),
                 lambda g, i, j: (i, j // jh, layer, 0, j % jh), tki=ns, tn=tn, into=into[2])
    return dh, (dwg, dwu, dwd)


def kernel(x, mix_norm, ffn_norm, final_norm, ab_w_in, lb_logits, hg_out_norm, ab_w_out, pool_w, pool_scale, ffn_w_gate, ffn_w_up, ffn_w_down, loss_target, m_mix_norm, m_ffn_norm, m_final_norm, m_ab_w_in, m_lb_logits, m_hg_out_norm, m_ab_w_out, m_pool_w, m_pool_scale, m_ffn_w_gate, m_ffn_w_up, m_ffn_w_down, v_mix_norm, v_ffn_norm, v_final_norm, v_ab_w_in, v_lb_logits, v_hg_out_norm, v_ab_w_out, v_pool_w, v_pool_scale, v_ffn_w_gate, v_ffn_w_up, v_ffn_w_down):
    xs, target = x[0], loss_target[0]
    s_n, d_n = xs.shape
    h_n = d_n // 2 // HEAD
    hw = h_n * HEAD
    n_grp = len(POOL_WINDOWS)
    grp = d_n // n_grp
    c_idx = lax.axis_index("c").astype(jnp.int32).reshape(1)
    chip = 2 * lax.axis_index("x") + lax.axis_index("y")

    units = [
        _Unit(ab_w_in.shape[1:], 1, 0),
        _Unit(ab_w_out.shape[1:], 0, 0),
        _Unit(pool_w.shape[1:], 1, 0),
        _Unit(ffn_w_gate.shape, 2, 1),
        _Unit(ffn_w_up.shape, 2, 1),
        _Unit(ffn_w_down.shape, 1, 2),
    ]
    chip_idx = chip.astype(jnp.int32).reshape(1)
    shards = [ab_w_in[0], ab_w_out[0], pool_w[0], ffn_w_gate, ffn_w_up, ffn_w_down]
    placed = [_place_shard(f"place_{n}", t, u, chip_idx) for n, (t, u) in enumerate(zip(shards, units))]
    w_in, w_out, w_pool, w_gate, w_up, w_down, scale_full = _gather_weights(units, placed, pool_scale)
    w_in3, w_out3 = w_in[None], w_out[None]
    row = lambda t: t.reshape(1, -1)

    h0 = _rms_fwd("norm_mix0", xs, row(mix_norm[0]))
    proj, projb = _mm_nn("proj_in", h0, [w_in3], [], lambda accs, ex: [accs[0], accs[0]], [F32, BF16])
    oraw, o_a, states = _hg_fwd("hgrn_fwd", proj, lb_logits, hg_out_norm, h_n)
    o_b, ltot = _att_fwd("attn_fwd", projb, h_n)
    cat = jnp.concatenate([o_a, o_b], axis=1)
    x1, = _mm_nn("proj_out", cat, [w_out3], [xs], lambda accs, ex: [ex[0] + accs[0]], [F32])
    h1 = _rms_fwd("norm_ffn0", x1, row(ffn_norm[0]))
    x2, a0, b0, s0 = _ffn_fwd("l0", x1, h1, w_gate[0:1], w_up[0:1], w_down[0:1])
    pooled = _pool_fwd("pool_fwd", x2, row(mix_norm[1]))
    x3, mixed = _mm_nn("pool_mix", pooled, [w_pool], [x2, scale_full],
                       lambda accs, ex: [ex[0] + accs[0] * ex[1], accs[0]], [F32, F32], tk=grp, tn=grp)
    h3 = _rms_fwd("norm_ffn1", x3, row(ffn_norm[1]))
    x4, a1, b1, s1 = _ffn_fwd("l1", x3, h3, w_gate[1:2], w_up[1:2], w_down[1:2])

    dx4, d_final, loss = _loss_bwd("loss_bwd", x4, target, row(final_norm))
    dh3, ffn_grads = _ffn_bwd("l1", 1, dx4, h3, a1, b1, s1, w_gate[1:2], w_up[1:2], w_down[1:2], (None, None, None))
    dx3, d_ffn1 = _rms_bwd("norm_ffn1_bwd", dh3, x3, row(ffn_norm[1]), dx4)
    dmixed, d_scale = _scale_bwd("pool_scale_bwd", dx3, mixed, scale_full)
    dpooled, = _mm_nt("pool_dpooled", [(dmixed, w_pool)], [], _first, [F32], to=grp, tr=grp)
    slab_rows = grp // N_CHIPS
    d_pool = _mm_tn("pool_dw", pooled, dmixed, n_grp, (N_CHIPS, 2, n_grp // 2, slab_rows, grp),
                    (None, None, None, slab_rows, grp), lambda g, i, j: (i, g // 2, g % 2, 0, 0),
                    tki=slab_rows, tn=grp)
    dx2, d_mix1 = _pool_bwd("pool_bwd", dpooled, x2, row(mix_norm[1]), dx3)
    dh1, ffn_grads = _ffn_bwd("l0", 0, dx2, h1, a0, b0, s0, w_gate[0:1], w_up[0:1], w_down[0:1], ffn_grads)
    dx1, d_ffn0 = _rms_bwd("norm_ffn0_bwd", dh1, x1, row(ffn_norm[0]), dx2)
    dx1b = dx1.astype(BF16)
    dcat, = _mm_nt("proj_out_dcat", [(dx1b, w_out3)], [], _first, [F32])
    d_wout = _mm_tn("proj_out_dw", cat, dx1b, 1, (1, 2 * hw, d_n), (None, _pick(2 * hw, 512), _pick(d_n, 512)),
                    lambda g, i, j: (g, i, j), tki=_pick(2 * hw, 512), tn=_pick(d_n, 512))
    dqb, dkb, dvb = _att_bwd("attn_bwd", projb, dcat, ltot, h_n)
    dqa, dfa, dia, dga, d_lb, d_hgn = _hg_bwd("hgrn_bwd", proj, dcat, oraw, states, lb_logits, hg_out_norm, h_n)
    dproj = jnp.concatenate([dqa, dfa, dia, dga, dqb, dkb.astype(BF16), dvb.astype(BF16)], axis=1)
    dh0, = _mm_nt("proj_in_dh", [(dproj, w_in3)], [], _first, [F32])
    ns_in = 7 * hw // N_CHIPS
    tki_in, tn_in = _pick(d_n // 2, 512), _pick(ns_in, 896)
    ih_in, jps_in = (d_n // 2) // tki_in, ns_in // tn_in
    d_win = _mm_tn("proj_in_dw", h0, dproj, 1, (N_CHIPS, 2, d_n // 2, ns_in), (None, None, tki_in, tn_in),
                   lambda g, i, j: (j // jps_in, i // ih_in, i % ih_in, j % jps_in), tki=tki_in, tn=tn_in)
    dx0, d_mix0 = _rms_bwd("norm_mix0_bwd", dh0, xs, row(mix_norm[0]), dx1)

    full_grads = [d_win, d_wout, d_pool, *ffn_grads]
    as4 = lambda g, u: g.reshape(N_CHIPS, 2, -1, u.half_shape[-1])
    grads4 = [as4(g, u) for g, u in zip(full_grads, units)]
    from_sibling = _to_sibling(grads4)
    partials = [_add_mine(f"add_sibling_{n}", g, r, c_idx) for n, (g, r) in enumerate(zip(grads4, from_sibling))]
    slots = _to_owners(partials)
    mine = [_add_slots(f"add_chips_{n}", p, s, chip_idx) for n, (p, s) in enumerate(zip(partials, slots))]
    theirs = _share_halves(mine)

    lanes = 2 * d_n
    pad = lambda t: jnp.pad(t.reshape(1, -1), ((0, 0), (0, lanes - t.size)))
    small = jnp.concatenate([
        pad(jnp.concatenate([d_mix0, d_mix1], axis=0)), pad(jnp.concatenate([d_ffn0, d_ffn1], axis=0)),
        pad(d_final), pad(d_lb), pad(d_hgn), pad(d_scale), jnp.zeros((2, lanes), F32)], axis=0)
    small = _allreduce_small(small)
    g_mix = small[0, :2 * d_n].reshape(2, d_n)
    g_ffn = small[1, :2 * d_n].reshape(2, d_n)
    g_final = small[2, :d_n]
    g_lb = small[3, :2 * hw].reshape(2, hw)
    g_hgn = small[4, :HEAD].reshape(1, HEAD)
    g_scale = lax.dynamic_slice(small[5, :d_n], (chip * grp,), (grp,)).reshape(1, grp)
    loss = lax.psum(loss[0, 0], ("x", "y", "c"))

    small_grads = {0: g_mix, 1: g_ffn, 2: g_final, 4: g_lb, 5: g_hgn, 8: g_scale}
    unit_of = {3: 0, 6: 1, 7: 2, 9: 3, 10: 4, 11: 5}
    weights = [mix_norm, ffn_norm, final_norm, ab_w_in, lb_logits, hg_out_norm, ab_w_out, pool_w, pool_scale,
               ffn_w_gate, ffn_w_up, ffn_w_down]
    ms = [m_mix_norm, m_ffn_norm, m_final_norm, m_ab_w_in, m_lb_logits, m_hg_out_norm, m_ab_w_out, m_pool_w,
          m_pool_scale, m_ffn_w_gate, m_ffn_w_up, m_ffn_w_down]
    vs = [v_mix_norm, v_ffn_norm, v_final_norm, v_ab_w_in, v_lb_logits, v_hg_out_norm, v_ab_w_out, v_pool_w,
          v_pool_scale, v_ffn_w_gate, v_ffn_w_up, v_ffn_w_down]
    grads, deltas, new_ms, new_vs = [], [], [], []
    for n, (w, m, v) in enumerate(zip(weights, ms, vs)):
        if n in unit_of:
            u = unit_of[n]
            g, d, nm, nv = _adamw_halves(f"adamw_{n}", units[u], w, m, v, mine[u], theirs[u], c_idx)
        else:
            w2 = w.reshape(1, -1) if w.ndim == 1 else w
            g = small_grads[n].reshape(w2.shape)
            d, nm, nv = _adamw_nd(f"adamw_{n}", w2, g, m.reshape(w2.shape), v.reshape(w2.shape))
        grads.append(g.reshape(w.shape))
        deltas.append(d.reshape(w.shape))
        new_ms.append(nm.reshape(w.shape))
        new_vs.append(nv.reshape(w.shape))
    return (loss, dx0[None], *grads, *deltas, *new_ms, *new_vs)
```

```python
import functools
import math

import jax
import jax.numpy as jnp
from jax import lax
from jax.experimental import pallas as pl
from jax.experimental.pallas import tpu as pltpu

F32 = jnp.float32
BF16 = jnp.bfloat16
HIGHEST = lax.Precision.HIGHEST
MESH = pl.DeviceIdType.MESH
ANY = pl.BlockSpec(memory_space=pl.ANY)

RMS_EPS = 1e-6
LOG2_E = 1.4426950408889634
HEAD = 128
HG_CHUNK = 64
HG_MID = HG_CHUNK // 2 - 1
HG_BLOCK = 512
HG_PAR = 4
ATT_BLOCK = 256
ATT_PAR = 2
POOL_WINDOWS = (2, 4, 8, 16)
POOL_HALO = 16
N_CHIPS = 4
ADAM_LR, ADAM_B1, ADAM_B2, ADAM_EPS, ADAM_WD, ADAM_STEP = 0.001, 0.9, 0.999, 1e-08, 0.01, 10
VMEM_LIMIT = 56 * 1024 * 1024

NT_DIMS = (((1,), (1,)), ((), ()))
TN_DIMS = (((0,), (0,)), ((), ()))


def _params(*sem):
    return pltpu.CompilerParams(dimension_semantics=sem, vmem_limit_bytes=VMEM_LIMIT)


def _pick(dim, pref, unit=128):
    best = None
    for t in range(unit, min(dim, pref) + 1, unit):
        if dim % t == 0:
            best = t
    return dim if best is None else best


def _sigmoid(z):
    return 1.0 / (1.0 + jnp.exp(-z))


def _dsilu(a, sg):
    return sg * (1.0 + a * (1.0 - sg))


def _mm_nn(name, a, bs, extras, epilogue, out_dtypes, *, tm=1024, tn=512, tk=2048):
    g_n, k_n, n_n = bs[0].shape
    m_n = a.shape[0]
    tm, tn, tk = _pick(m_n, tm, 8), _pick(n_n, tn), _pick(k_n, tk)
    i_n, j_n, kt = m_n // tm, n_n // tn, k_n // tk
    nb, ne, no = len(bs), len(extras), len(out_dtypes)

    def body(*refs):
        a_ref, b_refs, e_refs = refs[0], refs[1:1 + nb], refs[1 + nb:1 + nb + ne]
        o_refs, acc_refs = refs[1 + nb + ne:1 + nb + ne + no], refs[1 + nb + ne + no:]
        k = pl.program_id(3)
        av = a_ref[...]
        prods = [jnp.dot(av, b_ref[...], preferred_element_type=F32) for b_ref in b_refs]

        def finish(accs):
            outs = epilogue(accs, [e[...] for e in e_refs])
            for o_ref, o in zip(o_refs, outs):
                o_ref[...] = o.astype(o_ref.dtype)

        if kt == 1:
            finish(prods)
        else:
            @pl.when(k == 0)
            def _():
                for acc, p in zip(acc_refs, prods):
                    acc[...] = p

            @pl.when(k > 0)
            def _():
                for acc, p in zip(acc_refs, prods):
                    acc[...] += p

            @pl.when(k == kt - 1)
            def _():
                finish([acc[...] for acc in acc_refs])

    in_specs = [pl.BlockSpec((tm, tk), lambda g, i, j, k: (i, g * kt + k))]
    in_specs += [pl.BlockSpec((None, tk, tn), lambda g, i, j, k: (g, k, j)) for _ in bs]
    for e in extras:
        if e.shape[0] == 1:
            in_specs.append(pl.BlockSpec((1, tn), lambda g, i, j, k: (0, g * j_n + j)))
        else:
            in_specs.append(pl.BlockSpec((tm, tn), lambda g, i, j, k: (i, g * j_n + j)))
    out_specs = [pl.BlockSpec((tm, tn), lambda g, i, j, k: (i, g * j_n + j)) for _ in out_dtypes]
    out_shape = [jax.ShapeDtypeStruct((m_n, g_n * n_n), dt) for dt in out_dtypes]
    scratch = [] if kt == 1 else [pltpu.VMEM((tm, tn), F32) for _ in bs]
    return pl.pallas_call(
        body, name=name, grid=(g_n, i_n, j_n, kt), in_specs=in_specs, out_specs=out_specs, out_shape=out_shape,
        scratch_shapes=scratch, compiler_params=_params("parallel", "parallel", "parallel", "arbitrary"),
    )(a, *bs, *extras)


def _mm_nt(name, pairs, extras, epilogue, out_dtypes, *, tm=1024, to=512, tr=2048):
    g_n, kd, n_n = pairs[0][1].shape
    m_n = pairs[0][0].shape[0]
    tm, to, tr = _pick(m_n, tm, 8), _pick(kd, to), _pick(n_n, tr)
    i_n, j_n, rt = m_n // tm, kd // to, n_n // tr
    npairs, ne, no = len(pairs), len(extras), len(out_dtypes)

    def body(*refs):
        ab_refs, e_refs = refs[:2 * npairs], refs[2 * npairs:2 * npairs + ne]
        o_refs, acc_refs = refs[2 * npairs + ne:2 * npairs + ne + no], refs[2 * npairs + ne + no:]
        r = pl.program_id(3)
        prod = None
        for p in range(npairs):
            t = lax.dot_general(ab_refs[2 * p][...], ab_refs[2 * p + 1][...], NT_DIMS, preferred_element_type=F32)
            prod = t if prod is None else prod + t

        def finish(acc):
            outs = epilogue(acc, [e[...] for e in e_refs])
            for o_ref, o in zip(o_refs, outs):
                o_ref[...] = o.astype(o_ref.dtype)

        if rt == 1:
            finish(prod)
        else:
            acc = acc_refs[0]

            @pl.when(r == 0)
            def _():
                acc[...] = prod

            @pl.when(r > 0)
            def _():
                acc[...] += prod

            @pl.when(r == rt - 1)
            def _():
                finish(acc[...])

    in_specs, args = [], []
    for a, b in pairs:
        in_specs.append(pl.BlockSpec((tm, tr), lambda g, i, j, r: (i, g * rt + r)))
        in_specs.append(pl.BlockSpec((None, to, tr), lambda g, i, j, r: (g, j, r)))
        args += [a, b]
    in_specs += [pl.BlockSpec((tm, to), lambda g, i, j, r: (i, g * j_n + j)) for _ in extras]
    out_specs = [pl.BlockSpec((tm, to), lambda g, i, j, r: (i, g * j_n + j)) for _ in out_dtypes]
    out_shape = [jax.ShapeDtypeStruct((m_n, g_n * kd), dt) for dt in out_dtypes]
    scratch = [] if rt == 1 else [pltpu.VMEM((tm, to), F32)]
    return pl.pallas_call(
        body, name=name, grid=(g_n, i_n, j_n, rt), in_specs=in_specs, out_specs=out_specs, out_shape=out_shape,
        scratch_shapes=scratch, compiler_params=_params("parallel", "parallel", "parallel", "arbitrary"),
    )(*args, *extras)


def _mm_tn(name, a, b, g_n, out_shape, out_block, out_index, *, tki, tn, tm=2048, into=None):
    m_n = a.shape[0]
    k_n, n_n = a.shape[1] // g_n, b.shape[1] // g_n
    tm = _pick(m_n, tm, 8)
    i_n, j_n, mt = k_n // tki, n_n // tn, m_n // tm
    assert k_n % tki == 0 and n_n % tn == 0

    def body(*refs):
        a_ref, b_ref = refs[0], refs[1]
        o_ref, acc = refs[-2], refs[-1]
        m = pl.program_id(3)
        prod = lax.dot_general(a_ref[...], b_ref[...], TN_DIMS, preferred_element_type=F32)

        @pl.when(m == 0)
        def _():
            acc[...] = prod

        @pl.when(m > 0)
        def _():
            acc[...] += prod

        @pl.when(m == mt - 1)
        def _():
            o_ref[...] = acc[...].reshape(o_ref.shape)

    in_specs = [pl.BlockSpec((tm, tki), lambda g, i, j, m: (m, g * i_n + i)),
                pl.BlockSpec((tm, tn), lambda g, i, j, m: (m, g * j_n + j))]
    args = [a, b]
    aliases = {}
    if into is not None:
        in_specs.append(ANY)
        args.append(into)
        aliases = {2: 0}
    return pl.pallas_call(
        body, name=name, grid=(g_n, i_n, j_n, mt), in_specs=in_specs,
        out_specs=pl.BlockSpec(out_block, lambda g, i, j, m: out_index(g, i, j)),
        out_shape=jax.ShapeDtypeStruct(out_shape, F32), scratch_shapes=[pltpu.VMEM((tki, tn), F32)],
        input_output_aliases=aliases,
        compiler_params=_params("parallel", "parallel", "parallel", "arbitrary"),
    )(*args)


def _rstd(xv):
    return lax.rsqrt(jnp.mean(xv * xv, axis=-1, keepdims=True) + RMS_EPS)


def _rms_bwd_rows(dh, xv, gain, r):
    dy = dh * gain
    c = jnp.mean(dy * xv, axis=-1, keepdims=True)
    return r * dy - xv * (r * r * r) * c, dh * xv * r


def _fold8(t):
    return t.reshape(t.shape[0] // 8, 8, t.shape[1]).sum(axis=0)


def _rms_fwd(name, x, gain, tm=256):
    s_n, d_n = x.shape
    tm = _pick(s_n, tm, 8)

    def body(x_ref, g_ref, h_ref):
        xv = x_ref[...]
        h_ref[...] = (xv * _rstd(xv) * g_ref[...]).astype(h_ref.dtype)

    return pl.pallas_call(
        body, name=name, grid=(s_n // tm,),
        in_specs=[pl.BlockSpec((tm, d_n), lambda i: (i, 0)), pl.BlockSpec((1, d_n), lambda i: (0, 0))],
        out_specs=pl.BlockSpec((tm, d_n), lambda i: (i, 0)), out_shape=jax.ShapeDtypeStruct((s_n, d_n), BF16),
        compiler_params=_params("parallel"),
    )(x, gain)


def _rms_bwd(name, dh, x, gain, dres, tm=256):
    s_n, d_n = x.shape
    tm = _pick(s_n, tm, 8)
    nblk = s_n // tm

    def body(dh_ref, x_ref, g_ref, dres_ref, dx_ref, dg_ref, acc):
        i = pl.program_id(0)

        @pl.when(i == 0)
        def _():
            acc[...] = jnp.zeros_like(acc)

        xv = x_ref[...]
        dxv, dgt = _rms_bwd_rows(dh_ref[...].astype(F32), xv, g_ref[...], _rstd(xv))
        dx_ref[...] = dres_ref[...] + dxv
        acc[...] += _fold8(dgt)

        @pl.when(i == nblk - 1)
        def _():
            dg_ref[...] = jnp.sum(acc[...], axis=0, keepdims=True)

    row = pl.BlockSpec((tm, d_n), lambda i: (i, 0))
    vec = pl.BlockSpec((1, d_n), lambda i: (0, 0))
    return pl.pallas_call(
        body, name=name, grid=(nblk,), in_specs=[row, row, vec, row], out_specs=[row, vec],
        out_shape=[jax.ShapeDtypeStruct((s_n, d_n), F32), jax.ShapeDtypeStruct((1, d_n), F32)],
        scratch_shapes=[pltpu.VMEM((8, d_n), F32)], compiler_params=_params("arbitrary"),
    )(dh, x, gain, dres)


def _loss_bwd(name, x, target, gain, tm=256):
    s_n, d_n = x.shape
    tm = _pick(s_n, tm, 8)
    nblk = s_n // tm

    def body(x_ref, t_ref, g_ref, dx_ref, dg_ref, loss_ref, acc, lacc):
        i = pl.program_id(0)

        @pl.when(i == 0)
        def _():
            acc[...] = jnp.zeros_like(acc)
            lacc[...] = jnp.zeros_like(lacc)

        xv = x_ref[...]
        gain = g_ref[...]
        r = _rstd(xv)
        diff = xv * r * gain - t_ref[...]
        lacc[...] += _fold8(diff * diff)
        dxv, dgt = _rms_bwd_rows(diff * (1.0 / d_n), xv, gain, r)
        dx_ref[...] = dxv
        acc[...] += _fold8(dgt)

        @pl.when(i == nblk - 1)
        def _():
            dg_ref[...] = jnp.sum(acc[...], axis=0, keepdims=True)
            loss_ref[...] = jnp.sum(lacc[...], keepdims=True) * (0.5 / d_n)

    row = pl.BlockSpec((tm, d_n), lambda i: (i, 0))
    vec = pl.BlockSpec((1, d_n), lambda i: (0, 0))
    return pl.pallas_call(
        body, name=name, grid=(nblk,), in_specs=[row, row, vec],
        out_specs=[row, vec, pl.BlockSpec((1, 1), lambda i: (0, 0))],
        out_shape=[jax.ShapeDtypeStruct((s_n, d_n), F32), jax.ShapeDtypeStruct((1, d_n), F32),
                   jax.ShapeDtypeStruct((1, 1), F32)],
        scratch_shapes=[pltpu.VMEM((8, d_n), F32), pltpu.VMEM((8, d_n), F32)], compiler_params=_params("arbitrary"),
    )(x, target, gain)


def _pool_counts(t_idx, d_n):
    grp = d_n // len(POOL_WINDOWS)
    lane = lax.broadcasted_iota(jnp.int32, (1, d_n), 1) // grp
    win = jnp.zeros((1, d_n), jnp.int32)
    for gi, w in enumerate(POOL_WINDOWS):
        win = jnp.where(lane == gi, w, win)
    return jnp.minimum(t_idx + 1, win).astype(F32), lane


def _window_sums(rows, lane, backward):
    n = rows.shape[0]
    out = rows
    acc = rows
    width = 1
    for gi in range(len(POOL_WINDOWS)):
        shift = (n - width) if backward else width
        acc = acc + pltpu.roll(acc, shift, 0)
        width *= 2
        out = jnp.where(lane >= gi, acc, out)
    return out


def _pool_fwd(name, x, gain, tm=256):
    s_n, d_n = x.shape
    tm = _pick(s_n, tm, POOL_HALO)
    per = tm // POOL_HALO

    def body(x_ref, halo_ref, g_ref, o_ref):
        i = pl.program_id(0)
        halo = jnp.where(i == 0, 0.0, halo_ref[...])
        rows = jnp.concatenate([halo, x_ref[...]], axis=0)
        h = rows * _rstd(rows) * g_ref[...]
        t_idx = i * tm - POOL_HALO + lax.broadcasted_iota(jnp.int32, (tm + POOL_HALO, 1), 0)
        cnt, lane = _pool_counts(t_idx, d_n)
        pooled = _window_sums(h, lane, False) / cnt - h
        o_ref[...] = pooled[POOL_HALO:, :].astype(o_ref.dtype)

    return pl.pallas_call(
        body, name=name, grid=(s_n // tm,),
        in_specs=[pl.BlockSpec((tm, d_n), lambda i: (i, 0)),
                  pl.BlockSpec((POOL_HALO, d_n), lambda i: (jnp.maximum(i * per - 1, 0), 0)),
                  pl.BlockSpec((1, d_n), lambda i: (0, 0))],
        out_specs=pl.BlockSpec((tm, d_n), lambda i: (i, 0)), out_shape=jax.ShapeDtypeStruct((s_n, d_n), BF16),
        compiler_params=_params("parallel"),
    )(x, x, gain)


def _pool_bwd(name, dpooled, x, gain, dres, tm=256):
    s_n, d_n = x.shape
    tm = _pick(s_n, tm, POOL_HALO)
    per = tm // POOL_HALO
    nblk = s_n // tm
    last_halo = s_n // POOL_HALO - 1

    def body(dp_ref, halo_ref, x_ref, g_ref, dres_ref, dx_ref, dg_ref, acc):
        i = pl.program_id(0)

        @pl.when(i == 0)
        def _():
            acc[...] = jnp.zeros_like(acc)

        halo = jnp.where(i == nblk - 1, 0.0, halo_ref[...])
        rows = jnp.concatenate([dp_ref[...], halo], axis=0)
        t_idx = i * tm + lax.broadcasted_iota(jnp.int32, (tm + POOL_HALO, 1), 0)
        cnt, lane = _pool_counts(t_idx, d_n)
        dh = (_window_sums(rows / cnt, lane, True) - rows)[:tm, :]
        xv = x_ref[...]
        dxv, dgt = _rms_bwd_rows(dh, xv, g_ref[...], _rstd(xv))
        dx_ref[...] = dres_ref[...] + dxv
        acc[...] += _fold8(dgt)

        @pl.when(i == nblk - 1)
        def _():
            dg_ref[...] = jnp.sum(acc[...], axis=0, keepdims=True)

    row = pl.BlockSpec((tm, d_n), lambda i: (i, 0))
    vec = pl.BlockSpec((1, d_n), lambda i: (0, 0))
    return pl.pallas_call(
        body, name=name, grid=(nblk,),
        in_specs=[row, pl.BlockSpec((POOL_HALO, d_n), lambda i: (jnp.minimum((i + 1) * per, last_halo), 0)),
                  row, vec, row],
        out_specs=[row, vec],
        out_shape=[jax.ShapeDtypeStruct((s_n, d_n), F32), jax.ShapeDtypeStruct((1, d_n), F32)],
        scratch_shapes=[pltpu.VMEM((8, d_n), F32)], compiler_params=_params("arbitrary"),
    )(dpooled, dpooled, x, gain, dres)


def _scale_bwd(name, dx, mixed, scale, tm=256):
    s_n, d_n = dx.shape
    tm = _pick(s_n, tm, 8)
    nblk = s_n // tm

    def body(dx_ref, mx_ref, sc_ref, dm_ref, ds_ref, acc):
        i = pl.program_id(0)

        @pl.when(i == 0)
        def _():
            acc[...] = jnp.zeros_like(acc)

        dxv = dx_ref[...]
        dm_ref[...] = (dxv * sc_ref[...]).astype(dm_ref.dtype)
        acc[...] += _fold8(dxv * mx_ref[...])

        @pl.when(i == nblk - 1)
        def _():
            ds_ref[...] = jnp.sum(acc[...], axis=0, keepdims=True)

    row = pl.BlockSpec((tm, d_n), lambda i: (i, 0))
    vec = pl.BlockSpec((1, d_n), lambda i: (0, 0))
    return pl.pallas_call(
        body, name=name, grid=(nblk,), in_specs=[row, row, vec], out_specs=[row, vec],
        out_shape=[jax.ShapeDtypeStruct((s_n, d_n), BF16), jax.ShapeDtypeStruct((1, d_n), F32)],
        scratch_shapes=[pltpu.VMEM((8, d_n), F32)], compiler_params=_params("arbitrary"),
    )(dx, mixed, scale)


def _hg_gates(qa, fa, lb):
    sig = _sigmoid(fa)
    f = lb + (1.0 - lb) * sig
    sq = _sigmoid(qa)
    return sig, f, jnp.log(f), 1.0 - f, sq, qa * sq


def _hg_chunk_terms(q, k, g, lincl):
    gc = jnp.dot(lincl, g, precision=HIGHEST, preferred_element_type=F32)
    glast = gc[HG_CHUNK - 1:HG_CHUNK, :]
    gm = gc[HG_MID:HG_MID + 1, :]
    e_q, e_l = jnp.exp(gc), jnp.exp(glast - gc)
    e_m, e_mi = jnp.exp(gc - gm), jnp.exp(gm - gc)
    return glast, (e_q, e_l, e_m, e_mi), (q * e_q, k * e_l, q * e_m, k * e_mi)


def _hg_setup(s_n, n_heads):
    tb = _pick(s_n, HG_BLOCK, HG_CHUNK)
    par = HG_PAR if n_heads % HG_PAR == 0 else 1
    cols = [slice(p * HEAD, (p + 1) * HEAD) for p in range(par)]
    return tb, s_n // tb, tb // HG_CHUNK, par, cols


def _hg_fwd(name, proj, lb_logits, hgain, n_heads):
    s_n = proj.shape[0]
    h_n = n_heads
    tb, nblk, ncb, par, cols = _hg_setup(s_n, h_n)
    c_n = HG_CHUNK
    heads = range(par)

    def body(qa_ref, fa_ref, ia_ref, ga_ref, l_ref, gn_ref, oraw_ref, oa_ref, st_ref, state):
        @pl.when(pl.program_id(1) == 0)
        def _():
            state[...] = jnp.zeros_like(state)

        lv = l_ref[...]
        lbs = [_sigmoid(lv[0:1, c] - lv[1:2, c]) for c in cols]
        row = lax.broadcasted_iota(jnp.int32, (c_n, c_n), 0)
        col = lax.broadcasted_iota(jnp.int32, (c_n, c_n), 1)
        causal = col <= row
        lincl = causal.astype(F32)
        gn = gn_ref[...]

        def chunk(ci, carry):
            sl = pl.ds(pl.multiple_of(ci * c_n, c_n), c_n)
            gates = [_hg_gates(qa_ref[sl, cols[p]], fa_ref[sl, cols[p]], lbs[p]) for p in heads]
            terms = [_hg_chunk_terms(gates[p][5], gates[p][3], gates[p][2], lincl) for p in heads]
            vbs = [ia_ref[sl, cols[p]].astype(BF16) for p in heads]
            sts = [state[p] for p in heads]
            atts = [lax.dot_general(terms[p][2][2].astype(BF16), terms[p][2][3].astype(BF16), NT_DIMS,
                                    preferred_element_type=F32) for p in heads]
            inter = [lax.dot_general(terms[p][2][0].astype(BF16), sts[p].astype(BF16), NT_DIMS,
                                     preferred_element_type=F32) for p in heads]
            grown = [lax.dot_general(vbs[p], terms[p][2][1].astype(BF16), TN_DIMS, preferred_element_type=F32)
                     for p in heads]
            attb = [jnp.where(causal, atts[p], 0.0).astype(BF16) for p in heads]
            outs = [inter[p] + jnp.dot(attb[p], vbs[p], preferred_element_type=F32) for p in heads]
            for p in heads:
                st_ref[p, ci] = sts[p]
                state[p] = sts[p] * jnp.exp(terms[p][0]) + grown[p]
                o = outs[p]
                oraw_ref[sl, cols[p]] = o
                ga = ga_ref[sl, cols[p]]
                oa_ref[sl, cols[p]] = (o * _rstd(o) * gn * (ga * _sigmoid(ga))).astype(oa_ref.dtype)
            return carry

        lax.fori_loop(0, ncb, chunk, 0)

    wide = par * HEAD
    blk = lambda off: pl.BlockSpec((tb, wide), lambda h, c: (c, off // par + h))
    return pl.pallas_call(
        body, name=name, grid=(h_n // par, nblk),
        in_specs=[blk(0), blk(h_n), blk(2 * h_n), blk(3 * h_n),
                  pl.BlockSpec((2, wide), lambda h, c: (0, h)), pl.BlockSpec((1, HEAD), lambda h, c: (0, 0))],
        out_specs=[blk(0), blk(0), pl.BlockSpec((par, ncb, HEAD, HEAD), lambda h, c: (h, c, 0, 0))],
        out_shape=[jax.ShapeDtypeStruct((s_n, h_n * HEAD), F32), jax.ShapeDtypeStruct((s_n, h_n * HEAD), BF16),
                   jax.ShapeDtypeStruct((h_n, s_n // c_n, HEAD, HEAD), F32)],
        scratch_shapes=[pltpu.VMEM((par, HEAD, HEAD), F32)], compiler_params=_params("parallel", "arbitrary"),
    )(proj, proj, proj, proj, lb_logits, hgain)


def _hg_bwd(name, proj, dcat, oraw, states, lb_logits, hgain, n_heads):
    s_n = proj.shape[0]
    h_n = n_heads
    tb, nblk, ncb, par, cols = _hg_setup(s_n, h_n)
    c_n = HG_CHUNK
    n_steps = h_n // par

    def body(qa_ref, fa_ref, ia_ref, ga_ref, doa_ref, oraw_ref, st_ref, l_ref, gn_ref,
             dqa_ref, dfa_ref, dia_ref, dga_ref, dl_ref, dgn_ref, dstate, dlb_acc, dgn_acc):
        h, c = pl.program_id(0), pl.program_id(1)

        @pl.when(c == 0)
        def _():
            dstate[...] = jnp.zeros_like(dstate)
            dlb_acc[...] = jnp.zeros_like(dlb_acc)

        @pl.when((c == 0) & (h == 0))
        def _():
            dgn_acc[...] = jnp.zeros_like(dgn_acc)

        lv = l_ref[...]
        lbs = [_sigmoid(lv[0:1, cc] - lv[1:2, cc]) for cc in cols]
        row = lax.broadcasted_iota(jnp.int32, (c_n, c_n), 0)
        col = lax.broadcasted_iota(jnp.int32, (c_n, c_n), 1)
        causal = col <= row
        lincl = causal.astype(F32)
        uincl = (col >= row).astype(F32)
        is_last = lax.broadcasted_iota(jnp.int32, (c_n, 1), 0) == c_n - 1
        gn = gn_ref[...]

        def head_chunk(p, sl, ci):
            cc, lb = cols[p], lbs[p]
            qa = qa_ref[sl, cc]
            sig, f, g, k, sq, q = _hg_gates(qa, fa_ref[sl, cc], lb)
            glast, (e_q, e_l, e_m, e_mi), (qe, kl, qm, km) = _hg_chunk_terms(q, k, g, lincl)
            yield
            v = ia_ref[sl, cc]
            vb = v.astype(BF16)
            qmb, kmb, qeb, klb = qm.astype(BF16), km.astype(BF16), qe.astype(BF16), kl.astype(BF16)
            att = lax.dot_general(qmb, kmb, NT_DIMS, preferred_element_type=F32)

            o = oraw_ref[sl, cc]
            ga = ga_ref[sl, cc]
            sg = _sigmoid(ga)
            r = _rstd(o)
            doa = doa_ref[sl, cc]
            dn = doa * (ga * sg)
            dga_ref[sl, cc] = (doa * (o * r * gn) * _dsilu(ga, sg)).astype(dga_ref.dtype)
            yield
            attb = jnp.where(causal, att, 0.0).astype(BF16)
            do, dgt = _rms_bwd_rows(dn, o, gn, r)
            dgn_acc[...] += dgt
            dob = do.astype(BF16)

            st0 = st_ref[p, ci]
            ds1 = dstate[p]
            st0b, ds1b = st0.astype(BF16), ds1.astype(BF16)
            datt = lax.dot_general(dob, vb, NT_DIMS, preferred_element_type=F32)
            dv = lax.dot_general(attb, dob, TN_DIMS, preferred_element_type=F32)
            dv = dv + lax.dot_general(klb, ds1b, NT_DIMS, preferred_element_type=F32)
            dqe = jnp.dot(dob, st0b, preferred_element_type=F32)
            dkl = jnp.dot(vb, ds1b, preferred_element_type=F32)
            eg = jnp.exp(glast)
            dstate[p] = ds1 * eg + lax.dot_general(dob, qeb, TN_DIMS, preferred_element_type=F32)
            yield
            dattb = jnp.where(causal, datt, 0.0).astype(BF16)
            dqm = jnp.dot(dattb, kmb, preferred_element_type=F32)
            dkm = lax.dot_general(dattb, qmb, TN_DIMS, preferred_element_type=F32)
            dia_ref[sl, cc] = dv.astype(dia_ref.dtype)
            yield
            dq = dqm * e_m + dqe * e_q
            dk = dkm * e_mi + dkl * e_l
            dgc = dqm * qmb.astype(F32) - dkm * kmb.astype(F32) + dqe * qe - dkl * kl
            dglast = jnp.sum(dkl * kl, axis=0, keepdims=True) + eg * jnp.sum(ds1 * st0, axis=0, keepdims=True)
            dgc = dgc + jnp.where(is_last, dglast, 0.0)
            dg = jnp.dot(uincl, dgc, precision=HIGHEST, preferred_element_type=F32)
            dqa_ref[sl, cc] = (dq * _dsilu(qa, sq)).astype(dqa_ref.dtype)
            yield
            df = dg / f - dk
            dfa_ref[sl, cc] = (df * (1.0 - lb) * sig * (1.0 - sig)).astype(dfa_ref.dtype)
            dlb_acc[p] += df * (1.0 - sig)

        def chunk(idx, carry):
            ci = ncb - 1 - idx
            sl = pl.ds(pl.multiple_of(ci * c_n, c_n), c_n)
            running = [head_chunk(p, sl, ci) for p in range(par)]
            while running:
                running = [g for g in running if next(g, True) is None]
            return carry

        lax.fori_loop(0, ncb, chunk, 0)

        @pl.when(c == nblk - 1)
        def _():
            first = lax.broadcasted_iota(jnp.int32, (2, HEAD), 0) == 0
            for p, cc in enumerate(cols):
                dl0 = jnp.sum(dlb_acc[p], axis=0, keepdims=True) * lbs[p] * (1.0 - lbs[p])
                dl_ref[:, cc] = jnp.where(first, dl0, -dl0)

        @pl.when((c == nblk - 1) & (h == n_steps - 1))
        def _():
            dgn_ref[...] = jnp.sum(dgn_acc[...], axis=0, keepdims=True)

    wide = par * HEAD
    blk = lambda off: pl.BlockSpec((tb, wide), lambda h, c: (nblk - 1 - c, off // par + h))
    out_act = jax.ShapeDtypeStruct((s_n, h_n * HEAD), BF16)
    return pl.pallas_call(
        body, name=name, grid=(n_steps, nblk),
        in_specs=[blk(0), blk(h_n), blk(2 * h_n), blk(3 * h_n), blk(0), blk(0),
                  pl.BlockSpec((par, ncb, HEAD, HEAD), lambda h, c: (h, nblk - 1 - c, 0, 0)),
                  pl.BlockSpec((2, wide), lambda h, c: (0, h)), pl.BlockSpec((1, HEAD), lambda h, c: (0, 0))],
        out_specs=[blk(0), blk(0), blk(0), blk(0), pl.BlockSpec((2, wide), lambda h, c: (0, h)),
                   pl.BlockSpec((1, HEAD), lambda h, c: (0, 0))],
        out_shape=[out_act, out_act, out_act, out_act, jax.ShapeDtypeStruct((2, h_n * HEAD), F32),
                   jax.ShapeDtypeStruct((1, HEAD), F32)],
        scratch_shapes=[pltpu.VMEM((par, HEAD, HEAD), F32), pltpu.VMEM((par, c_n, HEAD), F32),
                        pltpu.VMEM((c_n, HEAD), F32)],
        compiler_params=_params("arbitrary", "arbitrary"),
    )(proj, proj, proj, proj, dcat, oraw, states, lb_logits, hgain)


def _split_dot(t, ones_b):
    hi = t.astype(BF16)
    lo = (t - hi.astype(F32)).astype(BF16)
    both = jnp.dot(jnp.concatenate([hi, lo], axis=0), ones_b, preferred_element_type=F32)
    return both[:t.shape[0]] + both[t.shape[0]:]


def _softplus(z):
    return jnp.maximum(z, 0.0) + jnp.log(1.0 + jnp.exp2(jnp.abs(z) * (-LOG2_E)))


def _att_setup(projb, n_heads):
    s_n = projb.shape[0]
    t_n = _pick(s_n, ATT_BLOCK, 8)
    par = ATT_PAR if n_heads % ATT_PAR == 0 else 1
    cols = [slice(p * HEAD, (p + 1) * HEAD) for p in range(par)]
    wide = par * HEAD
    full = lambda off: pl.BlockSpec((s_n, wide), lambda h, i: (0, off // par + h))
    tile = lambda off: pl.BlockSpec((t_n, wide), lambda h, i: (i, off // par + h))
    return s_n, t_n, par, cols, full, tile


def _att_fwd(name, projb, n_heads):
    h_n = n_heads
    s_n, t_n, par, cols, full, tile_spec = _att_setup(projb, h_n)
    scale = 1.0 / math.sqrt(HEAD)

    def body(q_ref, k_ref, v_ref, o_ref, lt_ref):
        i = pl.program_id(1)
        row = lax.broadcasted_iota(jnp.int32, (t_n, t_n), 0)
        col = lax.broadcasted_iota(jnp.int32, (t_n, t_n), 1)
        from_here = (row >= col).astype(BF16)
        tri = col < row
        qs = [q_ref[:, c] for c in cols]

        def tile(j, carry, diagonal):
            sl = pl.ds(pl.multiple_of(j * t_n, t_n), t_n)
            zs = [lax.dot_general(qs[p], k_ref[sl, c], NT_DIMS, preferred_element_type=F32)
                  for p, c in enumerate(cols)]
            mid = []
            for p in range(par):
                z = zs[p] * scale
                sp = _softplus(z)
                if diagonal:
                    sp = jnp.where(tri, sp, 0.0)
                mid.append((z - carry[p][1], _split_dot(sp, from_here)))
            out = []
            for p, c in enumerate(cols):
                zr, spent = mid[p]
                w = jnp.exp(zr - spent)
                if diagonal:
                    w = jnp.where(tri, w, 0.0)
                acc = carry[p][0] + jnp.dot(w.astype(BF16), v_ref[sl, c], preferred_element_type=F32)
                out.append((acc, carry[p][1] + spent[:, 0:1]))
            return tuple(out)

        init = tuple((jnp.zeros((t_n, HEAD), F32), jnp.zeros((t_n, 1), F32)) for _ in cols)
        carry = tile(i, init, True)
        carry = lax.fori_loop(0, i, lambda jj, cr: tile(i - 1 - jj, cr, False), carry)
        for p, c in enumerate(cols):
            o_ref[:, c] = carry[p][0].astype(o_ref.dtype)
            lt_ref[:, c] = jnp.broadcast_to(carry[p][1], (t_n, HEAD))

    return pl.pallas_call(
        body, name=name, grid=(h_n // par, s_n // t_n),
        in_specs=[tile_spec(4 * h_n), full(5 * h_n), full(6 * h_n)], out_specs=[tile_spec(0), tile_spec(0)],
        out_shape=[jax.ShapeDtypeStruct((s_n, h_n * HEAD), BF16), jax.ShapeDtypeStruct((s_n, h_n * HEAD), F32)],
        compiler_params=_params("parallel", "arbitrary"),
    )(projb, projb, projb)


def _att_bwd(name, projb, dcat, spent_all, n_heads):
    h_n = n_heads
    s_n, t_n, par, cols, full, tile_spec = _att_setup(projb, h_n)
    scale = 1.0 / math.sqrt(HEAD)

    def body(q_ref, k_ref, v_ref, do_ref, lt_ref, dq_ref, dk_ref, dv_ref):
        i = pl.program_id(1)

        @pl.when(i == 0)
        def _():
            dk_ref[...] = jnp.zeros_like(dk_ref)
            dv_ref[...] = jnp.zeros_like(dv_ref)

        row = lax.broadcasted_iota(jnp.int32, (t_n, t_n), 0)
        col = lax.broadcasted_iota(jnp.int32, (t_n, t_n), 1)
        before = (row < col).astype(BF16)
        upto = (row <= col).astype(BF16)
        tri = col < row
        qs = [q_ref[:, c] for c in cols]
        dos = [do_ref[:, c].astype(BF16) for c in cols]
        last = slice(t_n - 1, t_n)

        def tile(j, carry, diagonal):
            sl = pl.ds(pl.multiple_of(j * t_n, t_n), t_n)
            zs = [lax.dot_general(qs[p], k_ref[sl, c], NT_DIMS, preferred_element_type=F32)
                  for p, c in enumerate(cols)]
            dws = [lax.dot_general(dos[p], v_ref[sl, c], NT_DIMS, preferred_element_type=F32)
                   for p, c in enumerate(cols)]
            mid1 = []
            for p in range(par):
                z = zs[p] * scale
                sp = _softplus(z)
                sg = jnp.exp(z - sp)
                if diagonal:
                    sp = jnp.where(tri, sp, 0.0)
                prior = _split_dot(sp, before)
                mid1.append((z - carry[p][1], sg, prior, prior[:, last] + sp[:, last]))
            mid2 = []
            for p in range(par):
                zb, sg, prior, sp_sum = mid1[p]
                w = jnp.exp(zb + prior)
                if diagonal:
                    w = jnp.where(tri, w, 0.0)
                e = dws[p] * w
                mid2.append((w.astype(BF16), e, sg, _split_dot(e, upto), sp_sum))
            out = []
            for p, c in enumerate(cols):
                wb, e, sg, e_upto, sp_sum = mid2[p]
                dz = (e - sg * (carry[p][2] + e_upto)) * scale
                if diagonal:
                    dz = jnp.where(tri, dz, 0.0)
                dz = dz.astype(BF16)
                dq = carry[p][0] + jnp.dot(dz, k_ref[sl, c], preferred_element_type=F32)
                dk_ref[sl, c] += lax.dot_general(dz, qs[p], TN_DIMS, preferred_element_type=F32)
                dv_ref[sl, c] += lax.dot_general(wb, dos[p], TN_DIMS, preferred_element_type=F32)
                out.append((dq, carry[p][1] - sp_sum, carry[p][2] + e_upto[:, last]))
            return tuple(out)

        init = tuple((jnp.zeros((t_n, HEAD), F32), lt_ref[:, c][:, 0:1], jnp.zeros((t_n, 1), F32)) for c in cols)
        carry = lax.fori_loop(0, i, lambda j, cr: tile(j, cr, False), init)
        carry = tile(i, carry, True)
        for p, c in enumerate(cols):
            dq_ref[:, c] = carry[p][0].astype(dq_ref.dtype)

    return pl.pallas_call(
        body, name=name, grid=(h_n // par, s_n // t_n),
        in_specs=[tile_spec(4 * h_n), full(5 * h_n), full(6 * h_n), tile_spec(h_n), tile_spec(0)],
        out_specs=[tile_spec(0), full(0), full(0)],
        out_shape=[jax.ShapeDtypeStruct((s_n, h_n * HEAD), BF16), jax.ShapeDtypeStruct((s_n, h_n * HEAD), F32),
                   jax.ShapeDtypeStruct((s_n, h_n * HEAD), F32)],
        compiler_params=_params("arbitrary", "arbitrary"),
    )(projb, projb, projb, dcat, spent_all)


def _adamw(name, w, g, m, v, tr=256):
    r_n, c_n = w.shape
    tr = _pick(r_n, tr, 8)
    c1 = 1.0 - ADAM_B1 ** ADAM_STEP
    c2 = 1.0 - ADAM_B2 ** ADAM_STEP

    def body(w_ref, g_ref, m_ref, v_ref, d_ref, nm_ref, nv_ref):
        gv = g_ref[...]
        nm = ADAM_B1 * m_ref[...] + (1.0 - ADAM_B1) * gv
        nv = ADAM_B2 * v_ref[...] + (1.0 - ADAM_B2) * (gv * gv)
        d_ref[...] = -ADAM_LR * ((nm / c1) / (jnp.sqrt(nv / c2) + ADAM_EPS) + ADAM_WD * w_ref[...])
        nm_ref[...] = nm
        nv_ref[...] = nv

    blk = pl.BlockSpec((tr, c_n), lambda i: (i, 0))
    sds = jax.ShapeDtypeStruct((r_n, c_n), F32)
    return pl.pallas_call(
        body, name=name, grid=(r_n // tr,), in_specs=[blk] * 4, out_specs=[blk] * 3, out_shape=[sds] * 3,
        compiler_params=_params("parallel"),
    )(w, g, m, v)


def _adamw_nd(name, w, g, m, v):
    shape = w.shape
    flat = lambda t: t.reshape(-1, shape[-1])
    return tuple(t.reshape(shape) for t in _adamw(name, flat(w), flat(g.reshape(shape)), flat(m), flat(v)))


def _mesh_pos():
    x, y, c = lax.axis_index("x"), lax.axis_index("y"), lax.axis_index("c")
    chips = [(1 - x, y), (x, 1 - y), (1 - x, 1 - y)]
    return x, y, c, chips, 2 * x + y, [2 * cx + cy for cx, cy in chips]


class _Unit:
    def __init__(self, shard_shape, axis, half_axis):
        self.shard_shape = tuple(shard_shape)
        self.axis = axis
        self.half_axis = half_axis
        self.full_shape = tuple(n * N_CHIPS if a == axis else n for a, n in enumerate(shard_shape))
        self.half_shape = tuple(n // 2 if a == half_axis else n for a, n in enumerate(shard_shape))

    def _window(self, ref, k, c, with_slab):
        idx = []
        for a, n in enumerate(self.shard_shape):
            start, size = 0, n
            if a == self.half_axis:
                size = n // 2
                start = c * size
            if with_slab and a == self.axis:
                start = start + k * n
            idx.append(pl.ds(start, size))
        return ref.at[tuple(idx)]

    def full_half(self, ref, k, c):
        return self._window(ref, k, c, True)

    def place_view(self):
        s = self.shard_shape
        if self.axis == len(s) - 1:
            return math.prod(s[:-2]), s[-2], s[-1], True
        assert self.axis == len(s) - 2
        return math.prod(s[:self.axis]), s[self.axis], s[-1], False

    def half_view(self):
        s, h = self.shard_shape, self.half_axis
        if h == len(s) - 1:
            return math.prod(s[:-2]), s[-2], s[-1] // 2, True
        return math.prod(s[:h]), (s[h] // 2) * math.prod(s[h + 1:-1]), s[-1], False


def _place_shard(name, shard, unit, chip_idx):
    l_n, r_n, c_n, by_cols = unit.place_view()
    tr = _pick(r_n, 256, 16)
    per = r_n // tr

    def body(k_ref, s_ref, o_ref):
        o_ref[...] = s_ref[...].astype(o_ref.dtype)

    if by_cols:
        full3, out_index = (l_n, r_n, N_CHIPS * c_n), (lambda l, i, k_ref: (l, i, k_ref[0]))
    else:
        full3, out_index = (l_n, N_CHIPS * r_n, c_n), (lambda l, i, k_ref: (l, k_ref[0] * per + i, 0))
    out = pl.pallas_call(
        body, name=name,
        grid_spec=pltpu.PrefetchScalarGridSpec(
            num_scalar_prefetch=1, grid=(l_n, per),
            in_specs=[pl.BlockSpec((None, tr, c_n), lambda l, i, k_ref: (l, i, 0))],
            out_specs=pl.BlockSpec((None, tr, c_n), out_index)),
        out_shape=jax.ShapeDtypeStruct(full3, BF16), compiler_params=_params("parallel", "parallel"),
    )(chip_idx, shard.reshape(l_n, r_n, c_n))
    return out.reshape(unit.full_shape)


def _gather_weights(units, fulls, scale_shard):
    nu = len(units)
    ps = scale_shard.shape[1]

    def body(*refs):
        sc_in = refs[nu]
        outs, sc_out = refs[nu + 1:2 * nu + 1], refs[2 * nu + 1]
        send1, recv1, send2, recv2, send3, recv3, lsem = refs[2 * nu + 2:]
        x, y, c, chips, me, others = _mesh_pos()
        local = [pltpu.make_async_copy(sc_in, sc_out.at[:, pl.ds(me * ps, ps)], lsem.at[0])]
        for cp in local:
            cp.start()
        sends = []
        for u in range(nu):
            for j, chip in enumerate(chips):
                mine = units[u].full_half(outs[u], me, c)
                sends.append(pltpu.make_async_remote_copy(
                    src_ref=mine, dst_ref=mine, send_sem=send1.at[3 * u + j], recv_sem=recv1.at[3 * u + j],
                    device_id=(*chip, c), device_id_type=MESH))
        for j, chip in enumerate(chips):
            sends.append(pltpu.make_async_remote_copy(
                src_ref=sc_in, dst_ref=sc_out.at[:, pl.ds(me * ps, ps)], send_sem=send3.at[j], recv_sem=recv3.at[j],
                device_id=(*chip, c), device_id_type=MESH))
        for cp in sends:
            cp.start()
        for u in range(nu):
            for j in range(3):
                landed = units[u].full_half(outs[u], others[j], c)
                pltpu.make_async_remote_copy(
                    src_ref=landed, dst_ref=landed, send_sem=send1.at[3 * u + j], recv_sem=recv1.at[3 * u + j],
                    device_id=(x, y, c), device_id_type=MESH).wait_recv()
                fwd = pltpu.make_async_remote_copy(
                    src_ref=landed, dst_ref=landed, send_sem=send2.at[3 * u + j], recv_sem=recv2.at[3 * u + j],
                    device_id=(x, y, 1 - c), device_id_type=MESH)
                fwd.start()
                sends.append(fwd)
        for u in range(nu):
            for j in range(3):
                theirs = units[u].full_half(outs[u], others[j], 1 - c)
                pltpu.make_async_remote_copy(
                    src_ref=theirs, dst_ref=theirs, send_sem=send2.at[3 * u + j], recv_sem=recv2.at[3 * u + j],
                    device_id=(x, y, c), device_id_type=MESH).wait_recv()
        for j in range(3):
            dst = sc_out.at[:, pl.ds(others[j] * ps, ps)]
            pltpu.make_async_remote_copy(src_ref=dst, dst_ref=dst, send_sem=send3.at[j], recv_sem=recv3.at[j],
                                         device_id=(x, y, c), device_id_type=MESH).wait_recv()
        for cp in sends:
            cp.wait_send()
        for cp in local:
            cp.wait()

    out_shape = [jax.ShapeDtypeStruct(f.shape, f.dtype) for f in fulls]
    out_shape.append(jax.ShapeDtypeStruct((1, N_CHIPS * ps), scale_shard.dtype))
    dma = pltpu.SemaphoreType.DMA
    return pl.pallas_call(
        body, name="gather_weights", in_specs=[ANY] * (nu + 1), out_specs=[ANY] * (nu + 1), out_shape=out_shape,
        input_output_aliases={u: u for u in range(nu)},
        scratch_shapes=[dma((3 * nu,)), dma((3 * nu,)), dma((3 * nu,)), dma((3 * nu,)), dma((3,)), dma((3,)),
                        dma((1,))],
    )(*fulls, scale_shard)


def _to_sibling(grads):
    nu = len(grads)

    def body(*refs):
        ins, outs = refs[:nu], refs[nu:2 * nu]
        send, recv = refs[2 * nu:]
        x, y, c, _, _, _ = _mesh_pos()
        cps = [pltpu.make_async_remote_copy(
            src_ref=ins[u].at[:, 1 - c], dst_ref=outs[u], send_sem=send.at[u], recv_sem=recv.at[u],
            device_id=(x, y, 1 - c), device_id_type=MESH) for u in range(nu)]
        for cp in cps:
            cp.start()
        for cp in cps:
            cp.wait()

    out_shape = [jax.ShapeDtypeStruct((g.shape[0],) + g.shape[2:], g.dtype) for g in grads]
    dma = pltpu.SemaphoreType.DMA
    return pl.pallas_call(
        body, name="grads_to_sibling", in_specs=[ANY] * nu, out_specs=[ANY] * nu, out_shape=out_shape,
        scratch_shapes=[dma((nu,)), dma((nu,))],
    )(*grads)


def _to_owners(partials):
    nu = len(partials)

    def body(*refs):
        ins, outs = refs[:nu], refs[nu:2 * nu]
        send, recv = refs[2 * nu:]
        x, y, c, chips, me, others = _mesh_pos()
        cps = [pltpu.make_async_remote_copy(
            src_ref=ins[u].at[others[j]], dst_ref=outs[u].at[j], send_sem=send.at[3 * u + j],
            recv_sem=recv.at[3 * u + j], device_id=(*chips[j], c), device_id_type=MESH)
            for u in range(nu) for j in range(3)]
        for cp in cps:
            cp.start()
        for cp in cps:
            cp.wait()

    out_shape = [jax.ShapeDtypeStruct((3,) + p.shape[1:], p.dtype) for p in partials]
    dma = pltpu.SemaphoreType.DMA
    return pl.pallas_call(
        body, name="partials_to_owners", in_specs=[ANY] * nu, out_specs=[ANY] * nu, out_shape=out_shape,
        scratch_shapes=[dma((3 * nu,)), dma((3 * nu,))],
    )(*partials)


def _share_halves(halves):
    nu = len(halves)

    def body(*refs):
        ins, outs = refs[:nu], refs[nu:2 * nu]
        send, recv = refs[2 * nu:]
        x, y, c, _, _, _ = _mesh_pos()
        cps = [pltpu.make_async_remote_copy(
            src_ref=ins[u], dst_ref=outs[u], send_sem=send.at[u], recv_sem=recv.at[u],
            device_id=(x, y, 1 - c), device_id_type=MESH) for u in range(nu)]
        for cp in cps:
            cp.start()
        for cp in cps:
            cp.wait()

    out_shape = [jax.ShapeDtypeStruct(h.shape, h.dtype) for h in halves]
    dma = pltpu.SemaphoreType.DMA
    return pl.pallas_call(
        body, name="share_halves", in_specs=[ANY] * nu, out_specs=[ANY] * nu, out_shape=out_shape,
        scratch_shapes=[dma((nu,)), dma((nu,))],
    )(*halves)


def _add_mine(name, grad, recv, c_idx):
    _, _, r_n, c_n = grad.shape
    tr = _pick(r_n, 256, 16)

    def body(c_ref, g_ref, r_ref, o_ref):
        o_ref[...] = (g_ref[...] + r_ref[...]).astype(o_ref.dtype)

    return pl.pallas_call(
        body, name=name,
        grid_spec=pltpu.PrefetchScalarGridSpec(
            num_scalar_prefetch=1, grid=(N_CHIPS, r_n // tr),
            in_specs=[pl.BlockSpec((None, None, tr, c_n), lambda k, i, c_ref: (k, c_ref[0], i, 0)),
                      pl.BlockSpec((None, tr, c_n), lambda k, i, c_ref: (k, i, 0))],
            out_specs=pl.BlockSpec((None, tr, c_n), lambda k, i, c_ref: (k, i, 0))),
        out_shape=jax.ShapeDtypeStruct(recv.shape, BF16), compiler_params=_params("parallel", "parallel"),
    )(c_idx, grad, recv)


def _add_slots(name, partial, slots, chip_idx):
    _, r_n, c_n = slots.shape
    tr = _pick(r_n, 256, 16)

    def body(k_ref, p_ref, s_ref, o_ref):
        own = p_ref[...].astype(F32)
        o_ref[...] = ((own + s_ref[0].astype(F32)) + s_ref[1].astype(F32)) + s_ref[2].astype(F32)

    return pl.pallas_call(
        body, name=name,
        grid_spec=pltpu.PrefetchScalarGridSpec(
            num_scalar_prefetch=1, grid=(r_n // tr,),
            in_specs=[pl.BlockSpec((None, tr, c_n), lambda i, k_ref: (k_ref[0], i, 0)),
                      pl.BlockSpec((3, tr, c_n), lambda i, k_ref: (0, i, 0))],
            out_specs=pl.BlockSpec((tr, c_n), lambda i, k_ref: (i, 0))),
        out_shape=jax.ShapeDtypeStruct((r_n, c_n), F32), compiler_params=_params("parallel"),
    )(chip_idx, partial, slots)


def _adamw_halves(name, unit, w, m, v, mine, theirs, c_idx, tr=256):
    l_n, r_n, c_n, by_cols = unit.half_view()
    tr = _pick(r_n, tr, 8)
    c1 = 1.0 - ADAM_B1 ** ADAM_STEP
    c2 = 1.0 - ADAM_B2 ** ADAM_STEP

    def body(c_ref, w_ref, m_ref, v_ref, mine_ref, theirs_ref, g_ref, d_ref, nm_ref, nv_ref):
        gv = jnp.where(pl.program_id(1) == c_ref[0], mine_ref[...], theirs_ref[...])
        nm = ADAM_B1 * m_ref[...] + (1.0 - ADAM_B1) * gv
        nv = ADAM_B2 * v_ref[...] + (1.0 - ADAM_B2) * (gv * gv)
        d_ref[...] = -ADAM_LR * ((nm / c1) / (jnp.sqrt(nv / c2) + ADAM_EPS) + ADAM_WD * w_ref[...])
        g_ref[...] = gv
        nm_ref[...] = nm
        nv_ref[...] = nv

    if by_cols:
        view = (l_n, r_n, 2 * c_n)
        whole = pl.BlockSpec((None, tr, c_n), lambda l, h, i, c_ref: (l, i, h))
    else:
        view = (l_n, 2, r_n, c_n)
        whole = pl.BlockSpec((None, None, tr, c_n), lambda l, h, i, c_ref: (l, h, i, 0))
    half = pl.BlockSpec((None, tr, c_n), lambda l, h, i, c_ref: (l, i, 0))
    sds = jax.ShapeDtypeStruct(view, F32)
    outs = pl.pallas_call(
        body, name=name,
        grid_spec=pltpu.PrefetchScalarGridSpec(
            num_scalar_prefetch=1, grid=(l_n, 2, r_n // tr),
            in_specs=[whole, whole, whole, half, half], out_specs=[whole] * 4),
        out_shape=[sds] * 4, compiler_params=_params("parallel", "parallel", "parallel"),
    )(c_idx, w.reshape(view), m.reshape(view), v.reshape(view),
      mine.reshape(l_n, r_n, c_n), theirs.reshape(l_n, r_n, c_n))
    return tuple(t.reshape(w.shape) for t in outs)


def _allreduce_small(block):
    r_n, c_n = block.shape

    def body(in_ref, out_ref, slots, send, recv):
        x, y, c = lax.axis_index("x"), lax.axis_index("y"), lax.axis_index("c")
        me = 4 * x + 2 * y + c
        slots[me] = in_ref[...]
        flips = [(fx, fy, fc) for fx in (0, 1) for fy in (0, 1) for fc in (0, 1)][1:]
        peers = [(x ^ fx, y ^ fy, c ^ fc) for fx, fy, fc in flips]
        cps = [pltpu.make_async_remote_copy(src_ref=in_ref, dst_ref=slots.at[me], send_sem=send.at[j],
                                            recv_sem=recv.at[j], device_id=peers[j], device_id_type=MESH)
               for j in range(7)]
        for cp in cps:
            cp.start()
        for j, (px, py, pc) in enumerate(peers):
            slot = slots.at[4 * px + 2 * py + pc]
            pltpu.make_async_remote_copy(src_ref=slot, dst_ref=slot, send_sem=send.at[j], recv_sem=recv.at[j],
                                         device_id=(x, y, c), device_id_type=MESH).wait_recv()
        for cp in cps:
            cp.wait_send()
        total = slots[0]
        for d in range(1, 8):
            total = total + slots[d]
        out_ref[...] = total

    vmem = pl.BlockSpec(memory_space=pltpu.VMEM)
    return pl.pallas_call(
        body, name="allreduce_small", in_specs=[vmem], out_specs=vmem,
        out_shape=jax.ShapeDtypeStruct((r_n, c_n), F32),
        scratch_shapes=[pltpu.VMEM((8, r_n, c_n), F32), pltpu.SemaphoreType.DMA((7,)), pltpu.SemaphoreType.DMA((7,))],
    )(block)


def _first(accs, extras):
    return [accs[0]] if isinstance(accs, list) else [accs]


def _ffn_fwd(tag, x_in, h, wg, wu, wd):
    def act(accs, extras):
        a, b = accs
        return [a, b, a * _sigmoid(a) * b]

    a, b, s = _mm_nn(f"ffn_up_{tag}", h, [wg, wu], [], act, [BF16, BF16, BF16])
    x_out, = _mm_nn(f"ffn_down_{tag}", s, [wd], [x_in], lambda accs, ex: [ex[0] + accs[0]], [F32], tk=8192)
    return x_out, a, b, s


def _ffn_bwd(tag, layer, dx_out, h, a, b, s, wg, wu, wd, into):
    def mid(acc, extras):
        av, bv = extras[0].astype(F32), extras[1].astype(F32)
        sg = _sigmoid(av)
        return [acc * bv * _dsilu(av, sg), acc * (av * sg)]

    dxb = dx_out.astype(BF16)
    da, db = _mm_nt(f"ffn_dact_{tag}", [(dxb, wd)], [a, b], mid, [BF16, BF16], tm=1024)
    dh, = _mm_nt(f"ffn_dh_{tag}", [(da, wg), (db, wu)], [], _first, [F32], tm=1024, tr=2816)
    d_n, f_n = wg.shape[1], wg.shape[2]
    ns = f_n // N_CHIPS
    tki = _pick(d_n // 2, 512)
    ih = (d_n // 2) // tki
    col_shape = (N_CHIPS, 2, 2, d_n // 2, ns)
    col_block = (None, None, None, tki, ns)
    col_index = lambda g, i, j: (j, i // ih, layer, i % ih, 0)
    dwg = _mm_tn(f"ffn_dwg_{tag}", h, da, 1, col_shape, col_block, col_index, tki=tki, tn=ns, into=into[0])
    dwu = _mm_tn(f"ffn_dwu_{tag}", h, db, 1, col_shape, col_block, col_index, tki=tki, tn=ns, into=into[1])
    tn = _pick(d_n // 2, 512)
    jh = (d_n // 2) // tn
    dwd = _mm_tn(f"ffn_dwd_{tag}", s, dxb, 1, (N_CHIPS, 2, 2, ns, d_n // 2), (None, None, None, ns, tn),
                 lambda g, i, j: (i, j // jh, layer, 0, j % jh), tki=ns, tn=tn, into=into[2])
    return dh, (dwg, dwu, dwd)


def kernel(x, mix_norm, ffn_norm, final_norm, ab_w_in, lb_logits, hg_out_norm, ab_w_out, pool_w, pool_scale, ffn_w_gate, ffn_w_up, ffn_w_down, loss_target, m_mix_norm, m_ffn_norm, m_final_norm, m_ab_w_in, m_lb_logits, m_hg_out_norm, m_ab_w_out, m_pool_w, m_pool_scale, m_ffn_w_gate, m_ffn_w_up, m_ffn_w_down, v_mix_norm, v_ffn_norm, v_final_norm, v_ab_w_in, v_lb_logits, v_hg_out_norm, v_ab_w_out, v_pool_w, v_pool_scale, v_ffn_w_gate, v_ffn_w_up, v_ffn_w_down):
    xs, target = x[0], loss_target[0]
    s_n, d_n = xs.shape
    h_n = d_n // 2 // HEAD
    hw = h_n * HEAD
    n_grp = len(POOL_WINDOWS)
    grp = d_n // n_grp
    c_idx = lax.axis_index("c").astype(jnp.int32).reshape(1)
    chip = 2 * lax.axis_index("x") + lax.axis_index("y")

    units = [
        _Unit(ab_w_in.shape[1:], 1, 0),
        _Unit(ab_w_out.shape[1:], 0, 0),
        _Unit(pool_w.shape[1:], 1, 0),
        _Unit(ffn_w_gate.shape, 2, 1),
        _Unit(ffn_w_up.shape, 2, 1),
        _Unit(ffn_w_down.shape, 1, 2),
    ]
    chip_idx = chip.astype(jnp.int32).reshape(1)
    shards = [ab_w_in[0], ab_w_out[0], pool_w[0], ffn_w_gate, ffn_w_up, ffn_w_down]
    placed = [_place_shard(f"place_{n}", t, u, chip_idx) for n, (t, u) in enumerate(zip(shards, units))]
    w_in, w_out, w_pool, w_gate, w_up, w_down, scale_full = _gather_weights(units, placed, pool_scale)
    w_in3, w_out3 = w_in[None], w_out[None]
    row = lambda t: t.reshape(1, -1)

    h0 = _rms_fwd("norm_mix0", xs, row(mix_norm[0]))
    proj, projb = _mm_nn("proj_in", h0, [w_in3], [], lambda accs, ex: [accs[0], accs[0]], [F32, BF16])
    oraw, o_a, states = _hg_fwd("hgrn_fwd", proj, lb_logits, hg_out_norm, h_n)
    o_b, ltot = _att_fwd("attn_fwd", projb, h_n)
    cat = jnp.concatenate([o_a, o_b], axis=1)
    x1, = _mm_nn("proj_out", cat, [w_out3], [xs], lambda accs, ex: [ex[0] + accs[0]], [F32])
    h1 = _rms_fwd("norm_ffn0", x1, row(ffn_norm[0]))
    x2, a0, b0, s0 = _ffn_fwd("l0", x1, h1, w_gate[0:1], w_up[0:1], w_down[0:1])
    pooled = _pool_fwd("pool_fwd", x2, row(mix_norm[1]))
    x3, mixed = _mm_nn("pool_mix", pooled, [w_pool], [x2, scale_full],
                       lambda accs, ex: [ex[0] + accs[0] * ex[1], accs[0]], [F32, F32], tk=grp, tn=grp)
    h3 = _rms_fwd("norm_ffn1", x3, row(ffn_norm[1]))
    x4, a1, b1, s1 = _ffn_fwd("l1", x3, h3, w_gate[1:2], w_up[1:2], w_down[1:2])

    dx4, d_final, loss = _loss_bwd("loss_bwd", x4, target, row(final_norm))
    dh3, ffn_grads = _ffn_bwd("l1", 1, dx4, h3, a1, b1, s1, w_gate[1:2], w_up[1:2], w_down[1:2], (None, None, None))
    dx3, d_ffn1 = _rms_bwd("norm_ffn1_bwd", dh3, x3, row(ffn_norm[1]), dx4)
    dmixed, d_scale = _scale_bwd("pool_scale_bwd", dx3, mixed, scale_full)
    dpooled, = _mm_nt("pool_dpooled", [(dmixed, w_pool)], [], _first, [F32], to=grp, tr=grp)
    slab_rows = grp // N_CHIPS
    d_pool = _mm_tn("pool_dw", pooled, dmixed, n_grp, (N_CHIPS, 2, n_grp // 2, slab_rows, grp),
                    (None, None, None, slab_rows, grp), lambda g, i, j: (i, g // 2, g % 2, 0, 0),
                    tki=slab_rows, tn=grp)
    dx2, d_mix1 = _pool_bwd("pool_bwd", dpooled, x2, row(mix_norm[1]), dx3)
    dh1, ffn_grads = _ffn_bwd("l0", 0, dx2, h1, a0, b0, s0, w_gate[0:1], w_up[0:1], w_down[0:1], ffn_grads)
    dx1, d_ffn0 = _rms_bwd("norm_ffn0_bwd", dh1, x1, row(ffn_norm[0]), dx2)
    dx1b = dx1.astype(BF16)
    dcat, = _mm_nt("proj_out_dcat", [(dx1b, w_out3)], [], _first, [F32], tm=1024)
    d_wout = _mm_tn("proj_out_dw", cat, dx1b, 1, (1, 2 * hw, d_n), (None, _pick(2 * hw, 512), _pick(d_n, 512)),
                    lambda g, i, j: (g, i, j), tki=_pick(2 * hw, 512), tn=_pick(d_n, 512))
    dqb, dkb, dvb = _att_bwd("attn_bwd", projb, dcat, ltot, h_n)
    dqa, dfa, dia, dga, d_lb, d_hgn = _hg_bwd("hgrn_bwd", proj, dcat, oraw, states, lb_logits, hg_out_norm, h_n)
    dproj = jnp.concatenate([dqa, dfa, dia, dga, dqb, dkb.astype(BF16), dvb.astype(BF16)], axis=1)
    dh0, = _mm_nt("proj_in_dh", [(dproj, w_in3)], [], _first, [F32], tm=1024, tr=3584)
    ns_in = 7 * hw // N_CHIPS
    tki_in, tn_in = _pick(d_n // 2, 512), _pick(ns_in, 896)
    ih_in, jps_in = (d_n // 2) // tki_in, ns_in // tn_in
    d_win = _mm_tn("proj_in_dw", h0, dproj, 1, (N_CHIPS, 2, d_n // 2, ns_in), (None, None, tki_in, tn_in),
                   lambda g, i, j: (j // jps_in, i // ih_in, i % ih_in, j % jps_in), tki=tki_in, tn=tn_in)
    dx0, d_mix0 = _rms_bwd("norm_mix0_bwd", dh0, xs, row(mix_norm[0]), dx1)

    full_grads = [d_win, d_wout, d_pool, *ffn_grads]
    as4 = lambda g, u: g.reshape(N_CHIPS, 2, -1, u.half_shape[-1])
    grads4 = [as4(g, u) for g, u in zip(full_grads, units)]
    from_sibling = _to_sibling(grads4)
    partials = [_add_mine(f"add_sibling_{n}", g, r, c_idx) for n, (g, r) in enumerate(zip(grads4, from_sibling))]
    slots = _to_owners(partials)
    mine = [_add_slots(f"add_chips_{n}", p, s, chip_idx) for n, (p, s) in enumerate(zip(partials, slots))]
    theirs = _share_halves(mine)

    lanes = 2 * d_n
    pad = lambda t: jnp.pad(t.reshape(1, -1), ((0, 0), (0, lanes - t.size)))
    small = jnp.concatenate([
        pad(jnp.concatenate([d_mix0, d_mix1], axis=0)), pad(jnp.concatenate([d_ffn0, d_ffn1], axis=0)),
        pad(d_final), pad(d_lb), pad(d_hgn), pad(d_scale), jnp.zeros((2, lanes), F32)], axis=0)
    small = _allreduce_small(small)
    g_mix = small[0, :2 * d_n].reshape(2, d_n)
    g_ffn = small[1, :2 * d_n].reshape(2, d_n)
    g_final = small[2, :d_n]
    g_lb = small[3, :2 * hw].reshape(2, hw)
    g_hgn = small[4, :HEAD].reshape(1, HEAD)
    g_scale = lax.dynamic_slice(small[5, :d_n], (chip * grp,), (grp,)).reshape(1, grp)
    loss = lax.psum(loss[0, 0], ("x", "y", "c"))

    small_grads = {0: g_mix, 1: g_ffn, 2: g_final, 4: g_lb, 5: g_hgn, 8: g_scale}
    unit_of = {3: 0, 6: 1, 7: 2, 9: 3, 10: 4, 11: 5}
    weights = [mix_norm, ffn_norm, final_norm, ab_w_in, lb_logits, hg_out_norm, ab_w_out, pool_w, pool_scale,
               ffn_w_gate, ffn_w_up, ffn_w_down]
    ms = [m_mix_norm, m_ffn_norm, m_final_norm, m_ab_w_in, m_lb_logits, m_hg_out_norm, m_ab_w_out, m_pool_w,
          m_pool_scale, m_ffn_w_gate, m_ffn_w_up, m_ffn_w_down]
    vs = [v_mix_norm, v_ffn_norm, v_final_norm, v_ab_w_in, v_lb_logits, v_hg_out_norm, v_ab_w_out, v_pool_w,
          v_pool_scale, v_ffn_w_gate, v_ffn_w_up, v_ffn_w_down]
    grads, deltas, new_ms, new_vs = [], [], [], []
    for n, (w, m, v) in enumerate(zip(weights, ms, vs)):
        if n in unit_of:
            u = unit_of[n]
            g, d, nm, nv = _adamw_halves(f"adamw_{n}", units[u], w, m, v, mine[u], theirs[u], c_idx)
        else:
            w2 = w.reshape(1, -1) if w.ndim == 1 else w
            g = small_grads[n].reshape(w2.shape)
            d, nm, nv = _adamw_nd(f"adamw_{n}", w2, g, m.reshape(w2.shape), v.reshape(w2.shape))
        grads.append(g.reshape(w.shape))
        deltas.append(d.reshape(w.shape))
        new_ms.append(nm.reshape(w.shape))
        new_vs.append(nv.reshape(w.shape))
    return (loss, dx0[None], *grads, *deltas, *new_ms, *new_vs)
```

```python
import functools
import math

import jax
import jax.numpy as jnp
from jax import lax
from jax.experimental import pallas as pl
from jax.experimental.pallas import tpu as pltpu

F32 = jnp.float32
BF16 = jnp.bfloat16
HIGHEST = lax.Precision.HIGHEST
MESH = pl.DeviceIdType.MESH
ANY = pl.BlockSpec(memory_space=pl.ANY)

RMS_EPS = 1e-6
LOG2_E = 1.4426950408889634
HEAD = 128
HG_CHUNK = 64
HG_MID = HG_CHUNK // 2 - 1
HG_BLOCK = 512
HG_PAR = 4
ATT_BLOCK = 256
ATT_PAR_FWD = 4
ATT_PAR_BWD = 2
POOL_WINDOWS = (2, 4, 8, 16)
POOL_HALO = 16
N_CHIPS = 4
ADAM_LR, ADAM_B1, ADAM_B2, ADAM_EPS, ADAM_WD, ADAM_STEP = 0.001, 0.9, 0.999, 1e-08, 0.01, 10
VMEM_LIMIT = 56 * 1024 * 1024

NT_DIMS = (((1,), (1,)), ((), ()))
TN_DIMS = (((0,), (0,)), ((), ()))


def _params(*sem):
    return pltpu.CompilerParams(dimension_semantics=sem, vmem_limit_bytes=VMEM_LIMIT)


def _pick(dim, pref, unit=128):
    best = None
    for t in range(unit, min(dim, pref) + 1, unit):
        if dim % t == 0:
            best = t
    return dim if best is None else best


def _sigmoid(z):
    return 1.0 / (1.0 + jnp.exp(-z))


def _dsilu(a, sg):
    return sg * (1.0 + a * (1.0 - sg))


def _mm_nn(name, a, bs, extras, epilogue, out_dtypes, *, tm=1024, tn=512, tk=2048):
    g_n, k_n, n_n = bs[0].shape
    m_n = a.shape[0]
    tm, tn, tk = _pick(m_n, tm, 8), _pick(n_n, tn), _pick(k_n, tk)
    i_n, j_n, kt = m_n // tm, n_n // tn, k_n // tk
    nb, ne, no = len(bs), len(extras), len(out_dtypes)

    def body(*refs):
        a_ref, b_refs, e_refs = refs[0], refs[1:1 + nb], refs[1 + nb:1 + nb + ne]
        o_refs, acc_refs = refs[1 + nb + ne:1 + nb + ne + no], refs[1 + nb + ne + no:]
        k = pl.program_id(3)
        av = a_ref[...]
        prods = [jnp.dot(av, b_ref[...], preferred_element_type=F32) for b_ref in b_refs]

        def finish(accs):
            outs = epilogue(accs, [e[...] for e in e_refs])
            for o_ref, o in zip(o_refs, outs):
                o_ref[...] = o.astype(o_ref.dtype)

        if kt == 1:
            finish(prods)
        else:
            @pl.when(k == 0)
            def _():
                for acc, p in zip(acc_refs, prods):
                    acc[...] = p

            @pl.when(k > 0)
            def _():
                for acc, p in zip(acc_refs, prods):
                    acc[...] += p

            @pl.when(k == kt - 1)
            def _():
                finish([acc[...] for acc in acc_refs])

    in_specs = [pl.BlockSpec((tm, tk), lambda g, i, j, k: (i, g * kt + k))]
    in_specs += [pl.BlockSpec((None, tk, tn), lambda g, i, j, k: (g, k, j)) for _ in bs]
    for e in extras:
        if e.shape[0] == 1:
            in_specs.append(pl.BlockSpec((1, tn), lambda g, i, j, k: (0, g * j_n + j)))
        else:
            in_specs.append(pl.BlockSpec((tm, tn), lambda g, i, j, k: (i, g * j_n + j)))
    out_specs = [pl.BlockSpec((tm, tn), lambda g, i, j, k: (i, g * j_n + j)) for _ in out_dtypes]
    out_shape = [jax.ShapeDtypeStruct((m_n, g_n * n_n), dt) for dt in out_dtypes]
    scratch = [] if kt == 1 else [pltpu.VMEM((tm, tn), F32) for _ in bs]
    return pl.pallas_call(
        body, name=name, grid=(g_n, i_n, j_n, kt), in_specs=in_specs, out_specs=out_specs, out_shape=out_shape,
        scratch_shapes=scratch, compiler_params=_params("parallel", "parallel", "parallel", "arbitrary"),
    )(a, *bs, *extras)


def _mm_nt(name, pairs, extras, epilogue, out_dtypes, *, tm=1024, to=512, tr=2048):
    g_n, kd, n_n = pairs[0][1].shape
    m_n = pairs[0][0].shape[0]
    tm, to, tr = _pick(m_n, tm, 8), _pick(kd, to), _pick(n_n, tr)
    i_n, j_n, rt = m_n // tm, kd // to, n_n // tr
    npairs, ne, no = len(pairs), len(extras), len(out_dtypes)

    def body(*refs):
        ab_refs, e_refs = refs[:2 * npairs], refs[2 * npairs:2 * npairs + ne]
        o_refs, acc_refs = refs[2 * npairs + ne:2 * npairs + ne + no], refs[2 * npairs + ne + no:]
        r = pl.program_id(3)
        prod = None
        for p in range(npairs):
            t = lax.dot_general(ab_refs[2 * p][...], ab_refs[2 * p + 1][...], NT_DIMS, preferred_element_type=F32)
            prod = t if prod is None else prod + t

        def finish(acc):
            outs = epilogue(acc, [e[...] for e in e_refs])
            for o_ref, o in zip(o_refs, outs):
                o_ref[...] = o.astype(o_ref.dtype)

        if rt == 1:
            finish(prod)
        else:
            acc = acc_refs[0]

            @pl.when(r == 0)
            def _():
                acc[...] = prod

            @pl.when(r > 0)
            def _():
                acc[...] += prod

            @pl.when(r == rt - 1)
            def _():
                finish(acc[...])

    in_specs, args = [], []
    for a, b in pairs:
        in_specs.append(pl.BlockSpec((tm, tr), lambda g, i, j, r: (i, g * rt + r)))
        in_specs.append(pl.BlockSpec((None, to, tr), lambda g, i, j, r: (g, j, r)))
        args += [a, b]
    in_specs += [pl.BlockSpec((tm, to), lambda g, i, j, r: (i, g * j_n + j)) for _ in extras]
    out_specs = [pl.BlockSpec((tm, to), lambda g, i, j, r: (i, g * j_n + j)) for _ in out_dtypes]
    out_shape = [jax.ShapeDtypeStruct((m_n, g_n * kd), dt) for dt in out_dtypes]
    scratch = [] if rt == 1 else [pltpu.VMEM((tm, to), F32)]
    return pl.pallas_call(
        body, name=name, grid=(g_n, i_n, j_n, rt), in_specs=in_specs, out_specs=out_specs, out_shape=out_shape,
        scratch_shapes=scratch, compiler_params=_params("parallel", "parallel", "parallel", "arbitrary"),
    )(*args, *extras)


def _mm_tn(name, a, b, g_n, out_shape, out_block, out_index, *, tki, tn, tm=2048, into=None):
    m_n = a.shape[0]
    k_n, n_n = a.shape[1] // g_n, b.shape[1] // g_n
    tm = _pick(m_n, tm, 8)
    i_n, j_n, mt = k_n // tki, n_n // tn, m_n // tm
    assert k_n % tki == 0 and n_n % tn == 0

    def body(*refs):
        a_ref, b_ref = refs[0], refs[1]
        o_ref, acc = refs[-2], refs[-1]
        m = pl.program_id(3)
        prod = lax.dot_general(a_ref[...], b_ref[...], TN_DIMS, preferred_element_type=F32)

        @pl.when(m == 0)
        def _():
            acc[...] = prod

        @pl.when(m > 0)
        def _():
            acc[...] += prod

        @pl.when(m == mt - 1)
        def _():
            o_ref[...] = acc[...].reshape(o_ref.shape)

    in_specs = [pl.BlockSpec((tm, tki), lambda g, i, j, m: (m, g * i_n + i)),
                pl.BlockSpec((tm, tn), lambda g, i, j, m: (m, g * j_n + j))]
    args = [a, b]
    aliases = {}
    if into is not None:
        in_specs.append(ANY)
        args.append(into)
        aliases = {2: 0}
    return pl.pallas_call(
        body, name=name, grid=(g_n, i_n, j_n, mt), in_specs=in_specs,
        out_specs=pl.BlockSpec(out_block, lambda g, i, j, m: out_index(g, i, j)),
        out_shape=jax.ShapeDtypeStruct(out_shape, F32), scratch_shapes=[pltpu.VMEM((tki, tn), F32)],
        input_output_aliases=aliases,
        compiler_params=_params("parallel", "parallel", "parallel", "arbitrary"),
    )(*args)


def _rstd(xv):
    return lax.rsqrt(jnp.mean(xv * xv, axis=-1, keepdims=True) + RMS_EPS)


def _rms_bwd_rows(dh, xv, gain, r):
    dy = dh * gain
    c = jnp.mean(dy * xv, axis=-1, keepdims=True)
    return r * dy - xv * (r * r * r) * c, dh * xv * r


def _fold8(t):
    return t.reshape(t.shape[0] // 8, 8, t.shape[1]).sum(axis=0)


def _rms_fwd(name, x, gain, tm=256):
    s_n, d_n = x.shape
    tm = _pick(s_n, tm, 8)

    def body(x_ref, g_ref, h_ref):
        xv = x_ref[...]
        h_ref[...] = (xv * _rstd(xv) * g_ref[...]).astype(h_ref.dtype)

    return pl.pallas_call(
        body, name=name, grid=(s_n // tm,),
        in_specs=[pl.BlockSpec((tm, d_n), lambda i: (i, 0)), pl.BlockSpec((1, d_n), lambda i: (0, 0))],
        out_specs=pl.BlockSpec((tm, d_n), lambda i: (i, 0)), out_shape=jax.ShapeDtypeStruct((s_n, d_n), BF16),
        compiler_params=_params("parallel"),
    )(x, gain)


def _rms_bwd(name, dh, x, gain, dres, tm=256):
    s_n, d_n = x.shape
    tm = _pick(s_n, tm, 8)
    nblk = s_n // tm

    def body(dh_ref, x_ref, g_ref, dres_ref, dx_ref, dg_ref, acc):
        i = pl.program_id(0)

        @pl.when(i == 0)
        def _():
            acc[...] = jnp.zeros_like(acc)

        xv = x_ref[...]
        dxv, dgt = _rms_bwd_rows(dh_ref[...].astype(F32), xv, g_ref[...], _rstd(xv))
        dx_ref[...] = dres_ref[...] + dxv
        acc[...] += _fold8(dgt)

        @pl.when(i == nblk - 1)
        def _():
            dg_ref[...] = jnp.sum(acc[...], axis=0, keepdims=True)

    row = pl.BlockSpec((tm, d_n), lambda i: (i, 0))
    vec = pl.BlockSpec((1, d_n), lambda i: (0, 0))
    return pl.pallas_call(
        body, name=name, grid=(nblk,), in_specs=[row, row, vec, row], out_specs=[row, vec],
        out_shape=[jax.ShapeDtypeStruct((s_n, d_n), F32), jax.ShapeDtypeStruct((1, d_n), F32)],
        scratch_shapes=[pltpu.VMEM((8, d_n), F32)], compiler_params=_params("arbitrary"),
    )(dh, x, gain, dres)


def _loss_bwd(name, x, target, gain, tm=256):
    s_n, d_n = x.shape
    tm = _pick(s_n, tm, 8)
    nblk = s_n // tm

    def body(x_ref, t_ref, g_ref, dx_ref, dg_ref, loss_ref, acc, lacc):
        i = pl.program_id(0)

        @pl.when(i == 0)
        def _():
            acc[...] = jnp.zeros_like(acc)
            lacc[...] = jnp.zeros_like(lacc)

        xv = x_ref[...]
        gain = g_ref[...]
        r = _rstd(xv)
        diff = xv * r * gain - t_ref[...]
        lacc[...] += _fold8(diff * diff)
        dxv, dgt = _rms_bwd_rows(diff * (1.0 / d_n), xv, gain, r)
        dx_ref[...] = dxv
        acc[...] += _fold8(dgt)

        @pl.when(i == nblk - 1)
        def _():
            dg_ref[...] = jnp.sum(acc[...], axis=0, keepdims=True)
            loss_ref[...] = jnp.sum(lacc[...], keepdims=True) * (0.5 / d_n)

    row = pl.BlockSpec((tm, d_n), lambda i: (i, 0))
    vec = pl.BlockSpec((1, d_n), lambda i: (0, 0))
    return pl.pallas_call(
        body, name=name, grid=(nblk,), in_specs=[row, row, vec],
        out_specs=[row, vec, pl.BlockSpec((1, 1), lambda i: (0, 0))],
        out_shape=[jax.ShapeDtypeStruct((s_n, d_n), F32), jax.ShapeDtypeStruct((1, d_n), F32),
                   jax.ShapeDtypeStruct((1, 1), F32)],
        scratch_shapes=[pltpu.VMEM((8, d_n), F32), pltpu.VMEM((8, d_n), F32)], compiler_params=_params("arbitrary"),
    )(x, target, gain)


def _pool_counts(t_idx, d_n):
    grp = d_n // len(POOL_WINDOWS)
    lane = lax.broadcasted_iota(jnp.int32, (1, d_n), 1) // grp
    win = jnp.zeros((1, d_n), jnp.int32)
    for gi, w in enumerate(POOL_WINDOWS):
        win = jnp.where(lane == gi, w, win)
    return jnp.minimum(t_idx + 1, win).astype(F32), lane


def _window_sums(rows, lane, backward):
    n = rows.shape[0]
    out = rows
    acc = rows
    width = 1
    for gi in range(len(POOL_WINDOWS)):
        shift = (n - width) if backward else width
        acc = acc + pltpu.roll(acc, shift, 0)
        width *= 2
        out = jnp.where(lane >= gi, acc, out)
    return out


def _pool_fwd(name, x, gain, tm=256):
    s_n, d_n = x.shape
    tm = _pick(s_n, tm, POOL_HALO)
    per = tm // POOL_HALO

    def body(x_ref, halo_ref, g_ref, o_ref):
        i = pl.program_id(0)
        halo = jnp.where(i == 0, 0.0, halo_ref[...])
        rows = jnp.concatenate([halo, x_ref[...]], axis=0)
        h = rows * _rstd(rows) * g_ref[...]
        t_idx = i * tm - POOL_HALO + lax.broadcasted_iota(jnp.int32, (tm + POOL_HALO, 1), 0)
        cnt, lane = _pool_counts(t_idx, d_n)
        pooled = _window_sums(h, lane, False) / cnt - h
        o_ref[...] = pooled[POOL_HALO:, :].astype(o_ref.dtype)

    return pl.pallas_call(
        body, name=name, grid=(s_n // tm,),
        in_specs=[pl.BlockSpec((tm, d_n), lambda i: (i, 0)),
                  pl.BlockSpec((POOL_HALO, d_n), lambda i: (jnp.maximum(i * per - 1, 0), 0)),
                  pl.BlockSpec((1, d_n), lambda i: (0, 0))],
        out_specs=pl.BlockSpec((tm, d_n), lambda i: (i, 0)), out_shape=jax.ShapeDtypeStruct((s_n, d_n), BF16),
        compiler_params=_params("parallel"),
    )(x, x, gain)


def _pool_bwd(name, dpooled, x, gain, dres, tm=256):
    s_n, d_n = x.shape
    tm = _pick(s_n, tm, POOL_HALO)
    per = tm // POOL_HALO
    nblk = s_n // tm
    last_halo = s_n // POOL_HALO - 1

    def body(dp_ref, halo_ref, x_ref, g_ref, dres_ref, dx_ref, dg_ref, acc):
        i = pl.program_id(0)

        @pl.when(i == 0)
        def _():
            acc[...] = jnp.zeros_like(acc)

        halo = jnp.where(i == nblk - 1, 0.0, halo_ref[...])
        rows = jnp.concatenate([dp_ref[...], halo], axis=0)
        t_idx = i * tm + lax.broadcasted_iota(jnp.int32, (tm + POOL_HALO, 1), 0)
        cnt, lane = _pool_counts(t_idx, d_n)
        dh = (_window_sums(rows / cnt, lane, True) - rows)[:tm, :]
        xv = x_ref[...]
        dxv, dgt = _rms_bwd_rows(dh, xv, g_ref[...], _rstd(xv))
        dx_ref[...] = dres_ref[...] + dxv
        acc[...] += _fold8(dgt)

        @pl.when(i == nblk - 1)
        def _():
            dg_ref[...] = jnp.sum(acc[...], axis=0, keepdims=True)

    row = pl.BlockSpec((tm, d_n), lambda i: (i, 0))
    vec = pl.BlockSpec((1, d_n), lambda i: (0, 0))
    return pl.pallas_call(
        body, name=name, grid=(nblk,),
        in_specs=[row, pl.BlockSpec((POOL_HALO, d_n), lambda i: (jnp.minimum((i + 1) * per, last_halo), 0)),
                  row, vec, row],
        out_specs=[row, vec],
        out_shape=[jax.ShapeDtypeStruct((s_n, d_n), F32), jax.ShapeDtypeStruct((1, d_n), F32)],
        scratch_shapes=[pltpu.VMEM((8, d_n), F32)], compiler_params=_params("arbitrary"),
    )(dpooled, dpooled, x, gain, dres)


def _scale_bwd(name, dx, mixed, scale, tm=256):
    s_n, d_n = dx.shape
    tm = _pick(s_n, tm, 8)
    nblk = s_n // tm

    def body(dx_ref, mx_ref, sc_ref, dm_ref, ds_ref, acc):
        i = pl.program_id(0)

        @pl.when(i == 0)
        def _():
            acc[...] = jnp.zeros_like(acc)

        dxv = dx_ref[...]
        dm_ref[...] = (dxv * sc_ref[...]).astype(dm_ref.dtype)
        acc[...] += _fold8(dxv * mx_ref[...])

        @pl.when(i == nblk - 1)
        def _():
            ds_ref[...] = jnp.sum(acc[...], axis=0, keepdims=True)

    row = pl.BlockSpec((tm, d_n), lambda i: (i, 0))
    vec = pl.BlockSpec((1, d_n), lambda i: (0, 0))
    return pl.pallas_call(
        body, name=name, grid=(nblk,), in_specs=[row, row, vec], out_specs=[row, vec],
        out_shape=[jax.ShapeDtypeStruct((s_n, d_n), BF16), jax.ShapeDtypeStruct((1, d_n), F32)],
        scratch_shapes=[pltpu.VMEM((8, d_n), F32)], compiler_params=_params("arbitrary"),
    )(dx, mixed, scale)


def _hg_gates(qa, fa, lb):
    sig = _sigmoid(fa)
    f = lb + (1.0 - lb) * sig
    sq = _sigmoid(qa)
    return sig, f, jnp.log(f), 1.0 - f, sq, qa * sq


def _hg_chunk_terms(q, k, g, lincl):
    gc = jnp.dot(lincl, g, precision=HIGHEST, preferred_element_type=F32)
    glast = gc[HG_CHUNK - 1:HG_CHUNK, :]
    gm = gc[HG_MID:HG_MID + 1, :]
    e_q, e_l = jnp.exp(gc), jnp.exp(glast - gc)
    e_m, e_mi = jnp.exp(gc - gm), jnp.exp(gm - gc)
    return glast, (e_q, e_l, e_m, e_mi), (q * e_q, k * e_l, q * e_m, k * e_mi)


def _hg_setup(s_n, n_heads):
    tb = _pick(s_n, HG_BLOCK, HG_CHUNK)
    par = HG_PAR if n_heads % HG_PAR == 0 else 1
    cols = [slice(p * HEAD, (p + 1) * HEAD) for p in range(par)]
    return tb, s_n // tb, tb // HG_CHUNK, par, cols


def _hg_fwd(name, proj, lb_logits, hgain, n_heads):
    s_n = proj.shape[0]
    h_n = n_heads
    tb, nblk, ncb, par, cols = _hg_setup(s_n, h_n)
    c_n = HG_CHUNK
    heads = range(par)

    def body(qa_ref, fa_ref, ia_ref, ga_ref, l_ref, gn_ref, oraw_ref, oa_ref, st_ref, state):
        @pl.when(pl.program_id(1) == 0)
        def _():
            state[...] = jnp.zeros_like(state)

        lv = l_ref[...]
        lbs = [_sigmoid(lv[0:1, c] - lv[1:2, c]) for c in cols]
        row = lax.broadcasted_iota(jnp.int32, (c_n, c_n), 0)
        col = lax.broadcasted_iota(jnp.int32, (c_n, c_n), 1)
        causal = col <= row
        lincl = causal.astype(F32)
        gn = gn_ref[...]

        def chunk(ci, carry):
            sl = pl.ds(pl.multiple_of(ci * c_n, c_n), c_n)
            gates = [_hg_gates(qa_ref[sl, cols[p]], fa_ref[sl, cols[p]], lbs[p]) for p in heads]
            terms = [_hg_chunk_terms(gates[p][5], gates[p][3], gates[p][2], lincl) for p in heads]
            vbs = [ia_ref[sl, cols[p]].astype(BF16) for p in heads]
            sts = [state[p] for p in heads]
            atts = [lax.dot_general(terms[p][2][2].astype(BF16), terms[p][2][3].astype(BF16), NT_DIMS,
                                    preferred_element_type=F32) for p in heads]
            inter = [lax.dot_general(terms[p][2][0].astype(BF16), sts[p].astype(BF16), NT_DIMS,
                                     preferred_element_type=F32) for p in heads]
            grown = [lax.dot_general(vbs[p], terms[p][2][1].astype(BF16), TN_DIMS, preferred_element_type=F32)
                     for p in heads]
            attb = [jnp.where(causal, atts[p], 0.0).astype(BF16) for p in heads]
            outs = [inter[p] + jnp.dot(attb[p], vbs[p], preferred_element_type=F32) for p in heads]
            for p in heads:
                st_ref[p, ci] = sts[p]
                state[p] = sts[p] * jnp.exp(terms[p][0]) + grown[p]
                o = outs[p]
                oraw_ref[sl, cols[p]] = o
                ga = ga_ref[sl, cols[p]]
                oa_ref[sl, cols[p]] = (o * _rstd(o) * gn * (ga * _sigmoid(ga))).astype(oa_ref.dtype)
            return carry

        lax.fori_loop(0, ncb, chunk, 0)

    wide = par * HEAD
    blk = lambda off: pl.BlockSpec((tb, wide), lambda h, c: (c, off // par + h))
    return pl.pallas_call(
        body, name=name, grid=(h_n // par, nblk),
        in_specs=[blk(0), blk(h_n), blk(2 * h_n), blk(3 * h_n),
                  pl.BlockSpec((2, wide), lambda h, c: (0, h)), pl.BlockSpec((1, HEAD), lambda h, c: (0, 0))],
        out_specs=[blk(0), blk(0), pl.BlockSpec((par, ncb, HEAD, HEAD), lambda h, c: (h, c, 0, 0))],
        out_shape=[jax.ShapeDtypeStruct((s_n, h_n * HEAD), F32), jax.ShapeDtypeStruct((s_n, h_n * HEAD), BF16),
                   jax.ShapeDtypeStruct((h_n, s_n // c_n, HEAD, HEAD), F32)],
        scratch_shapes=[pltpu.VMEM((par, HEAD, HEAD), F32)], compiler_params=_params("parallel", "arbitrary"),
    )(proj, proj, proj, proj, lb_logits, hgain)


def _hg_bwd(name, proj, dcat, oraw, states, lb_logits, hgain, n_heads):
    s_n = proj.shape[0]
    h_n = n_heads
    tb, nblk, ncb, par, cols = _hg_setup(s_n, h_n)
    c_n = HG_CHUNK
    n_steps = h_n // par

    def body(qa_ref, fa_ref, ia_ref, ga_ref, doa_ref, oraw_ref, st_ref, l_ref, gn_ref,
             dqa_ref, dfa_ref, dia_ref, dga_ref, dl_ref, dgn_ref, dstate, dlb_acc, dgn_acc):
        h, c = pl.program_id(0), pl.program_id(1)

        @pl.when(c == 0)
        def _():
            dstate[...] = jnp.zeros_like(dstate)
            dlb_acc[...] = jnp.zeros_like(dlb_acc)

        @pl.when((c == 0) & (h == 0))
        def _():
            dgn_acc[...] = jnp.zeros_like(dgn_acc)

        lv = l_ref[...]
        lbs = [_sigmoid(lv[0:1, cc] - lv[1:2, cc]) for cc in cols]
        row = lax.broadcasted_iota(jnp.int32, (c_n, c_n), 0)
        col = lax.broadcasted_iota(jnp.int32, (c_n, c_n), 1)
        causal = col <= row
        lincl = causal.astype(F32)
        uincl = (col >= row).astype(F32)
        is_last = lax.broadcasted_iota(jnp.int32, (c_n, 1), 0) == c_n - 1
        gn = gn_ref[...]

        def head_chunk(p, sl, ci):
            cc, lb = cols[p], lbs[p]
            qa = qa_ref[sl, cc]
            sig, f, g, k, sq, q = _hg_gates(qa, fa_ref[sl, cc], lb)
            glast, (e_q, e_l, e_m, e_mi), (qe, kl, qm, km) = _hg_chunk_terms(q, k, g, lincl)
            yield
            v = ia_ref[sl, cc]
            vb = v.astype(BF16)
            qmb, kmb, qeb, klb = qm.astype(BF16), km.astype(BF16), qe.astype(BF16), kl.astype(BF16)
            att = lax.dot_general(qmb, kmb, NT_DIMS, preferred_element_type=F32)

            o = oraw_ref[sl, cc]
            ga = ga_ref[sl, cc]
            sg = _sigmoid(ga)
            r = _rstd(o)
            doa = doa_ref[sl, cc]
            dn = doa * (ga * sg)
            dga_ref[sl, cc] = (doa * (o * r * gn) * _dsilu(ga, sg)).astype(dga_ref.dtype)
            yield
            attb = jnp.where(causal, att, 0.0).astype(BF16)
            do, dgt = _rms_bwd_rows(dn, o, gn, r)
            dgn_acc[...] += dgt
            dob = do.astype(BF16)

            st0 = st_ref[p, ci]
            ds1 = dstate[p]
            st0b, ds1b = st0.astype(BF16), ds1.astype(BF16)
            datt = lax.dot_general(dob, vb, NT_DIMS, preferred_element_type=F32)
            dv = lax.dot_general(attb, dob, TN_DIMS, preferred_element_type=F32)
            dv = dv + lax.dot_general(klb, ds1b, NT_DIMS, preferred_element_type=F32)
            dqe = jnp.dot(dob, st0b, preferred_element_type=F32)
            dkl = jnp.dot(vb, ds1b, preferred_element_type=F32)
            eg = jnp.exp(glast)
            dstate[p] = ds1 * eg + lax.dot_general(dob, qeb, TN_DIMS, preferred_element_type=F32)
            yield
            dattb = jnp.where(causal, datt, 0.0).astype(BF16)
            dqm = jnp.dot(dattb, kmb, preferred_element_type=F32)
            dkm = lax.dot_general(dattb, qmb, TN_DIMS, preferred_element_type=F32)
            dia_ref[sl, cc] = dv.astype(dia_ref.dtype)
            yield
            dq = dqm * e_m + dqe * e_q
            dk = dkm * e_mi + dkl * e_l
            dgc = dqm * qmb.astype(F32) - dkm * kmb.astype(F32) + dqe * qe - dkl * kl
            dglast = jnp.sum(dkl * kl, axis=0, keepdims=True) + eg * jnp.sum(ds1 * st0, axis=0, keepdims=True)
            dgc = dgc + jnp.where(is_last, dglast, 0.0)
            dg = jnp.dot(uincl, dgc, precision=HIGHEST, preferred_element_type=F32)
            dqa_ref[sl, cc] = (dq * _dsilu(qa, sq)).astype(dqa_ref.dtype)
            yield
            df = dg / f - dk
            dfa_ref[sl, cc] = (df * (1.0 - lb) * sig * (1.0 - sig)).astype(dfa_ref.dtype)
            dlb_acc[p] += df * (1.0 - sig)

        def chunk(idx, carry):
            ci = ncb - 1 - idx
            sl = pl.ds(pl.multiple_of(ci * c_n, c_n), c_n)
            running = [head_chunk(p, sl, ci) for p in range(par)]
            while running:
                running = [g for g in running if next(g, True) is None]
            return carry

        lax.fori_loop(0, ncb, chunk, 0)

        @pl.when(c == nblk - 1)
        def _():
            first = lax.broadcasted_iota(jnp.int32, (2, HEAD), 0) == 0
            for p, cc in enumerate(cols):
                dl0 = jnp.sum(dlb_acc[p], axis=0, keepdims=True) * lbs[p] * (1.0 - lbs[p])
                dl_ref[:, cc] = jnp.where(first, dl0, -dl0)

        @pl.when((c == nblk - 1) & (h == n_steps - 1))
        def _():
            dgn_ref[...] = jnp.sum(dgn_acc[...], axis=0, keepdims=True)

    wide = par * HEAD
    blk = lambda off: pl.BlockSpec((tb, wide), lambda h, c: (nblk - 1 - c, off // par + h))
    out_act = jax.ShapeDtypeStruct((s_n, h_n * HEAD), BF16)
    return pl.pallas_call(
        body, name=name, grid=(n_steps, nblk),
        in_specs=[blk(0), blk(h_n), blk(2 * h_n), blk(3 * h_n), blk(0), blk(0),
                  pl.BlockSpec((par, ncb, HEAD, HEAD), lambda h, c: (h, nblk - 1 - c, 0, 0)),
                  pl.BlockSpec((2, wide), lambda h, c: (0, h)), pl.BlockSpec((1, HEAD), lambda h, c: (0, 0))],
        out_specs=[blk(0), blk(0), blk(0), blk(0), pl.BlockSpec((2, wide), lambda h, c: (0, h)),
                   pl.BlockSpec((1, HEAD), lambda h, c: (0, 0))],
        out_shape=[out_act, out_act, out_act, out_act, jax.ShapeDtypeStruct((2, h_n * HEAD), F32),
                   jax.ShapeDtypeStruct((1, HEAD), F32)],
        scratch_shapes=[pltpu.VMEM((par, HEAD, HEAD), F32), pltpu.VMEM((par, c_n, HEAD), F32),
                        pltpu.VMEM((c_n, HEAD), F32)],
        compiler_params=_params("arbitrary", "arbitrary"),
    )(proj, proj, proj, proj, dcat, oraw, states, lb_logits, hgain)


def _split_dot(t, ones_b):
    return jnp.dot(t.astype(BF16), ones_b, preferred_element_type=F32)


def _softplus(z):
    return jnp.maximum(z, 0.0) + jnp.log(1.0 + jnp.exp2(jnp.abs(z) * (-LOG2_E)))


def _att_setup(projb, n_heads, want):
    s_n = projb.shape[0]
    t_n = _pick(s_n, ATT_BLOCK, 8)
    par = want if n_heads % want == 0 else 1
    cols = [slice(p * HEAD, (p + 1) * HEAD) for p in range(par)]
    wide = par * HEAD
    full = lambda off: pl.BlockSpec((s_n, wide), lambda h, i: (0, off // par + h))
    tile = lambda off: pl.BlockSpec((t_n, wide), lambda h, i: (i, off // par + h))
    return s_n, t_n, par, cols, full, tile


def _att_fwd(name, projb, n_heads, gather_units, gather_fulls):
    h_n = n_heads
    s_n, t_n, par, cols, full, tile_spec = _att_setup(projb, h_n, ATT_PAR_FWD)
    scale = 1.0 / math.sqrt(HEAD)
    ng = len(gather_fulls)
    h_steps, i_steps = h_n // par, s_n // t_n

    def body(*refs):
        q_ref, k_ref, v_ref = refs[:3]
        o_ref, lt_ref = refs[3 + ng:5 + ng]
        plan = _GatherPlan(gather_units, refs[5 + ng:5 + 2 * ng], refs[5 + 2 * ng:])
        h, i = pl.program_id(0), pl.program_id(1)
        pl.when((h == 0) & (i == 0))(plan.start)
        pl.when((h == h_steps - 1) & (i == (5 * i_steps) // 8))(plan.forward)
        pl.when((h == h_steps - 1) & (i == i_steps - 1))(plan.finish)
        row = lax.broadcasted_iota(jnp.int32, (t_n, t_n), 0)
        col = lax.broadcasted_iota(jnp.int32, (t_n, t_n), 1)
        from_here = (row >= col).astype(BF16)
        tri = col < row
        qs = [q_ref[:, c] for c in cols]

        def tile(j, carry, diagonal):
            sl = pl.ds(pl.multiple_of(j * t_n, t_n), t_n)
            zs = [lax.dot_general(qs[p], k_ref[sl, c], NT_DIMS, preferred_element_type=F32)
                  for p, c in enumerate(cols)]
            mid = []
            for p in range(par):
                z = zs[p] * scale
                sp = _softplus(z)
                if diagonal:
                    sp = jnp.where(tri, sp, 0.0)
                mid.append((z - carry[p][1], _split_dot(sp, from_here)))
            out = []
            for p, c in enumerate(cols):
                zr, spent = mid[p]
                w = jnp.exp(zr - spent)
                if diagonal:
                    w = jnp.where(tri, w, 0.0)
                acc = carry[p][0] + jnp.dot(w.astype(BF16), v_ref[sl, c], preferred_element_type=F32)
                out.append((acc, carry[p][1] + spent[:, 0:1]))
            return tuple(out)

        init = tuple((jnp.zeros((t_n, HEAD), F32), jnp.zeros((t_n, 1), F32)) for _ in cols)
        carry = tile(i, init, True)
        carry = lax.fori_loop(0, i, lambda jj, cr: tile(i - 1 - jj, cr, False), carry)
        for p, c in enumerate(cols):
            o_ref[:, c] = carry[p][0].astype(o_ref.dtype)
            lt_ref[:, c] = jnp.broadcast_to(carry[p][1], (t_n, HEAD))

    outs = pl.pallas_call(
        body, name=name, grid=(h_steps, i_steps),
        in_specs=[tile_spec(4 * h_n), full(5 * h_n), full(6 * h_n)] + [ANY] * ng,
        out_specs=[tile_spec(0), tile_spec(0)] + [ANY] * ng,
        out_shape=[jax.ShapeDtypeStruct((s_n, h_n * HEAD), BF16), jax.ShapeDtypeStruct((s_n, h_n * HEAD), F32)]
        + [jax.ShapeDtypeStruct(f.shape, f.dtype) for f in gather_fulls],
        input_output_aliases={3 + u: 2 + u for u in range(ng)},
        scratch_shapes=_GatherPlan.scratch(ng), compiler_params=_params("arbitrary", "arbitrary"),
    )(projb, projb, projb, *gather_fulls)
    return outs[0], outs[1], outs[2:]


def _att_bwd(name, projb, dcat, spent_all, n_heads, partials):
    h_n = n_heads
    s_n, t_n, par, cols, full, tile_spec = _att_setup(projb, h_n, ATT_PAR_BWD)
    scale = 1.0 / math.sqrt(HEAD)
    npart = len(partials)
    h_steps, i_steps = h_n // par, s_n // t_n

    def body(*refs):
        q_ref, k_ref, v_ref, do_ref, lt_ref = refs[:5]
        dq_ref, dk_ref, dv_ref = refs[5 + npart:8 + npart]
        plan = _OwnersPlan(refs[5:5 + npart], refs[8 + npart:8 + 2 * npart], refs[8 + 2 * npart:])
        h, i = pl.program_id(0), pl.program_id(1)
        pl.when((h == 0) & (i == 0))(plan.start)
        pl.when((h == h_steps - 1) & (i == i_steps - 1))(plan.finish)

        @pl.when(i == 0)
        def _():
            dk_ref[...] = jnp.zeros_like(dk_ref)
            dv_ref[...] = jnp.zeros_like(dv_ref)

        row = lax.broadcasted_iota(jnp.int32, (t_n, t_n), 0)
        col = lax.broadcasted_iota(jnp.int32, (t_n, t_n), 1)
        before = (row < col).astype(BF16)
        upto = (row <= col).astype(BF16)
        tri = col < row
        qs = [q_ref[:, c] for c in cols]
        dos = [do_ref[:, c].astype(BF16) for c in cols]
        last = slice(t_n - 1, t_n)

        def tile(j, carry, diagonal):
            sl = pl.ds(pl.multiple_of(j * t_n, t_n), t_n)
            zs = [lax.dot_general(qs[p], k_ref[sl, c], NT_DIMS, preferred_element_type=F32)
                  for p, c in enumerate(cols)]
            dws = [lax.dot_general(dos[p], v_ref[sl, c], NT_DIMS, preferred_element_type=F32)
                   for p, c in enumerate(cols)]
            mid1 = []
            for p in range(par):
                z = zs[p] * scale
                sp = _softplus(z)
                sg = jnp.exp(z - sp)
                if diagonal:
                    sp = jnp.where(tri, sp, 0.0)
                prior = _split_dot(sp, before)
                mid1.append((z - carry[p][1], sg, prior, prior[:, last] + sp[:, last]))
            mid2 = []
            for p in range(par):
                zb, sg, prior, sp_sum = mid1[p]
                w = jnp.exp(zb + prior)
                if diagonal:
                    w = jnp.where(tri, w, 0.0)
                e = dws[p] * w
                mid2.append((w.astype(BF16), e, sg, _split_dot(e, upto), sp_sum))
            out = []
            for p, c in enumerate(cols):
                wb, e, sg, e_upto, sp_sum = mid2[p]
                dz = (e - sg * (carry[p][2] + e_upto)) * scale
                if diagonal:
                    dz = jnp.where(tri, dz, 0.0)
                dz = dz.astype(BF16)
                dq = carry[p][0] + jnp.dot(dz, k_ref[sl, c], preferred_element_type=F32)
                dk_ref[sl, c] += lax.dot_general(dz, qs[p], TN_DIMS, preferred_element_type=F32)
                dv_ref[sl, c] += lax.dot_general(wb, dos[p], TN_DIMS, preferred_element_type=F32)
                out.append((dq, carry[p][1] - sp_sum, carry[p][2] + e_upto[:, last]))
            return tuple(out)

        init = tuple((jnp.zeros((t_n, HEAD), F32), lt_ref[:, c][:, 0:1], jnp.zeros((t_n, 1), F32)) for c in cols)
        carry = lax.fori_loop(0, i, lambda j, cr: tile(j, cr, False), init)
        carry = tile(i, carry, True)
        for p, c in enumerate(cols):
            dq_ref[:, c] = carry[p][0].astype(dq_ref.dtype)

    outs = pl.pallas_call(
        body, name=name, grid=(h_steps, i_steps),
        in_specs=[tile_spec(4 * h_n), full(5 * h_n), full(6 * h_n), tile_spec(h_n), tile_spec(0)] + [ANY] * npart,
        out_specs=[tile_spec(0), full(0), full(0)] + [ANY] * npart,
        out_shape=[jax.ShapeDtypeStruct((s_n, h_n * HEAD), BF16), jax.ShapeDtypeStruct((s_n, h_n * HEAD), F32),
                   jax.ShapeDtypeStruct((s_n, h_n * HEAD), F32)] + _OwnersPlan.out_shapes(partials),
        scratch_shapes=_OwnersPlan.scratch(npart), compiler_params=_params("arbitrary", "arbitrary"),
    )(projb, projb, projb, dcat, spent_all, *partials)
    return outs[0], outs[1], outs[2], outs[3:]


def _adamw(name, w, g, m, v, tr=256):
    r_n, c_n = w.shape
    tr = _pick(r_n, tr, 8)
    c1 = 1.0 - ADAM_B1 ** ADAM_STEP
    c2 = 1.0 - ADAM_B2 ** ADAM_STEP

    def body(w_ref, g_ref, m_ref, v_ref, d_ref, nm_ref, nv_ref):
        gv = g_ref[...]
        nm = ADAM_B1 * m_ref[...] + (1.0 - ADAM_B1) * gv
        nv = ADAM_B2 * v_ref[...] + (1.0 - ADAM_B2) * (gv * gv)
        d_ref[...] = -ADAM_LR * ((nm / c1) / (jnp.sqrt(nv / c2) + ADAM_EPS) + ADAM_WD * w_ref[...])
        nm_ref[...] = nm
        nv_ref[...] = nv

    blk = pl.BlockSpec((tr, c_n), lambda i: (i, 0))
    sds = jax.ShapeDtypeStruct((r_n, c_n), F32)
    return pl.pallas_call(
        body, name=name, grid=(r_n // tr,), in_specs=[blk] * 4, out_specs=[blk] * 3, out_shape=[sds] * 3,
        compiler_params=_params("parallel"),
    )(w, g, m, v)


def _adamw_nd(name, w, g, m, v):
    shape = w.shape
    flat = lambda t: t.reshape(-1, shape[-1])
    return tuple(t.reshape(shape) for t in _adamw(name, flat(w), flat(g.reshape(shape)), flat(m), flat(v)))


def _mesh_pos():
    x, y, c = lax.axis_index("x"), lax.axis_index("y"), lax.axis_index("c")
    chips = [(1 - x, y), (x, 1 - y), (1 - x, 1 - y)]
    return x, y, c, chips, 2 * x + y, [2 * cx + cy for cx, cy in chips]


class _Unit:
    def __init__(self, shard_shape, axis, half_axis):
        self.shard_shape = tuple(shard_shape)
        self.axis = axis
        self.half_axis = half_axis
        self.full_shape = tuple(n * N_CHIPS if a == axis else n for a, n in enumerate(shard_shape))
        self.half_shape = tuple(n // 2 if a == half_axis else n for a, n in enumerate(shard_shape))

    def _window(self, ref, k, c, with_slab):
        idx = []
        for a, n in enumerate(self.shard_shape):
            start, size = 0, n
            if a == self.half_axis:
                size = n // 2
                start = c * size
            if with_slab and a == self.axis:
                start = start + k * n
            idx.append(pl.ds(start, size))
        return ref.at[tuple(idx)]

    def full_half(self, ref, k, c):
        return self._window(ref, k, c, True)

    def place_view(self):
        s = self.shard_shape
        if self.axis == len(s) - 1:
            return math.prod(s[:-2]), s[-2], s[-1], True
        assert self.axis == len(s) - 2
        return math.prod(s[:self.axis]), s[self.axis], s[-1], False

    def half_view(self):
        s, h = self.shard_shape, self.half_axis
        if h == len(s) - 1:
            return math.prod(s[:-2]), s[-2], s[-1] // 2, True
        return math.prod(s[:h]), (s[h] // 2) * math.prod(s[h + 1:-1]), s[-1], False


def _place_shard(name, shard, unit, chip_idx):
    l_n, r_n, c_n, by_cols = unit.place_view()
    tr = _pick(r_n, 256, 16)
    per = r_n // tr

    def body(k_ref, s_ref, o_ref):
        o_ref[...] = s_ref[...].astype(o_ref.dtype)

    if by_cols:
        full3, out_index = (l_n, r_n, N_CHIPS * c_n), (lambda l, i, k_ref: (l, i, k_ref[0]))
    else:
        full3, out_index = (l_n, N_CHIPS * r_n, c_n), (lambda l, i, k_ref: (l, k_ref[0] * per + i, 0))
    out = pl.pallas_call(
        body, name=name,
        grid_spec=pltpu.PrefetchScalarGridSpec(
            num_scalar_prefetch=1, grid=(l_n, per),
            in_specs=[pl.BlockSpec((None, tr, c_n), lambda l, i, k_ref: (l, i, 0))],
            out_specs=pl.BlockSpec((None, tr, c_n), out_index)),
        out_shape=jax.ShapeDtypeStruct(full3, BF16), compiler_params=_params("parallel", "parallel"),
    )(chip_idx, shard.reshape(l_n, r_n, c_n))
    return out.reshape(unit.full_shape)


def _gather_weights(units, fulls, scale_shard):
    nu = len(units)
    ps = scale_shard.shape[1]

    def body(*refs):
        sc_in = refs[nu]
        outs, sc_out = refs[nu + 1:2 * nu + 1], refs[2 * nu + 1]
        send3, recv3, lsem = refs[2 * nu + 2:2 * nu + 5]
        plan = _GatherPlan(units, outs, refs[2 * nu + 5:])
        x, y, c, chips, me, others = _mesh_pos()
        local = pltpu.make_async_copy(sc_in, sc_out.at[:, pl.ds(me * ps, ps)], lsem.at[0])
        local.start()
        plan.start()
        sends = [pltpu.make_async_remote_copy(
            src_ref=sc_in, dst_ref=sc_out.at[:, pl.ds(me * ps, ps)], send_sem=send3.at[j], recv_sem=recv3.at[j],
            device_id=(*chip, c), device_id_type=MESH) for j, chip in enumerate(chips)]
        for cp in sends:
            cp.start()
        plan.forward()
        plan.finish()
        for j in range(3):
            dst = sc_out.at[:, pl.ds(others[j] * ps, ps)]
            pltpu.make_async_remote_copy(src_ref=dst, dst_ref=dst, send_sem=send3.at[j], recv_sem=recv3.at[j],
                                         device_id=(x, y, c), device_id_type=MESH).wait_recv()
        for cp in sends:
            cp.wait_send()
        local.wait()

    out_shape = [jax.ShapeDtypeStruct(f.shape, f.dtype) for f in fulls]
    out_shape.append(jax.ShapeDtypeStruct((1, N_CHIPS * ps), scale_shard.dtype))
    dma = pltpu.SemaphoreType.DMA
    return pl.pallas_call(
        body, name="gather_weights", in_specs=[ANY] * (nu + 1), out_specs=[ANY] * (nu + 1), out_shape=out_shape,
        input_output_aliases={u: u for u in range(nu)},
        scratch_shapes=[dma((3,)), dma((3,)), dma((1,))] + _GatherPlan.scratch(nu),
    )(*fulls, scale_shard)


class _GatherPlan:
    def __init__(self, units, outs, sems):
        self.units, self.outs = units, outs
        self.ici, self.d2d = (sems[0], sems[1]), (sems[2], sems[3])
        self.x, self.y, self.c, self.chips, self.me, self.others = _mesh_pos()
        self.pairs = [(u, j) for u in range(len(units)) for j in range(3)]

    @staticmethod
    def scratch(nu):
        return [pltpu.SemaphoreType.DMA((3 * nu,)) for _ in range(4)]

    def _copy(self, window, sems, u, j, to):
        return pltpu.make_async_remote_copy(src_ref=window, dst_ref=window, send_sem=sems[0].at[3 * u + j],
                                            recv_sem=sems[1].at[3 * u + j], device_id=to, device_id_type=MESH)

    def _half(self, u, chip, core):
        return self.units[u].full_half(self.outs[u], chip, core)

    def start(self):
        for u, j in self.pairs:
            self._copy(self._half(u, self.me, self.c), self.ici, u, j, (*self.chips[j], self.c)).start()

    def forward(self):
        here, sibling = (self.x, self.y, self.c), (self.x, self.y, 1 - self.c)
        for u, j in self.pairs:
            landed = self._half(u, self.others[j], self.c)
            self._copy(landed, self.ici, u, j, here).wait_recv()
            self._copy(landed, self.d2d, u, j, sibling).start()

    def finish(self):
        here = (self.x, self.y, self.c)
        for u, j in self.pairs:
            self._copy(self._half(u, self.others[j], 1 - self.c), self.d2d, u, j, here).wait_recv()
        for u, j in self.pairs:
            self._copy(self._half(u, self.me, self.c), self.ici, u, j, here).wait_send()
            self._copy(self._half(u, self.others[j], self.c), self.d2d, u, j, here).wait_send()


class _OwnersPlan:
    def __init__(self, ins, outs, sems):
        self.ins, self.outs, self.send, self.recv = ins, outs, sems[0], sems[1]
        _, _, self.c, self.chips, _, self.others = _mesh_pos()

    @staticmethod
    def scratch(nu):
        return [pltpu.SemaphoreType.DMA((3 * nu,)) for _ in range(2)]

    @staticmethod
    def out_shapes(partials):
        return [jax.ShapeDtypeStruct((3,) + p.shape[1:], p.dtype) for p in partials]

    def _copies(self):
        return [pltpu.make_async_remote_copy(
            src_ref=self.ins[u].at[self.others[j]], dst_ref=self.outs[u].at[j], send_sem=self.send.at[3 * u + j],
            recv_sem=self.recv.at[3 * u + j], device_id=(*self.chips[j], self.c), device_id_type=MESH)
            for u in range(len(self.ins)) for j in range(3)]

    def start(self):
        for cp in self._copies():
            cp.start()

    def finish(self):
        for cp in self._copies():
            cp.wait()


def _to_sibling(name, grads):
    nu = len(grads)

    def body(*refs):
        ins, outs = refs[:nu], refs[nu:2 * nu]
        send, recv = refs[2 * nu:]
        x, y, c, _, _, _ = _mesh_pos()
        cps = [pltpu.make_async_remote_copy(
            src_ref=ins[u].at[:, 1 - c], dst_ref=outs[u], send_sem=send.at[u], recv_sem=recv.at[u],
            device_id=(x, y, 1 - c), device_id_type=MESH) for u in range(nu)]
        for cp in cps:
            cp.start()
        for cp in cps:
            cp.wait()

    out_shape = [jax.ShapeDtypeStruct((g.shape[0],) + g.shape[2:], g.dtype) for g in grads]
    dma = pltpu.SemaphoreType.DMA
    return pl.pallas_call(
        body, name=name, in_specs=[ANY] * nu, out_specs=[ANY] * nu, out_shape=out_shape,
        scratch_shapes=[dma((nu,)), dma((nu,))],
    )(*grads)


def _to_owners(name, partials):
    nu = len(partials)

    def body(*refs):
        plan = _OwnersPlan(refs[:nu], refs[nu:2 * nu], refs[2 * nu:])
        plan.start()
        plan.finish()

    return pl.pallas_call(
        body, name=name, in_specs=[ANY] * nu, out_specs=[ANY] * nu,
        out_shape=_OwnersPlan.out_shapes(partials), scratch_shapes=_OwnersPlan.scratch(nu),
    )(*partials)


def _share_halves(halves):
    nu = len(halves)

    def body(*refs):
        ins, outs = refs[:nu], refs[nu:2 * nu]
        send, recv = refs[2 * nu:]
        x, y, c, _, _, _ = _mesh_pos()
        cps = [pltpu.make_async_remote_copy(
            src_ref=ins[u], dst_ref=outs[u], send_sem=send.at[u], recv_sem=recv.at[u],
            device_id=(x, y, 1 - c), device_id_type=MESH) for u in range(nu)]
        for cp in cps:
            cp.start()
        for cp in cps:
            cp.wait()

    out_shape = [jax.ShapeDtypeStruct(h.shape, h.dtype) for h in halves]
    dma = pltpu.SemaphoreType.DMA
    return pl.pallas_call(
        body, name="share_halves", in_specs=[ANY] * nu, out_specs=[ANY] * nu, out_shape=out_shape,
        scratch_shapes=[dma((nu,)), dma((nu,))],
    )(*halves)


def _add_mine(name, grad, recv, c_idx):
    _, _, r_n, c_n = grad.shape
    tr = _pick(r_n, 256, 16)

    def body(c_ref, g_ref, r_ref, o_ref):
        o_ref[...] = (g_ref[...] + r_ref[...]).astype(o_ref.dtype)

    return pl.pallas_call(
        body, name=name,
        grid_spec=pltpu.PrefetchScalarGridSpec(
            num_scalar_prefetch=1, grid=(N_CHIPS, r_n // tr),
            in_specs=[pl.BlockSpec((None, None, tr, c_n), lambda k, i, c_ref: (k, c_ref[0], i, 0)),
                      pl.BlockSpec((None, tr, c_n), lambda k, i, c_ref: (k, i, 0))],
            out_specs=pl.BlockSpec((None, tr, c_n), lambda k, i, c_ref: (k, i, 0))),
        out_shape=jax.ShapeDtypeStruct(recv.shape, BF16), compiler_params=_params("parallel", "parallel"),
    )(c_idx, grad, recv)


def _add_slots(name, partial, slots, chip_idx):
    _, r_n, c_n = slots.shape
    tr = _pick(r_n, 256, 16)

    def body(k_ref, p_ref, s_ref, o_ref):
        own = p_ref[...].astype(F32)
        o_ref[...] = ((own + s_ref[0].astype(F32)) + s_ref[1].astype(F32)) + s_ref[2].astype(F32)

    return pl.pallas_call(
        body, name=name,
        grid_spec=pltpu.PrefetchScalarGridSpec(
            num_scalar_prefetch=1, grid=(r_n // tr,),
            in_specs=[pl.BlockSpec((None, tr, c_n), lambda i, k_ref: (k_ref[0], i, 0)),
                      pl.BlockSpec((3, tr, c_n), lambda i, k_ref: (0, i, 0))],
            out_specs=pl.BlockSpec((tr, c_n), lambda i, k_ref: (i, 0))),
        out_shape=jax.ShapeDtypeStruct((r_n, c_n), F32), compiler_params=_params("parallel"),
    )(chip_idx, partial, slots)


def _adamw_halves(name, unit, w, m, v, mine, theirs, c_idx, tr=256):
    l_n, r_n, c_n, by_cols = unit.half_view()
    tr = _pick(r_n, tr, 8)
    c1 = 1.0 - ADAM_B1 ** ADAM_STEP
    c2 = 1.0 - ADAM_B2 ** ADAM_STEP

    def body(c_ref, w_ref, m_ref, v_ref, mine_ref, theirs_ref, g_ref, d_ref, nm_ref, nv_ref):
        gv = jnp.where(pl.program_id(1) == c_ref[0], mine_ref[...], theirs_ref[...])
        nm = ADAM_B1 * m_ref[...] + (1.0 - ADAM_B1) * gv
        nv = ADAM_B2 * v_ref[...] + (1.0 - ADAM_B2) * (gv * gv)
        d_ref[...] = -ADAM_LR * ((nm / c1) / (jnp.sqrt(nv / c2) + ADAM_EPS) + ADAM_WD * w_ref[...])
        g_ref[...] = gv
        nm_ref[...] = nm
        nv_ref[...] = nv

    if by_cols:
        view = (l_n, r_n, 2 * c_n)
        whole = pl.BlockSpec((None, tr, c_n), lambda l, h, i, c_ref: (l, i, h))
    else:
        view = (l_n, 2, r_n, c_n)
        whole = pl.BlockSpec((None, None, tr, c_n), lambda l, h, i, c_ref: (l, h, i, 0))
    half = pl.BlockSpec((None, tr, c_n), lambda l, h, i, c_ref: (l, i, 0))
    sds = jax.ShapeDtypeStruct(view, F32)
    outs = pl.pallas_call(
        body, name=name,
        grid_spec=pltpu.PrefetchScalarGridSpec(
            num_scalar_prefetch=1, grid=(l_n, 2, r_n // tr),
            in_specs=[whole, whole, whole, half, half], out_specs=[whole] * 4),
        out_shape=[sds] * 4, compiler_params=_params("parallel", "parallel", "parallel"),
    )(c_idx, w.reshape(view), m.reshape(view), v.reshape(view),
      mine.reshape(l_n, r_n, c_n), theirs.reshape(l_n, r_n, c_n))
    return tuple(t.reshape(w.shape) for t in outs)


def _allreduce_small(block):
    r_n, c_n = block.shape

    def body(in_ref, out_ref, slots, send, recv):
        x, y, c = lax.axis_index("x"), lax.axis_index("y"), lax.axis_index("c")
        me = 4 * x + 2 * y + c
        slots[me] = in_ref[...]
        flips = [(fx, fy, fc) for fx in (0, 1) for fy in (0, 1) for fc in (0, 1)][1:]
        peers = [(x ^ fx, y ^ fy, c ^ fc) for fx, fy, fc in flips]
        cps = [pltpu.make_async_remote_copy(src_ref=in_ref, dst_ref=slots.at[me], send_sem=send.at[j],
                                            recv_sem=recv.at[j], device_id=peers[j], device_id_type=MESH)
               for j in range(7)]
        for cp in cps:
            cp.start()
        for j, (px, py, pc) in enumerate(peers):
            slot = slots.at[4 * px + 2 * py + pc]
            pltpu.make_async_remote_copy(src_ref=slot, dst_ref=slot, send_sem=send.at[j], recv_sem=recv.at[j],
                                         device_id=(x, y, c), device_id_type=MESH).wait_recv()
        for cp in cps:
            cp.wait_send()
        total = slots[0]
        for d in range(1, 8):
            total = total + slots[d]
        out_ref[...] = total

    vmem = pl.BlockSpec(memory_space=pltpu.VMEM)
    return pl.pallas_call(
        body, name="allreduce_small", in_specs=[vmem], out_specs=vmem,
        out_shape=jax.ShapeDtypeStruct((r_n, c_n), F32),
        scratch_shapes=[pltpu.VMEM((8, r_n, c_n), F32), pltpu.SemaphoreType.DMA((7,)), pltpu.SemaphoreType.DMA((7,))],
    )(block)


def _first(accs, extras):
    return [accs[0]] if isinstance(accs, list) else [accs]


def _ffn_fwd(tag, x_in, h, wg, wu, wd):
    def act(accs, extras):
        a, b = accs
        return [a, b, a * _sigmoid(a) * b]

    a, b, s = _mm_nn(f"ffn_up_{tag}", h, [wg, wu], [], act, [BF16, BF16, BF16])
    x_out, = _mm_nn(f"ffn_down_{tag}", s, [wd], [x_in], lambda accs, ex: [ex[0] + accs[0]], [F32], tk=8192)
    return x_out, a, b, s


def _ffn_bwd(tag, layer, dx_out, h, a, b, s, wg, wu, wd, into):
    def mid(acc, extras):
        av, bv = extras[0].astype(F32), extras[1].astype(F32)
        sg = _sigmoid(av)
        return [acc * bv * _dsilu(av, sg), acc * (av * sg)]

    dxb = dx_out.astype(BF16)
    da, db = _mm_nt(f"ffn_dact_{tag}", [(dxb, wd)], [a, b], mid, [BF16, BF16], tm=1024)
    dh, = _mm_nt(f"ffn_dh_{tag}", [(da, wg), (db, wu)], [], _first, [F32], tm=1024, tr=2816)
    d_n, f_n = wg.shape[1], wg.shape[2]
    ns = f_n // N_CHIPS
    tki = _pick(d_n // 2, 512)
    ih = (d_n // 2) // tki
    col_shape = (N_CHIPS, 2, 2, d_n // 2, ns)
    col_block = (None, None, None, tki, ns)
    col_index = lambda g, i, j: (j, i // ih, layer, i % ih, 0)
    dwg = _mm_tn(f"ffn_dwg_{tag}", h, da, 1, col_shape, col_block, col_index, tki=tki, tn=ns, into=into[0])
    dwu = _mm_tn(f"ffn_dwu_{tag}", h, db, 1, col_shape, col_block, col_index, tki=tki, tn=ns, into=into[1])
    tn = _pick(d_n // 2, 512)
    jh = (d_n // 2) // tn
    dwd = _mm_tn(f"ffn_dwd_{tag}", s, dxb, 1, (N_CHIPS, 2, 2, ns, d_n // 2), (None, None, None, ns, tn),
                 lambda g, i, j: (i, j // jh, layer, 0, j % jh), tki=ns, tn=tn, into=into[2])
    return dh, (dwg, dwu, dwd)


def kernel(x, mix_norm, ffn_norm, final_norm, ab_w_in, lb_logits, hg_out_norm, ab_w_out, pool_w, pool_scale, ffn_w_gate, ffn_w_up, ffn_w_down, loss_target, m_mix_norm, m_ffn_norm, m_final_norm, m_ab_w_in, m_lb_logits, m_hg_out_norm, m_ab_w_out, m_pool_w, m_pool_scale, m_ffn_w_gate, m_ffn_w_up, m_ffn_w_down, v_mix_norm, v_ffn_norm, v_final_norm, v_ab_w_in, v_lb_logits, v_hg_out_norm, v_ab_w_out, v_pool_w, v_pool_scale, v_ffn_w_gate, v_ffn_w_up, v_ffn_w_down):
    xs, target = x[0], loss_target[0]
    s_n, d_n = xs.shape
    h_n = d_n // 2 // HEAD
    hw = h_n * HEAD
    n_grp = len(POOL_WINDOWS)
    grp = d_n // n_grp
    c_idx = lax.axis_index("c").astype(jnp.int32).reshape(1)
    chip = 2 * lax.axis_index("x") + lax.axis_index("y")

    units = [
        _Unit(ab_w_in.shape[1:], 1, 0),
        _Unit(ab_w_out.shape[1:], 0, 0),
        _Unit(pool_w.shape[1:], 1, 0),
        _Unit(ffn_w_gate.shape, 2, 1),
        _Unit(ffn_w_up.shape, 2, 1),
        _Unit(ffn_w_down.shape, 1, 2),
    ]
    chip_idx = chip.astype(jnp.int32).reshape(1)
    shards = [ab_w_in[0], ab_w_out[0], pool_w[0], ffn_w_gate, ffn_w_up, ffn_w_down]
    placed = [_place_shard(f"place_{n}", t, u, chip_idx) for n, (t, u) in enumerate(zip(shards, units))]
    w_in, w_out, scale_full = _gather_weights(units[:2], placed[:2], pool_scale)
    w_in3, w_out3 = w_in[None], w_out[None]
    row = lambda t: t.reshape(1, -1)

    h0 = _rms_fwd("norm_mix0", xs, row(mix_norm[0]))
    proj, projb = _mm_nn("proj_in", h0, [w_in3], [], lambda accs, ex: [accs[0], accs[0]], [F32, BF16])
    oraw, o_a, states = _hg_fwd("hgrn_fwd", proj, lb_logits, hg_out_norm, h_n)
    o_b, ltot, (w_pool, w_gate, w_up, w_down) = _att_fwd("attn_fwd", projb, h_n, units[2:], placed[2:])
    cat = jnp.concatenate([o_a, o_b], axis=1)
    x1, = _mm_nn("proj_out", cat, [w_out3], [xs], lambda accs, ex: [ex[0] + accs[0]], [F32])
    h1 = _rms_fwd("norm_ffn0", x1, row(ffn_norm[0]))
    x2, a0, b0, s0 = _ffn_fwd("l0", x1, h1, w_gate[0:1], w_up[0:1], w_down[0:1])
    pooled = _pool_fwd("pool_fwd", x2, row(mix_norm[1]))
    x3, mixed = _mm_nn("pool_mix", pooled, [w_pool], [x2, scale_full],
                       lambda accs, ex: [ex[0] + accs[0] * ex[1], accs[0]], [F32, F32], tk=grp, tn=grp)
    h3 = _rms_fwd("norm_ffn1", x3, row(ffn_norm[1]))
    x4, a1, b1, s1 = _ffn_fwd("l1", x3, h3, w_gate[1:2], w_up[1:2], w_down[1:2])

    dx4, d_final, loss = _loss_bwd("loss_bwd", x4, target, row(final_norm))
    dh3, ffn_grads = _ffn_bwd("l1", 1, dx4, h3, a1, b1, s1, w_gate[1:2], w_up[1:2], w_down[1:2], (None, None, None))
    dx3, d_ffn1 = _rms_bwd("norm_ffn1_bwd", dh3, x3, row(ffn_norm[1]), dx4)
    dmixed, d_scale = _scale_bwd("pool_scale_bwd", dx3, mixed, scale_full)
    dpooled, = _mm_nt("pool_dpooled", [(dmixed, w_pool)], [], _first, [F32], to=grp, tr=grp)
    slab_rows = grp // N_CHIPS
    d_pool = _mm_tn("pool_dw", pooled, dmixed, n_grp, (N_CHIPS, 2, n_grp // 2, slab_rows, grp),
                    (None, None, None, slab_rows, grp), lambda g, i, j: (i, g // 2, g % 2, 0, 0),
                    tki=slab_rows, tn=grp)
    dx2, d_mix1 = _pool_bwd("pool_bwd", dpooled, x2, row(mix_norm[1]), dx3)
    dh1, ffn_grads = _ffn_bwd("l0", 0, dx2, h1, a0, b0, s0, w_gate[0:1], w_up[0:1], w_down[0:1], ffn_grads)
    dx1, d_ffn0 = _rms_bwd("norm_ffn0_bwd", dh1, x1, row(ffn_norm[0]), dx2)
    dx1b = dx1.astype(BF16)
    dcat, = _mm_nt("proj_out_dcat", [(dx1b, w_out3)], [], _first, [F32], tm=1024)
    d_wout = _mm_tn("proj_out_dw", cat, dx1b, 1, (1, 2 * hw, d_n), (None, _pick(2 * hw, 512), _pick(d_n, 512)),
                    lambda g, i, j: (g, i, j), tki=_pick(2 * hw, 512), tn=_pick(d_n, 512))
    as4 = lambda g, u: g.reshape(N_CHIPS, 2, -1, u.half_shape[-1])
    early4 = [as4(g, u) for g, u in zip([d_wout, d_pool, *ffn_grads], units[1:])]
    early_sib = _to_sibling("grads_to_sibling", early4)
    early_part = [_add_mine(f"add_sibling_{n + 1}", g, r, c_idx) for n, (g, r) in enumerate(zip(early4, early_sib))]
    dqb, dkb, dvb, early_slots = _att_bwd("attn_bwd", projb, dcat, ltot, h_n, early_part)
    dqa, dfa, dia, dga, d_lb, d_hgn = _hg_bwd("hgrn_bwd", proj, dcat, oraw, states, lb_logits, hg_out_norm, h_n)
    dproj = jnp.concatenate([dqa, dfa, dia, dga, dqb, dkb.astype(BF16), dvb.astype(BF16)], axis=1)
    dh0, = _mm_nt("proj_in_dh", [(dproj, w_in3)], [], _first, [F32], tm=1024, tr=3584)
    ns_in = 7 * hw // N_CHIPS
    tki_in, tn_in = _pick(d_n // 2, 512), _pick(ns_in, 896)
    ih_in, jps_in = (d_n // 2) // tki_in, ns_in // tn_in
    d_win = _mm_tn("proj_in_dw", h0, dproj, 1, (N_CHIPS, 2, d_n // 2, ns_in), (None, None, tki_in, tn_in),
                   lambda g, i, j: (j // jps_in, i // ih_in, i % ih_in, j % jps_in), tki=tki_in, tn=tn_in)
    dx0, d_mix0 = _rms_bwd("norm_mix0_bwd", dh0, xs, row(mix_norm[0]), dx1)

    win4 = as4(d_win, units[0])
    win_part = _add_mine("add_sibling_0", win4, _to_sibling("grad_in_to_sibling", [win4])[0], c_idx)
    partials = [win_part, *early_part]
    slots = [_to_owners("partial_in_to_owners", [win_part])[0], *early_slots]
    mine = [_add_slots(f"add_chips_{n}", p, s, chip_idx) for n, (p, s) in enumerate(zip(partials, slots))]
    theirs = _share_halves(mine)

    lanes = 2 * d_n
    pad = lambda t: jnp.pad(t.reshape(1, -1), ((0, 0), (0, lanes - t.size)))
    small = jnp.concatenate([
        pad(jnp.concatenate([d_mix0, d_mix1], axis=0)), pad(jnp.concatenate([d_ffn0, d_ffn1], axis=0)),
        pad(d_final), pad(d_lb), pad(d_hgn), pad(d_scale), jnp.zeros((2, lanes), F32)], axis=0)
    small = _allreduce_small(small)
    g_mix = small[0, :2 * d_n].reshape(2, d_n)
    g_ffn = small[1, :2 * d_n].reshape(2, d_n)
    g_final = small[2, :d_n]
    g_lb = small[3, :2 * hw].reshape(2, hw)
    g_hgn = small[4, :HEAD].reshape(1, HEAD)
    g_scale = lax.dynamic_slice(small[5, :d_n], (chip * grp,), (grp,)).reshape(1, grp)
    loss = lax.psum(loss[0, 0], ("x", "y", "c"))

    small_grads = {0: g_mix, 1: g_ffn, 2: g_final, 4: g_lb, 5: g_hgn, 8: g_scale}
    unit_of = {3: 0, 6: 1, 7: 2, 9: 3, 10: 4, 11: 5}
    weights = [mix_norm, ffn_norm, final_norm, ab_w_in, lb_logits, hg_out_norm, ab_w_out, pool_w, pool_scale,
               ffn_w_gate, ffn_w_up, ffn_w_down]
    ms = [m_mix_norm, m_ffn_norm, m_final_norm, m_ab_w_in, m_lb_logits, m_hg_out_norm, m_ab_w_out, m_pool_w,
          m_pool_scale, m_ffn_w_gate, m_ffn_w_up, m_ffn_w_down]
    vs = [v_mix_norm, v_ffn_norm, v_final_norm, v_ab_w_in, v_lb_logits, v_hg_out_norm, v_ab_w_out, v_pool_w,
          v_pool_scale, v_ffn_w_gate, v_ffn_w_up, v_ffn_w_down]
    grads, deltas, new_ms, new_vs = [], [], [], []
    for n, (w, m, v) in enumerate(zip(weights, ms, vs)):
        if n in unit_of:
            u = unit_of[n]
            g, d, nm, nv = _adamw_halves(f"adamw_{n}", units[u], w, m, v, mine[u], theirs[u], c_idx)
        else:
            w2 = w.reshape(1, -1) if w.ndim == 1 else w
            g = small_grads[n].reshape(w2.shape)
            d, nm, nv = _adamw_nd(f"adamw_{n}", w2, g, m.reshape(w2.shape), v.reshape(w2.shape))
        grads.append(g.reshape(w.shape))
        deltas.append(d.reshape(w.shape))
        new_ms.append(nm.reshape(w.shape))
        new_vs.append(nv.reshape(w.shape))
    return (loss, dx0[None], *grads, *deltas, *new_ms, *new_vs)
```

```python
import functools
import math

import jax
import jax.numpy as jnp
from jax import lax
from jax.experimental import pallas as pl
from jax.experimental.pallas import tpu as pltpu

F32 = jnp.float32
BF16 = jnp.bfloat16
HIGHEST = lax.Precision.HIGHEST
MESH = pl.DeviceIdType.MESH
ANY = pl.BlockSpec(memory_space=pl.ANY)

RMS_EPS = 1e-6
LOG2_E = 1.4426950408889634
HEAD = 128
HG_CHUNK = 64
HG_MID = HG_CHUNK // 2 - 1
HG_BLOCK = 512
HG_PAR = 4
ATT_BLOCK = 256
ATT_PAR_FWD = 4
ATT_PAR_BWD = 2
POOL_WINDOWS = (2, 4, 8, 16)
POOL_HALO = 16
N_CHIPS = 4
ADAM_LR, ADAM_B1, ADAM_B2, ADAM_EPS, ADAM_WD, ADAM_STEP = 0.001, 0.9, 0.999, 1e-08, 0.01, 10
VMEM_LIMIT = 56 * 1024 * 1024
MM_SPLIT = 4

NT_DIMS = (((1,), (1,)), ((), ()))
TN_DIMS = (((0,), (0,)), ((), ()))


def _params(*sem):
    return pltpu.CompilerParams(dimension_semantics=sem, vmem_limit_bytes=VMEM_LIMIT)


def _pick(dim, pref, unit=128):
    best = None
    for t in range(unit, min(dim, pref) + 1, unit):
        if dim % t == 0:
            best = t
    return dim if best is None else best


def _sigmoid(z):
    return 1.0 / (1.0 + jnp.exp(-z))


def _dsilu(a, sg):
    return sg * (1.0 + a * (1.0 - sg))


def _mm_nn(name, a, bs, extras, epilogue, out_dtypes, *, tm=1024, tn=512, tk=2048):
    g_n, k_n, n_n = bs[0].shape
    m_n = a.shape[0]
    tm, tn, tk = _pick(m_n, tm, 8), _pick(n_n, tn), _pick(k_n, tk)
    i_n, j_n, kt = m_n // tm, n_n // tn, k_n // tk
    nb, ne, no = len(bs), len(extras), len(out_dtypes)
    split = MM_SPLIT if tm % (16 * MM_SPLIT) == 0 else 1

    def body(*refs):
        a_ref, b_refs, e_refs = refs[0], refs[1:1 + nb], refs[1 + nb:1 + nb + ne]
        o_refs, acc_refs = refs[1 + nb + ne:1 + nb + ne + no], refs[1 + nb + ne + no:]
        k = pl.program_id(3)

        def finish(accs, rows=slice(None)):
            outs = epilogue(accs, [e[...] if e.shape[0] == 1 else e[rows, :] for e in e_refs])
            for o_ref, o in zip(o_refs, outs):
                o_ref[rows, :] = o.astype(o_ref.dtype)

        if kt == 1:
            chunks = [slice(s * (tm // split), (s + 1) * (tm // split)) for s in range(split)]
            prods = [[jnp.dot(a_ref[rows, :], b_ref[...], preferred_element_type=F32) for b_ref in b_refs]
                     for rows in chunks]
            for rows, p in zip(chunks, prods):
                finish(p, rows)
        else:
            av = a_ref[...]
            prods = [jnp.dot(av, b_ref[...], preferred_element_type=F32) for b_ref in b_refs]

            @pl.when(k == 0)
            def _():
                for acc, p in zip(acc_refs, prods):
                    acc[...] = p

            @pl.when(k > 0)
            def _():
                for acc, p in zip(acc_refs, prods):
                    acc[...] += p

            @pl.when(k == kt - 1)
            def _():
                finish([acc[...] for acc in acc_refs])

    in_specs = [pl.BlockSpec((tm, tk), lambda g, i, j, k: (i, g * kt + k))]
    in_specs += [pl.BlockSpec((None, tk, tn), lambda g, i, j, k: (g, k, j)) for _ in bs]
    for e in extras:
        if e.shape[0] == 1:
            in_specs.append(pl.BlockSpec((1, tn), lambda g, i, j, k: (0, g * j_n + j)))
        else:
            in_specs.append(pl.BlockSpec((tm, tn), lambda g, i, j, k: (i, g * j_n + j)))
    out_specs = [pl.BlockSpec((tm, tn), lambda g, i, j, k: (i, g * j_n + j)) for _ in out_dtypes]
    out_shape = [jax.ShapeDtypeStruct((m_n, g_n * n_n), dt) for dt in out_dtypes]
    scratch = [] if kt == 1 else [pltpu.VMEM((tm, tn), F32) for _ in bs]
    return pl.pallas_call(
        body, name=name, grid=(g_n, i_n, j_n, kt), in_specs=in_specs, out_specs=out_specs, out_shape=out_shape,
        scratch_shapes=scratch, compiler_params=_params("parallel", "parallel", "parallel", "arbitrary"),
    )(a, *bs, *extras)


def _mm_nt(name, pairs, extras, epilogue, out_dtypes, *, tm=1024, to=512, tr=2048, exchange=()):
    g_n, kd, n_n = pairs[0][1].shape
    m_n = pairs[0][0].shape[0]
    tm, to, tr = _pick(m_n, tm, 8), _pick(kd, to), _pick(n_n, tr)
    i_n, j_n, rt = m_n // tm, kd // to, n_n // tr
    npairs, ne, no, nx = len(pairs), len(extras), len(out_dtypes), len(exchange)
    n_in = 2 * npairs + ne
    n_acc = 0 if rt == 1 else 1
    split = MM_SPLIT if tm % (16 * MM_SPLIT) == 0 else 1

    def body(*refs):
        ab_refs, e_refs = refs[:2 * npairs], refs[2 * npairs:n_in]
        o_refs = refs[n_in + nx:n_in + nx + no]
        acc_refs = refs[n_in + 2 * nx + no:n_in + 2 * nx + no + n_acc]
        if nx:
            plan = _OwnersPlan(refs[n_in:n_in + nx], refs[n_in + nx + no:n_in + 2 * nx + no],
                               refs[n_in + 2 * nx + no + n_acc:])
            step = ((pl.program_id(0) * i_n + pl.program_id(1)) * j_n + pl.program_id(2)) * rt + pl.program_id(3)
            pl.when(step == 0)(plan.start)
            pl.when(step == g_n * i_n * j_n * rt - 1)(plan.finish)
        r = pl.program_id(3)

        def product(rows):
            prod = None
            for p in range(npairs):
                t = lax.dot_general(ab_refs[2 * p][rows, :], ab_refs[2 * p + 1][...], NT_DIMS,
                                    preferred_element_type=F32)
                prod = t if prod is None else prod + t
            return prod

        def finish(acc, rows=slice(None)):
            outs = epilogue(acc, [e[rows, :] for e in e_refs])
            for o_ref, o in zip(o_refs, outs):
                o_ref[rows, :] = o.astype(o_ref.dtype)

        if rt == 1:
            chunks = [slice(s * (tm // split), (s + 1) * (tm // split)) for s in range(split)]
            prods = [product(rows) for rows in chunks]
            for rows, p in zip(chunks, prods):
                finish(p, rows)
        else:
            prod = product(slice(None))
            acc = acc_refs[0]

            @pl.when(r == 0)
            def _():
                acc[...] = prod

            @pl.when(r > 0)
            def _():
                acc[...] += prod

            @pl.when(r == rt - 1)
            def _():
                finish(acc[...])

    in_specs, args = [], []
    for a, b in pairs:
        in_specs.append(pl.BlockSpec((tm, tr), lambda g, i, j, r: (i, g * rt + r)))
        in_specs.append(pl.BlockSpec((None, to, tr), lambda g, i, j, r: (g, j, r)))
        args += [a, b]
    in_specs += [pl.BlockSpec((tm, to), lambda g, i, j, r: (i, g * j_n + j)) for _ in extras]
    out_specs = [pl.BlockSpec((tm, to), lambda g, i, j, r: (i, g * j_n + j)) for _ in out_dtypes]
    out_shape = [jax.ShapeDtypeStruct((m_n, g_n * kd), dt) for dt in out_dtypes]
    scratch = [] if rt == 1 else [pltpu.VMEM((tm, to), F32)]
    if not nx:
        return pl.pallas_call(
            body, name=name, grid=(g_n, i_n, j_n, rt), in_specs=in_specs, out_specs=out_specs, out_shape=out_shape,
            scratch_shapes=scratch, compiler_params=_params("parallel", "parallel", "parallel", "arbitrary"),
        )(*args, *extras)
    return pl.pallas_call(
        body, name=name, grid=(g_n, i_n, j_n, rt), in_specs=in_specs + [ANY] * nx, out_specs=out_specs + [ANY] * nx,
        out_shape=out_shape + _OwnersPlan.out_shapes(exchange), scratch_shapes=scratch + _OwnersPlan.scratch(nx),
        compiler_params=_params("arbitrary", "arbitrary", "arbitrary", "arbitrary"),
    )(*args, *extras, *exchange)


def _mm_tn(name, a, b, g_n, out_shape, out_block, out_index, *, tki, tn, tm=2048, into=None):
    m_n = a.shape[0]
    k_n, n_n = a.shape[1] // g_n, b.shape[1] // g_n
    tm = _pick(m_n, tm, 8)
    i_n, j_n, mt = k_n // tki, n_n // tn, m_n // tm
    assert k_n % tki == 0 and n_n % tn == 0

    def body(*refs):
        a_ref, b_ref = refs[0], refs[1]
        o_ref, acc = refs[-2], refs[-1]
        m = pl.program_id(3)
        prod = lax.dot_general(a_ref[...], b_ref[...], TN_DIMS, preferred_element_type=F32)

        @pl.when(m == 0)
        def _():
            acc[...] = prod

        @pl.when(m > 0)
        def _():
            acc[...] += prod

        @pl.when(m == mt - 1)
        def _():
            o_ref[...] = acc[...].reshape(o_ref.shape)

    in_specs = [pl.BlockSpec((tm, tki), lambda g, i, j, m: (m, g * i_n + i)),
                pl.BlockSpec((tm, tn), lambda g, i, j, m: (m, g * j_n + j))]
    args = [a, b]
    aliases = {}
    if into is not None:
        in_specs.append(ANY)
        args.append(into)
        aliases = {2: 0}
    return pl.pallas_call(
        body, name=name, grid=(g_n, i_n, j_n, mt), in_specs=in_specs,
        out_specs=pl.BlockSpec(out_block, lambda g, i, j, m: out_index(g, i, j)),
        out_shape=jax.ShapeDtypeStruct(out_shape, F32), scratch_shapes=[pltpu.VMEM((tki, tn), F32)],
        input_output_aliases=aliases,
        compiler_params=_params("parallel", "parallel", "parallel", "arbitrary"),
    )(*args)


def _rstd(xv):
    return lax.rsqrt(jnp.mean(xv * xv, axis=-1, keepdims=True) + RMS_EPS)


def _rms_bwd_rows(dh, xv, gain, r):
    dy = dh * gain
    c = jnp.mean(dy * xv, axis=-1, keepdims=True)
    return r * dy - xv * (r * r * r) * c, dh * xv * r


def _fold8(t):
    return t.reshape(t.shape[0] // 8, 8, t.shape[1]).sum(axis=0)


def _rms_fwd(name, x, gain, tm=256):
    s_n, d_n = x.shape
    tm = _pick(s_n, tm, 8)

    def body(x_ref, g_ref, h_ref):
        xv = x_ref[...]
        h_ref[...] = (xv * _rstd(xv) * g_ref[...]).astype(h_ref.dtype)

    return pl.pallas_call(
        body, name=name, grid=(s_n // tm,),
        in_specs=[pl.BlockSpec((tm, d_n), lambda i: (i, 0)), pl.BlockSpec((1, d_n), lambda i: (0, 0))],
        out_specs=pl.BlockSpec((tm, d_n), lambda i: (i, 0)), out_shape=jax.ShapeDtypeStruct((s_n, d_n), BF16),
        compiler_params=_params("parallel"),
    )(x, gain)


def _rms_bwd(name, dh, x, gain, dres, tm=256):
    s_n, d_n = x.shape
    tm = _pick(s_n, tm, 8)
    nblk = s_n // tm

    def body(dh_ref, x_ref, g_ref, dres_ref, dx_ref, dg_ref, acc):
        i = pl.program_id(0)

        @pl.when(i == 0)
        def _():
            acc[...] = jnp.zeros_like(acc)

        xv = x_ref[...]
        dxv, dgt = _rms_bwd_rows(dh_ref[...].astype(F32), xv, g_ref[...], _rstd(xv))
        dx_ref[...] = dres_ref[...] + dxv
        acc[...] += _fold8(dgt)

        @pl.when(i == nblk - 1)
        def _():
            dg_ref[...] = jnp.sum(acc[...], axis=0, keepdims=True)

    row = pl.BlockSpec((tm, d_n), lambda i: (i, 0))
    vec = pl.BlockSpec((1, d_n), lambda i: (0, 0))
    return pl.pallas_call(
        body, name=name, grid=(nblk,), in_specs=[row, row, vec, row], out_specs=[row, vec],
        out_shape=[jax.ShapeDtypeStruct((s_n, d_n), F32), jax.ShapeDtypeStruct((1, d_n), F32)],
        scratch_shapes=[pltpu.VMEM((8, d_n), F32)], compiler_params=_params("arbitrary"),
    )(dh, x, gain, dres)


def _loss_bwd(name, x, target, gain, tm=256):
    s_n, d_n = x.shape
    tm = _pick(s_n, tm, 8)
    nblk = s_n // tm

    def body(x_ref, t_ref, g_ref, dx_ref, dg_ref, loss_ref, acc, lacc):
        i = pl.program_id(0)

        @pl.when(i == 0)
        def _():
            acc[...] = jnp.zeros_like(acc)
            lacc[...] = jnp.zeros_like(lacc)

        xv = x_ref[...]
        gain = g_ref[...]
        r = _rstd(xv)
        diff = xv * r * gain - t_ref[...]
        lacc[...] += _fold8(diff * diff)
        dxv, dgt = _rms_bwd_rows(diff * (1.0 / d_n), xv, gain, r)
        dx_ref[...] = dxv
        acc[...] += _fold8(dgt)

        @pl.when(i == nblk - 1)
        def _():
            dg_ref[...] = jnp.sum(acc[...], axis=0, keepdims=True)
            loss_ref[...] = jnp.sum(lacc[...], keepdims=True) * (0.5 / d_n)

    row = pl.BlockSpec((tm, d_n), lambda i: (i, 0))
    vec = pl.BlockSpec((1, d_n), lambda i: (0, 0))
    return pl.pallas_call(
        body, name=name, grid=(nblk,), in_specs=[row, row, vec],
        out_specs=[row, vec, pl.BlockSpec((1, 1), lambda i: (0, 0))],
        out_shape=[jax.ShapeDtypeStruct((s_n, d_n), F32), jax.ShapeDtypeStruct((1, d_n), F32),
                   jax.ShapeDtypeStruct((1, 1), F32)],
        scratch_shapes=[pltpu.VMEM((8, d_n), F32), pltpu.VMEM((8, d_n), F32)], compiler_params=_params("arbitrary"),
    )(x, target, gain)


def _pool_counts(t_idx, d_n):
    grp = d_n // len(POOL_WINDOWS)
    lane = lax.broadcasted_iota(jnp.int32, (1, d_n), 1) // grp
    win = jnp.zeros((1, d_n), jnp.int32)
    for gi, w in enumerate(POOL_WINDOWS):
        win = jnp.where(lane == gi, w, win)
    return jnp.minimum(t_idx + 1, win).astype(F32), lane


def _window_sums(rows, lane, backward):
    n = rows.shape[0]
    out = rows
    acc = rows
    width = 1
    for gi in range(len(POOL_WINDOWS)):
        shift = (n - width) if backward else width
        acc = acc + pltpu.roll(acc, shift, 0)
        width *= 2
        out = jnp.where(lane >= gi, acc, out)
    return out


def _pool_fwd(name, x, gain, tm=256):
    s_n, d_n = x.shape
    tm = _pick(s_n, tm, POOL_HALO)
    per = tm // POOL_HALO

    def body(x_ref, halo_ref, g_ref, o_ref):
        i = pl.program_id(0)
        halo = jnp.where(i == 0, 0.0, halo_ref[...])
        rows = jnp.concatenate([halo, x_ref[...]], axis=0)
        h = rows * _rstd(rows) * g_ref[...]
        t_idx = i * tm - POOL_HALO + lax.broadcasted_iota(jnp.int32, (tm + POOL_HALO, 1), 0)
        cnt, lane = _pool_counts(t_idx, d_n)
        pooled = _window_sums(h, lane, False) / cnt - h
        o_ref[...] = pooled[POOL_HALO:, :].astype(o_ref.dtype)

    return pl.pallas_call(
        body, name=name, grid=(s_n // tm,),
        in_specs=[pl.BlockSpec((tm, d_n), lambda i: (i, 0)),
                  pl.BlockSpec((POOL_HALO, d_n), lambda i: (jnp.maximum(i * per - 1, 0), 0)),
                  pl.BlockSpec((1, d_n), lambda i: (0, 0))],
        out_specs=pl.BlockSpec((tm, d_n), lambda i: (i, 0)), out_shape=jax.ShapeDtypeStruct((s_n, d_n), BF16),
        compiler_params=_params("parallel"),
    )(x, x, gain)


def _pool_bwd(name, dpooled, x, gain, dres, tm=256):
    s_n, d_n = x.shape
    tm = _pick(s_n, tm, POOL_HALO)
    per = tm // POOL_HALO
    nblk = s_n // tm
    last_halo = s_n // POOL_HALO - 1

    def body(dp_ref, halo_ref, x_ref, g_ref, dres_ref, dx_ref, dg_ref, acc):
        i = pl.program_id(0)

        @pl.when(i == 0)
        def _():
            acc[...] = jnp.zeros_like(acc)

        halo = jnp.where(i == nblk - 1, 0.0, halo_ref[...])
        rows = jnp.concatenate([dp_ref[...], halo], axis=0)
        t_idx = i * tm + lax.broadcasted_iota(jnp.int32, (tm + POOL_HALO, 1), 0)
        cnt, lane = _pool_counts(t_idx, d_n)
        dh = (_window_sums(rows / cnt, lane, True) - rows)[:tm, :]
        xv = x_ref[...]
        dxv, dgt = _rms_bwd_rows(dh, xv, g_ref[...], _rstd(xv))
        dx_ref[...] = dres_ref[...] + dxv
        acc[...] += _fold8(dgt)

        @pl.when(i == nblk - 1)
        def _():
            dg_ref[...] = jnp.sum(acc[...], axis=0, keepdims=True)

    row = pl.BlockSpec((tm, d_n), lambda i: (i, 0))
    vec = pl.BlockSpec((1, d_n), lambda i: (0, 0))
    return pl.pallas_call(
        body, name=name, grid=(nblk,),
        in_specs=[row, pl.BlockSpec((POOL_HALO, d_n), lambda i: (jnp.minimum((i + 1) * per, last_halo), 0)),
                  row, vec, row],
        out_specs=[row, vec],
        out_shape=[jax.ShapeDtypeStruct((s_n, d_n), F32), jax.ShapeDtypeStruct((1, d_n), F32)],
        scratch_shapes=[pltpu.VMEM((8, d_n), F32)], compiler_params=_params("arbitrary"),
    )(dpooled, dpooled, x, gain, dres)


def _scale_bwd(name, dx, mixed, scale, tm=256):
    s_n, d_n = dx.shape
    tm = _pick(s_n, tm, 8)
    nblk = s_n // tm

    def body(dx_ref, mx_ref, sc_ref, dm_ref, ds_ref, acc):
        i = pl.program_id(0)

        @pl.when(i == 0)
        def _():
            acc[...] = jnp.zeros_like(acc)

        dxv = dx_ref[...]
        dm_ref[...] = (dxv * sc_ref[...]).astype(dm_ref.dtype)
        acc[...] += _fold8(dxv * mx_ref[...])

        @pl.when(i == nblk - 1)
        def _():
            ds_ref[...] = jnp.sum(acc[...], axis=0, keepdims=True)

    row = pl.BlockSpec((tm, d_n), lambda i: (i, 0))
    vec = pl.BlockSpec((1, d_n), lambda i: (0, 0))
    return pl.pallas_call(
        body, name=name, grid=(nblk,), in_specs=[row, row, vec], out_specs=[row, vec],
        out_shape=[jax.ShapeDtypeStruct((s_n, d_n), BF16), jax.ShapeDtypeStruct((1, d_n), F32)],
        scratch_shapes=[pltpu.VMEM((8, d_n), F32)], compiler_params=_params("arbitrary"),
    )(dx, mixed, scale)


def _hg_gates(qa, fa, lb):
    sig = _sigmoid(fa)
    f = lb + (1.0 - lb) * sig
    sq = _sigmoid(qa)
    return sig, f, jnp.log(f), 1.0 - f, sq, qa * sq


def _hg_chunk_terms(q, k, g, lincl):
    gc = jnp.dot(lincl, g, precision=HIGHEST, preferred_element_type=F32)
    glast = gc[HG_CHUNK - 1:HG_CHUNK, :]
    gm = gc[HG_MID:HG_MID + 1, :]
    e_q, e_l = jnp.exp(gc), jnp.exp(glast - gc)
    e_m, e_mi = jnp.exp(gc - gm), jnp.exp(gm - gc)
    return glast, (e_q, e_l, e_m, e_mi), (q * e_q, k * e_l, q * e_m, k * e_mi)


def _hg_setup(s_n, n_heads):
    tb = _pick(s_n, HG_BLOCK, HG_CHUNK)
    par = HG_PAR if n_heads % HG_PAR == 0 else 1
    cols = [slice(p * HEAD, (p + 1) * HEAD) for p in range(par)]
    return tb, s_n // tb, tb // HG_CHUNK, par, cols


def _hg_fwd(name, proj, lb_logits, hgain, n_heads):
    s_n = proj.shape[0]
    h_n = n_heads
    tb, nblk, ncb, par, cols = _hg_setup(s_n, h_n)
    c_n = HG_CHUNK
    heads = range(par)

    def body(qa_ref, fa_ref, ia_ref, ga_ref, l_ref, gn_ref, oraw_ref, oa_ref, st_ref, state):
        @pl.when(pl.program_id(1) == 0)
        def _():
            state[...] = jnp.zeros_like(state)

        lv = l_ref[...]
        lbs = [_sigmoid(lv[0:1, c] - lv[1:2, c]) for c in cols]
        row = lax.broadcasted_iota(jnp.int32, (c_n, c_n), 0)
        col = lax.broadcasted_iota(jnp.int32, (c_n, c_n), 1)
        causal = col <= row
        lincl = causal.astype(F32)
        gn = gn_ref[...]

        def chunk(ci, carry):
            sl = pl.ds(pl.multiple_of(ci * c_n, c_n), c_n)
            gates = [_hg_gates(qa_ref[sl, cols[p]], fa_ref[sl, cols[p]], lbs[p]) for p in heads]
            terms = [_hg_chunk_terms(gates[p][5], gates[p][3], gates[p][2], lincl) for p in heads]
            vbs = [ia_ref[sl, cols[p]].astype(BF16) for p in heads]
            sts = [state[p] for p in heads]
            atts = [lax.dot_general(terms[p][2][2].astype(BF16), terms[p][2][3].astype(BF16), NT_DIMS,
                                    preferred_element_type=F32) for p in heads]
            inter = [lax.dot_general(terms[p][2][0].astype(BF16), sts[p].astype(BF16), NT_DIMS,
                                     preferred_element_type=F32) for p in heads]
            grown = [lax.dot_general(vbs[p], terms[p][2][1].astype(BF16), TN_DIMS, preferred_element_type=F32)
                     for p in heads]
            attb = [jnp.where(causal, atts[p], 0.0).astype(BF16) for p in heads]
            outs = [inter[p] + jnp.dot(attb[p], vbs[p], preferred_element_type=F32) for p in heads]
            for p in heads:
                st_ref[p, ci] = sts[p]
                state[p] = sts[p] * jnp.exp(terms[p][0]) + grown[p]
                o = outs[p]
                oraw_ref[sl, cols[p]] = o
                ga = ga_ref[sl, cols[p]]
                oa_ref[sl, cols[p]] = (o * _rstd(o) * gn * (ga * _sigmoid(ga))).astype(oa_ref.dtype)
            return carry

        lax.fori_loop(0, ncb, chunk, 0)

    wide = par * HEAD
    blk = lambda off: pl.BlockSpec((tb, wide), lambda h, c: (c, off // par + h))
    return pl.pallas_call(
        body, name=name, grid=(h_n // par, nblk),
        in_specs=[blk(0), blk(h_n), blk(2 * h_n), blk(3 * h_n),
                  pl.BlockSpec((2, wide), lambda h, c: (0, h)), pl.BlockSpec((1, HEAD), lambda h, c: (0, 0))],
        out_specs=[blk(0), blk(0), pl.BlockSpec((par, ncb, HEAD, HEAD), lambda h, c: (h, c, 0, 0))],
        out_shape=[jax.ShapeDtypeStruct((s_n, h_n * HEAD), F32), jax.ShapeDtypeStruct((s_n, h_n * HEAD), BF16),
                   jax.ShapeDtypeStruct((h_n, s_n // c_n, HEAD, HEAD), F32)],
        scratch_shapes=[pltpu.VMEM((par, HEAD, HEAD), F32)], compiler_params=_params("parallel", "arbitrary"),
    )(proj, proj, proj, proj, lb_logits, hgain)


def _hg_bwd(name, proj, dcat, oraw, states, lb_logits, hgain, n_heads):
    s_n = proj.shape[0]
    h_n = n_heads
    tb, nblk, ncb, par, cols = _hg_setup(s_n, h_n)
    c_n = HG_CHUNK
    n_steps = h_n // par

    def body(qa_ref, fa_ref, ia_ref, ga_ref, doa_ref, oraw_ref, st_ref, l_ref, gn_ref,
             dqa_ref, dfa_ref, dia_ref, dga_ref, dl_ref, dgn_ref, dstate, dlb_acc, dgn_acc):
        h, c = pl.program_id(0), pl.program_id(1)

        @pl.when(c == 0)
        def _():
            dstate[...] = jnp.zeros_like(dstate)
            dlb_acc[...] = jnp.zeros_like(dlb_acc)

        @pl.when((c == 0) & (h == 0))
        def _():
            dgn_acc[...] = jnp.zeros_like(dgn_acc)

        lv = l_ref[...]
        lbs = [_sigmoid(lv[0:1, cc] - lv[1:2, cc]) for cc in cols]
        row = lax.broadcasted_iota(jnp.int32, (c_n, c_n), 0)
        col = lax.broadcasted_iota(jnp.int32, (c_n, c_n), 1)
        causal = col <= row
        lincl = causal.astype(F32)
        uincl = (col >= row).astype(F32)
        is_last = lax.broadcasted_iota(jnp.int32, (c_n, 1), 0) == c_n - 1
        gn = gn_ref[...]

        def head_chunk(p, sl, ci):
            cc, lb = cols[p], lbs[p]
            qa = qa_ref[sl, cc]
            sig, f, g, k, sq, q = _hg_gates(qa, fa_ref[sl, cc], lb)
            glast, (e_q, e_l, e_m, e_mi), (qe, kl, qm, km) = _hg_chunk_terms(q, k, g, lincl)
            yield
            v = ia_ref[sl, cc]
            vb = v.astype(BF16)
            qmb, kmb, qeb, klb = qm.astype(BF16), km.astype(BF16), qe.astype(BF16), kl.astype(BF16)
            att = lax.dot_general(qmb, kmb, NT_DIMS, preferred_element_type=F32)

            o = oraw_ref[sl, cc]
            ga = ga_ref[sl, cc]
            sg = _sigmoid(ga)
            r = _rstd(o)
            doa = doa_ref[sl, cc]
            dn = doa * (ga * sg)
            dga_ref[sl, cc] = (doa * (o * r * gn) * _dsilu(ga, sg)).astype(dga_ref.dtype)
            yield
            attb = jnp.where(causal, att, 0.0).astype(BF16)
            do, dgt = _rms_bwd_rows(dn, o, gn, r)
            dgn_acc[...] += dgt
            dob = do.astype(BF16)

            st0 = st_ref[p, ci]
            ds1 = dstate[p]
            st0b, ds1b = st0.astype(BF16), ds1.astype(BF16)
            datt = lax.dot_general(dob, vb, NT_DIMS, preferred_element_type=F32)
            dv = lax.dot_general(attb, dob, TN_DIMS, preferred_element_type=F32)
            dv = dv + lax.dot_general(klb, ds1b, NT_DIMS, preferred_element_type=F32)
            dqe = jnp.dot(dob, st0b, preferred_element_type=F32)
            dkl = jnp.dot(vb, ds1b, preferred_element_type=F32)
            eg = jnp.exp(glast)
            dstate[p] = ds1 * eg + lax.dot_general(dob, qeb, TN_DIMS, preferred_element_type=F32)
            yield
            dattb = jnp.where(causal, datt, 0.0).astype(BF16)
            dqm = jnp.dot(dattb, kmb, preferred_element_type=F32)
            dkm = lax.dot_general(dattb, qmb, TN_DIMS, preferred_element_type=F32)
            dia_ref[sl, cc] = dv.astype(dia_ref.dtype)
            yield
            dq = dqm * e_m + dqe * e_q
            dk = dkm * e_mi + dkl * e_l
            dgc = dqm * qmb.astype(F32) - dkm * kmb.astype(F32) + dqe * qe - dkl * kl
            dglast = jnp.sum(dkl * kl, axis=0, keepdims=True) + eg * jnp.sum(ds1 * st0, axis=0, keepdims=True)
            dgc = dgc + jnp.where(is_last, dglast, 0.0)
            dg = jnp.dot(uincl, dgc, precision=HIGHEST, preferred_element_type=F32)
            dqa_ref[sl, cc] = (dq * _dsilu(qa, sq)).astype(dqa_ref.dtype)
            yield
            df = dg / f - dk
            dfa_ref[sl, cc] = (df * (1.0 - lb) * sig * (1.0 - sig)).astype(dfa_ref.dtype)
            dlb_acc[p] += df * (1.0 - sig)

        def chunk(idx, carry):
            ci = ncb - 1 - idx
            sl = pl.ds(pl.multiple_of(ci * c_n, c_n), c_n)
            running = [head_chunk(p, sl, ci) for p in range(par)]
            while running:
                running = [g for g in running if next(g, True) is None]
            return carry

        lax.fori_loop(0, ncb, chunk, 0)

        @pl.when(c == nblk - 1)
        def _():
            first = lax.broadcasted_iota(jnp.int32, (2, HEAD), 0) == 0
            for p, cc in enumerate(cols):
                dl0 = jnp.sum(dlb_acc[p], axis=0, keepdims=True) * lbs[p] * (1.0 - lbs[p])
                dl_ref[:, cc] = jnp.where(first, dl0, -dl0)

        @pl.when((c == nblk - 1) & (h == n_steps - 1))
        def _():
            dgn_ref[...] = jnp.sum(dgn_acc[...], axis=0, keepdims=True)

    wide = par * HEAD
    blk = lambda off: pl.BlockSpec((tb, wide), lambda h, c: (nblk - 1 - c, off // par + h))
    out_act = jax.ShapeDtypeStruct((s_n, h_n * HEAD), BF16)
    return pl.pallas_call(
        body, name=name, grid=(n_steps, nblk),
        in_specs=[blk(0), blk(h_n), blk(2 * h_n), blk(3 * h_n), blk(0), blk(0),
                  pl.BlockSpec((par, ncb, HEAD, HEAD), lambda h, c: (h, nblk - 1 - c, 0, 0)),
                  pl.BlockSpec((2, wide), lambda h, c: (0, h)), pl.BlockSpec((1, HEAD), lambda h, c: (0, 0))],
        out_specs=[blk(0), blk(0), blk(0), blk(0), pl.BlockSpec((2, wide), lambda h, c: (0, h)),
                   pl.BlockSpec((1, HEAD), lambda h, c: (0, 0))],
        out_shape=[out_act, out_act, out_act, out_act, jax.ShapeDtypeStruct((2, h_n * HEAD), F32),
                   jax.ShapeDtypeStruct((1, HEAD), F32)],
        scratch_shapes=[pltpu.VMEM((par, HEAD, HEAD), F32), pltpu.VMEM((par, c_n, HEAD), F32),
                        pltpu.VMEM((c_n, HEAD), F32)],
        compiler_params=_params("arbitrary", "arbitrary"),
    )(proj, proj, proj, proj, dcat, oraw, states, lb_logits, hgain)


def _split_dot(t, ones_b):
    return jnp.dot(t.astype(BF16), ones_b, preferred_element_type=F32)


def _softplus(z):
    return jnp.maximum(z, 0.0) + jnp.log(1.0 + jnp.exp2(jnp.abs(z) * (-LOG2_E)))


def _att_setup(projb, n_heads, want):
    s_n = projb.shape[0]
    t_n = _pick(s_n, ATT_BLOCK, 8)
    par = want if n_heads % want == 0 else 1
    cols = [slice(p * HEAD, (p + 1) * HEAD) for p in range(par)]
    wide = par * HEAD
    full = lambda off: pl.BlockSpec((s_n, wide), lambda h, i: (0, off // par + h))
    tile = lambda off: pl.BlockSpec((t_n, wide), lambda h, i: (i, off // par + h))
    return s_n, t_n, par, cols, full, tile


def _att_fwd(name, projb, n_heads, gather_units, gather_fulls):
    h_n = n_heads
    s_n, t_n, par, cols, full, tile_spec = _att_setup(projb, h_n, ATT_PAR_FWD)
    scale = 1.0 / math.sqrt(HEAD)
    ng = len(gather_fulls)
    h_steps, i_steps = h_n // par, s_n // t_n

    def body(*refs):
        q_ref, k_ref, v_ref = refs[:3]
        o_ref, lt_ref = refs[3 + ng:5 + ng]
        plan = _GatherPlan(gather_units, refs[5 + ng:5 + 2 * ng], refs[5 + 2 * ng:])
        h, i = pl.program_id(0), pl.program_id(1)
        pl.when((h == 0) & (i == 0))(plan.start)
        pl.when((h == h_steps - 1) & (i == (5 * i_steps) // 8))(plan.forward)
        pl.when((h == h_steps - 1) & (i == i_steps - 1))(plan.finish)
        row = lax.broadcasted_iota(jnp.int32, (t_n, t_n), 0)
        col = lax.broadcasted_iota(jnp.int32, (t_n, t_n), 1)
        from_here = (row >= col).astype(BF16)
        tri = col < row
        qs = [(q_ref[:, c].astype(F32) * scale).astype(BF16) for c in cols]

        def tile(j, carry, diagonal):
            sl = pl.ds(pl.multiple_of(j * t_n, t_n), t_n)
            zs = [lax.dot_general(qs[p], k_ref[sl, c], NT_DIMS, preferred_element_type=F32)
                  for p, c in enumerate(cols)]
            mid = []
            for p in range(par):
                z = zs[p]
                sp = _softplus(z)
                if diagonal:
                    sp = jnp.where(tri, sp, 0.0)
                mid.append((z - carry[p][1], _split_dot(sp, from_here)))
            out = []
            for p, c in enumerate(cols):
                zr, spent = mid[p]
                w = jnp.exp(zr - spent)
                if diagonal:
                    w = jnp.where(tri, w, 0.0)
                acc = carry[p][0] + jnp.dot(w.astype(BF16), v_ref[sl, c], preferred_element_type=F32)
                out.append((acc, carry[p][1] + spent[:, 0:1]))
            return tuple(out)

        init = tuple((jnp.zeros((t_n, HEAD), F32), jnp.zeros((t_n, 1), F32)) for _ in cols)
        carry = tile(i, init, True)
        carry = lax.fori_loop(0, i, lambda jj, cr: tile(i - 1 - jj, cr, False), carry)
        for p, c in enumerate(cols):
            o_ref[:, c] = carry[p][0].astype(o_ref.dtype)
            lt_ref[:, c] = jnp.broadcast_to(carry[p][1], (t_n, HEAD))

    outs = pl.pallas_call(
        body, name=name, grid=(h_steps, i_steps),
        in_specs=[tile_spec(4 * h_n), full(5 * h_n), full(6 * h_n)] + [ANY] * ng,
        out_specs=[tile_spec(0), tile_spec(0)] + [ANY] * ng,
        out_shape=[jax.ShapeDtypeStruct((s_n, h_n * HEAD), BF16), jax.ShapeDtypeStruct((s_n, h_n * HEAD), F32)]
        + [jax.ShapeDtypeStruct(f.shape, f.dtype) for f in gather_fulls],
        input_output_aliases={3 + u: 2 + u for u in range(ng)},
        scratch_shapes=_GatherPlan.scratch(ng), compiler_params=_params("arbitrary", "arbitrary"),
    )(projb, projb, projb, *gather_fulls)
    return outs[0], outs[1], outs[2:]


def _att_bwd(name, projb, dcat, spent_all, n_heads, partials):
    h_n = n_heads
    s_n, t_n, par, cols, full, tile_spec = _att_setup(projb, h_n, ATT_PAR_BWD)
    scale = 1.0 / math.sqrt(HEAD)
    npart = len(partials)
    h_steps, i_steps = h_n // par, s_n // t_n

    def body(*refs):
        q_ref, k_ref, v_ref, do_ref, lt_ref = refs[:5]
        dq_ref, dk_ref, dv_ref = refs[5 + npart:8 + npart]
        plan = _OwnersPlan(refs[5:5 + npart], refs[8 + npart:8 + 2 * npart], refs[8 + 2 * npart:10 + 2 * npart])
        dkt, dvt = refs[10 + 2 * npart:]
        h, i = pl.program_id(0), pl.program_id(1)
        pl.when((h == 0) & (i == 0))(plan.start)
        pl.when((h == h_steps - 1) & (i == i_steps - 1))(plan.finish)

        @pl.when(i == 0)
        def _():
            dkt[...] = jnp.zeros_like(dkt)
            dvt[...] = jnp.zeros_like(dvt)

        row = lax.broadcasted_iota(jnp.int32, (t_n, t_n), 0)
        col = lax.broadcasted_iota(jnp.int32, (t_n, t_n), 1)
        before = (row < col).astype(BF16)
        upto = (row <= col).astype(BF16)
        tri = col < row
        q32 = [q_ref[:, c].astype(F32) * scale for c in cols]
        qs = [t.astype(BF16) for t in q32]
        qts = [t.T.astype(BF16) for t in q32]
        do32 = [do_ref[:, c] for c in cols]
        dos = [t.astype(BF16) for t in do32]
        dots = [t.T.astype(BF16) for t in do32]
        last = slice(t_n - 1, t_n)

        def tile(j, carry, diagonal):
            sl = pl.ds(pl.multiple_of(j * t_n, t_n), t_n)
            zs = [lax.dot_general(qs[p], k_ref[sl, c], NT_DIMS, preferred_element_type=F32)
                  for p, c in enumerate(cols)]
            dws = [lax.dot_general(dos[p], v_ref[sl, c], NT_DIMS, preferred_element_type=F32)
                   for p, c in enumerate(cols)]
            mid1 = []
            for p in range(par):
                z = zs[p]
                sp = _softplus(z)
                sg = jnp.exp(z - sp)
                if diagonal:
                    sp = jnp.where(tri, sp, 0.0)
                prior = _split_dot(sp, before)
                mid1.append((z - carry[p][1], sg, prior, prior[:, last] + sp[:, last]))
            mid2 = []
            for p in range(par):
                zb, sg, prior, sp_sum = mid1[p]
                w = jnp.exp(zb + prior)
                if diagonal:
                    w = jnp.where(tri, w, 0.0)
                e = dws[p] * w
                mid2.append((w.astype(BF16), e, sg, _split_dot(e, upto), sp_sum))
            out = []
            for p, c in enumerate(cols):
                wb, e, sg, e_upto, sp_sum = mid2[p]
                dz = e - sg * (carry[p][2] + e_upto)
                if diagonal:
                    dz = jnp.where(tri, dz, 0.0)
                dz = dz.astype(BF16)
                dq = carry[p][0] + jnp.dot(dz, k_ref[sl, c], preferred_element_type=F32)
                dkt[p, j] += jnp.dot(qts[p], dz, preferred_element_type=F32)
                dvt[p, j] += jnp.dot(dots[p], wb, preferred_element_type=F32)
                out.append((dq, carry[p][1] - sp_sum, carry[p][2] + e_upto[:, last]))
            return tuple(out)

        init = tuple((jnp.zeros((t_n, HEAD), F32), lt_ref[:, c][:, 0:1], jnp.zeros((t_n, 1), F32)) for c in cols)
        carry = lax.fori_loop(0, i, lambda j, cr: tile(j, cr, False), init)
        carry = tile(i, carry, True)
        for p, c in enumerate(cols):
            dq_ref[:, c] = (carry[p][0] * scale).astype(dq_ref.dtype)

        @pl.when(i == i_steps - 1)
        def _():
            def put(j, _):
                sl = pl.ds(pl.multiple_of(j * t_n, t_n), t_n)
                for p, c in enumerate(cols):
                    dk_ref[sl, c] = dkt[p, j].T.astype(dk_ref.dtype)
                    dv_ref[sl, c] = dvt[p, j].T.astype(dv_ref.dtype)
                return 0

            lax.fori_loop(0, i_steps, put, 0)

    act = jax.ShapeDtypeStruct((s_n, h_n * HEAD), BF16)
    acc_t = pltpu.VMEM((par, i_steps, HEAD, t_n), F32)
    outs = pl.pallas_call(
        body, name=name, grid=(h_steps, i_steps),
        in_specs=[tile_spec(4 * h_n), full(5 * h_n), full(6 * h_n), tile_spec(h_n), tile_spec(0)] + [ANY] * npart,
        out_specs=[tile_spec(0), full(0), full(0)] + [ANY] * npart,
        out_shape=[act, act, act] + _OwnersPlan.out_shapes(partials),
        scratch_shapes=_OwnersPlan.scratch(npart) + [acc_t, acc_t],
        compiler_params=_params("arbitrary", "arbitrary"),
    )(projb, projb, projb, dcat, spent_all, *partials)
    return outs[0], outs[1], outs[2], outs[3:]


def _adamw(name, w, g, m, v, tr=256):
    r_n, c_n = w.shape
    tr = _pick(r_n, tr, 8)
    c1 = 1.0 - ADAM_B1 ** ADAM_STEP
    c2 = 1.0 - ADAM_B2 ** ADAM_STEP

    def body(w_ref, g_ref, m_ref, v_ref, d_ref, nm_ref, nv_ref):
        gv = g_ref[...]
        nm = ADAM_B1 * m_ref[...] + (1.0 - ADAM_B1) * gv
        nv = ADAM_B2 * v_ref[...] + (1.0 - ADAM_B2) * (gv * gv)
        d_ref[...] = -ADAM_LR * ((nm / c1) / (jnp.sqrt(nv / c2) + ADAM_EPS) + ADAM_WD * w_ref[...])
        nm_ref[...] = nm
        nv_ref[...] = nv

    blk = pl.BlockSpec((tr, c_n), lambda i: (i, 0))
    sds = jax.ShapeDtypeStruct((r_n, c_n), F32)
    return pl.pallas_call(
        body, name=name, grid=(r_n // tr,), in_specs=[blk] * 4, out_specs=[blk] * 3, out_shape=[sds] * 3,
        compiler_params=_params("parallel"),
    )(w, g, m, v)


def _adamw_nd(name, w, g, m, v):
    shape = w.shape
    flat = lambda t: t.reshape(-1, shape[-1])
    return tuple(t.reshape(shape) for t in _adamw(name, flat(w), flat(g.reshape(shape)), flat(m), flat(v)))


def _mesh_pos():
    x, y, c = lax.axis_index("x"), lax.axis_index("y"), lax.axis_index("c")
    chips = [(1 - x, y), (x, 1 - y), (1 - x, 1 - y)]
    return x, y, c, chips, 2 * x + y, [2 * cx + cy for cx, cy in chips]


class _Unit:
    def __init__(self, shard_shape, axis, half_axis):
        self.shard_shape = tuple(shard_shape)
        self.axis = axis
        self.half_axis = half_axis
        self.full_shape = tuple(n * N_CHIPS if a == axis else n for a, n in enumerate(shard_shape))
        self.half_shape = tuple(n // 2 if a == half_axis else n for a, n in enumerate(shard_shape))

    def _window(self, ref, k, c, with_slab):
        idx = []
        for a, n in enumerate(self.shard_shape):
            start, size = 0, n
            if a == self.half_axis:
                size = n // 2
                start = c * size
            if with_slab and a == self.axis:
                start = start + k * n
            idx.append(pl.ds(start, size))
        return ref.at[tuple(idx)]

    def full_half(self, ref, k, c):
        return self._window(ref, k, c, True)

    def place_view(self):
        s = self.shard_shape
        if self.axis == len(s) - 1:
            return math.prod(s[:-2]), s[-2], s[-1], True
        assert self.axis == len(s) - 2
        return math.prod(s[:self.axis]), s[self.axis], s[-1], False

    def half_view(self):
        s, h = self.shard_shape, self.half_axis
        if h == len(s) - 1:
            return math.prod(s[:-2]), s[-2], s[-1] // 2, True
        return math.prod(s[:h]), (s[h] // 2) * math.prod(s[h + 1:-1]), s[-1], False


def _place_shard(name, shard, unit, chip_idx):
    l_n, r_n, c_n, by_cols = unit.place_view()
    tr = _pick(r_n, 256, 16)
    per = r_n // tr

    def body(k_ref, s_ref, o_ref):
        o_ref[...] = s_ref[...].astype(o_ref.dtype)

    if by_cols:
        full3, out_index = (l_n, r_n, N_CHIPS * c_n), (lambda l, i, k_ref: (l, i, k_ref[0]))
    else:
        full3, out_index = (l_n, N_CHIPS * r_n, c_n), (lambda l, i, k_ref: (l, k_ref[0] * per + i, 0))
    out = pl.pallas_call(
        body, name=name,
        grid_spec=pltpu.PrefetchScalarGridSpec(
            num_scalar_prefetch=1, grid=(l_n, per),
            in_specs=[pl.BlockSpec((None, tr, c_n), lambda l, i, k_ref: (l, i, 0))],
            out_specs=pl.BlockSpec((None, tr, c_n), out_index)),
        out_shape=jax.ShapeDtypeStruct(full3, BF16), compiler_params=_params("parallel", "parallel"),
    )(chip_idx, shard.reshape(l_n, r_n, c_n))
    return out.reshape(unit.full_shape)


def _gather_weights(units, fulls, scale_shard):
    nu = len(units)
    ps = scale_shard.shape[1]

    def body(*refs):
        sc_in = refs[nu]
        outs, sc_out = refs[nu + 1:2 * nu + 1], refs[2 * nu + 1]
        send3, recv3, lsem = refs[2 * nu + 2:2 * nu + 5]
        plan = _GatherPlan(units, outs, refs[2 * nu + 5:])
        x, y, c, chips, me, others = _mesh_pos()
        local = pltpu.make_async_copy(sc_in, sc_out.at[:, pl.ds(me * ps, ps)], lsem.at[0])
        local.start()
        plan.start()
        sends = [pltpu.make_async_remote_copy(
            src_ref=sc_in, dst_ref=sc_out.at[:, pl.ds(me * ps, ps)], send_sem=send3.at[j], recv_sem=recv3.at[j],
            device_id=(*chip, c), device_id_type=MESH) for j, chip in enumerate(chips)]
        for cp in sends:
            cp.start()
        plan.forward()
        plan.finish()
        for j in range(3):
            dst = sc_out.at[:, pl.ds(others[j] * ps, ps)]
            pltpu.make_async_remote_copy(src_ref=dst, dst_ref=dst, send_sem=send3.at[j], recv_sem=recv3.at[j],
                                         device_id=(x, y, c), device_id_type=MESH).wait_recv()
        for cp in sends:
            cp.wait_send()
        local.wait()

    out_shape = [jax.ShapeDtypeStruct(f.shape, f.dtype) for f in fulls]
    out_shape.append(jax.ShapeDtypeStruct((1, N_CHIPS * ps), scale_shard.dtype))
    dma = pltpu.SemaphoreType.DMA
    return pl.pallas_call(
        body, name="gather_weights", in_specs=[ANY] * (nu + 1), out_specs=[ANY] * (nu + 1), out_shape=out_shape,
        input_output_aliases={u: u for u in range(nu)},
        scratch_shapes=[dma((3,)), dma((3,)), dma((1,))] + _GatherPlan.scratch(nu),
    )(*fulls, scale_shard)


class _GatherPlan:
    def __init__(self, units, outs, sems):
        self.units, self.outs = units, outs
        self.ici, self.d2d = (sems[0], sems[1]), (sems[2], sems[3])
        self.x, self.y, self.c, self.chips, self.me, self.others = _mesh_pos()
        self.pairs = [(u, j) for u in range(len(units)) for j in range(3)]

    @staticmethod
    def scratch(nu):
        return [pltpu.SemaphoreType.DMA((3 * nu,)) for _ in range(4)]

    def _copy(self, window, sems, u, j, to):
        return pltpu.make_async_remote_copy(src_ref=window, dst_ref=window, send_sem=sems[0].at[3 * u + j],
                                            recv_sem=sems[1].at[3 * u + j], device_id=to, device_id_type=MESH)

    def _half(self, u, chip, core):
        return self.units[u].full_half(self.outs[u], chip, core)

    def start(self):
        for u, j in self.pairs:
            self._copy(self._half(u, self.me, self.c), self.ici, u, j, (*self.chips[j], self.c)).start()

    def forward(self):
        here, sibling = (self.x, self.y, self.c), (self.x, self.y, 1 - self.c)
        for u, j in self.pairs:
            landed = self._half(u, self.others[j], self.c)
            self._copy(landed, self.ici, u, j, here).wait_recv()
            self._copy(landed, self.d2d, u, j, sibling).start()

    def finish(self):
        here = (self.x, self.y, self.c)
        for u, j in self.pairs:
            self._copy(self._half(u, self.others[j], 1 - self.c), self.d2d, u, j, here).wait_recv()
        for u, j in self.pairs:
            self._copy(self._half(u, self.me, self.c), self.ici, u, j, here).wait_send()
            self._copy(self._half(u, self.others[j], self.c), self.d2d, u, j, here).wait_send()


class _OwnersPlan:
    def __init__(self, ins, outs, sems):
        self.ins, self.outs, self.send, self.recv = ins, outs, sems[0], sems[1]
        _, _, self.c, self.chips, _, self.others = _mesh_pos()

    @staticmethod
    def scratch(nu):
        return [pltpu.SemaphoreType.DMA((3 * nu,)) for _ in range(2)]

    @staticmethod
    def out_shapes(partials):
        return [jax.ShapeDtypeStruct((3,) + p.shape[1:], p.dtype) for p in partials]

    def _copies(self):
        return [pltpu.make_async_remote_copy(
            src_ref=self.ins[u].at[self.others[j]], dst_ref=self.outs[u].at[j], send_sem=self.send.at[3 * u + j],
            recv_sem=self.recv.at[3 * u + j], device_id=(*self.chips[j], self.c), device_id_type=MESH)
            for u in range(len(self.ins)) for j in range(3)]

    def start(self):
        for cp in self._copies():
            cp.start()

    def finish(self):
        for cp in self._copies():
            cp.wait()


def _to_sibling(name, grads):
    nu = len(grads)

    def body(*refs):
        ins, outs = refs[:nu], refs[nu:2 * nu]
        send, recv = refs[2 * nu:]
        x, y, c, _, _, _ = _mesh_pos()
        cps = [pltpu.make_async_remote_copy(
            src_ref=ins[u].at[:, 1 - c], dst_ref=outs[u], send_sem=send.at[u], recv_sem=recv.at[u],
            device_id=(x, y, 1 - c), device_id_type=MESH) for u in range(nu)]
        for cp in cps:
            cp.start()
        for cp in cps:
            cp.wait()

    out_shape = [jax.ShapeDtypeStruct((g.shape[0],) + g.shape[2:], g.dtype) for g in grads]
    dma = pltpu.SemaphoreType.DMA
    return pl.pallas_call(
        body, name=name, in_specs=[ANY] * nu, out_specs=[ANY] * nu, out_shape=out_shape,
        scratch_shapes=[dma((nu,)), dma((nu,))],
    )(*grads)


def _share_halves(halves):
    nu = len(halves)

    def body(*refs):
        ins, outs = refs[:nu], refs[nu:2 * nu]
        send, recv = refs[2 * nu:]
        x, y, c, _, _, _ = _mesh_pos()
        cps = [pltpu.make_async_remote_copy(
            src_ref=ins[u], dst_ref=outs[u], send_sem=send.at[u], recv_sem=recv.at[u],
            device_id=(x, y, 1 - c), device_id_type=MESH) for u in range(nu)]
        for cp in cps:
            cp.start()
        for cp in cps:
            cp.wait()

    out_shape = [jax.ShapeDtypeStruct(h.shape, h.dtype) for h in halves]
    dma = pltpu.SemaphoreType.DMA
    return pl.pallas_call(
        body, name="share_halves", in_specs=[ANY] * nu, out_specs=[ANY] * nu, out_shape=out_shape,
        scratch_shapes=[dma((nu,)), dma((nu,))],
    )(*halves)


def _add_mine(name, grad, recv, c_idx):
    _, _, r_n, c_n = grad.shape
    tr = _pick(r_n, 256, 16)

    def body(c_ref, g_ref, r_ref, o_ref):
        o_ref[...] = (g_ref[...] + r_ref[...]).astype(o_ref.dtype)

    return pl.pallas_call(
        body, name=name,
        grid_spec=pltpu.PrefetchScalarGridSpec(
            num_scalar_prefetch=1, grid=(N_CHIPS, r_n // tr),
            in_specs=[pl.BlockSpec((None, None, tr, c_n), lambda k, i, c_ref: (k, c_ref[0], i, 0)),
                      pl.BlockSpec((None, tr, c_n), lambda k, i, c_ref: (k, i, 0))],
            out_specs=pl.BlockSpec((None, tr, c_n), lambda k, i, c_ref: (k, i, 0))),
        out_shape=jax.ShapeDtypeStruct(recv.shape, BF16), compiler_params=_params("parallel", "parallel"),
    )(c_idx, grad, recv)


def _add_slots(name, partial, slots, chip_idx):
    _, r_n, c_n = slots.shape
    tr = _pick(r_n, 256, 16)

    def body(k_ref, p_ref, s_ref, o_ref):
        own = p_ref[...].astype(F32)
        o_ref[...] = ((own + s_ref[0].astype(F32)) + s_ref[1].astype(F32)) + s_ref[2].astype(F32)

    return pl.pallas_call(
        body, name=name,
        grid_spec=pltpu.PrefetchScalarGridSpec(
            num_scalar_prefetch=1, grid=(r_n // tr,),
            in_specs=[pl.BlockSpec((None, tr, c_n), lambda i, k_ref: (k_ref[0], i, 0)),
                      pl.BlockSpec((3, tr, c_n), lambda i, k_ref: (0, i, 0))],
            out_specs=pl.BlockSpec((tr, c_n), lambda i, k_ref: (i, 0))),
        out_shape=jax.ShapeDtypeStruct((r_n, c_n), F32), compiler_params=_params("parallel"),
    )(chip_idx, partial, slots)


def _adamw_halves(name, unit, w, m, v, mine, theirs, c_idx, tr=256):
    l_n, r_n, c_n, by_cols = unit.half_view()
    tr = _pick(r_n, tr, 8)
    c1 = 1.0 - ADAM_B1 ** ADAM_STEP
    c2 = 1.0 - ADAM_B2 ** ADAM_STEP

    def body(c_ref, w_ref, m_ref, v_ref, mine_ref, theirs_ref, g_ref, d_ref, nm_ref, nv_ref):
        gv = jnp.where(pl.program_id(1) == c_ref[0], mine_ref[...], theirs_ref[...])
        nm = ADAM_B1 * m_ref[...] + (1.0 - ADAM_B1) * gv
        nv = ADAM_B2 * v_ref[...] + (1.0 - ADAM_B2) * (gv * gv)
        d_ref[...] = -ADAM_LR * ((nm / c1) / (jnp.sqrt(nv / c2) + ADAM_EPS) + ADAM_WD * w_ref[...])
        g_ref[...] = gv
        nm_ref[...] = nm
        nv_ref[...] = nv

    if by_cols:
        view = (l_n, r_n, 2 * c_n)
        whole = pl.BlockSpec((None, tr, c_n), lambda l, h, i, c_ref: (l, i, h))
    else:
        view = (l_n, 2, r_n, c_n)
        whole = pl.BlockSpec((None, None, tr, c_n), lambda l, h, i, c_ref: (l, h, i, 0))
    mine_spec = pl.BlockSpec((None, tr, c_n), lambda l, h, i, c_ref: (l, jnp.where(h == c_ref[0], i, 0), 0))
    theirs_spec = pl.BlockSpec((None, tr, c_n), lambda l, h, i, c_ref: (l, jnp.where(h == c_ref[0], 0, i), 0))
    sds = jax.ShapeDtypeStruct(view, F32)
    outs = pl.pallas_call(
        body, name=name,
        grid_spec=pltpu.PrefetchScalarGridSpec(
            num_scalar_prefetch=1, grid=(l_n, 2, r_n // tr),
            in_specs=[whole, whole, whole, mine_spec, theirs_spec], out_specs=[whole] * 4),
        out_shape=[sds] * 4, compiler_params=_params("parallel", "parallel", "parallel"),
    )(c_idx, w.reshape(view), m.reshape(view), v.reshape(view),
      mine.reshape(l_n, r_n, c_n), theirs.reshape(l_n, r_n, c_n))
    return tuple(t.reshape(w.shape) for t in outs)


def _allreduce_small(block):
    r_n, c_n = block.shape

    def body(in_ref, out_ref, slots, send, recv):
        x, y, c = lax.axis_index("x"), lax.axis_index("y"), lax.axis_index("c")
        me = 4 * x + 2 * y + c
        slots[me] = in_ref[...]
        flips = [(fx, fy, fc) for fx in (0, 1) for fy in (0, 1) for fc in (0, 1)][1:]
        peers = [(x ^ fx, y ^ fy, c ^ fc) for fx, fy, fc in flips]
        cps = [pltpu.make_async_remote_copy(src_ref=in_ref, dst_ref=slots.at[me], send_sem=send.at[j],
                                            recv_sem=recv.at[j], device_id=peers[j], device_id_type=MESH)
               for j in range(7)]
        for cp in cps:
            cp.start()
        for j, (px, py, pc) in enumerate(peers):
            slot = slots.at[4 * px + 2 * py + pc]
            pltpu.make_async_remote_copy(src_ref=slot, dst_ref=slot, send_sem=send.at[j], recv_sem=recv.at[j],
                                         device_id=(x, y, c), device_id_type=MESH).wait_recv()
        for cp in cps:
            cp.wait_send()
        total = slots[0]
        for d in range(1, 8):
            total = total + slots[d]
        out_ref[...] = total

    vmem = pl.BlockSpec(memory_space=pltpu.VMEM)
    return pl.pallas_call(
        body, name="allreduce_small", in_specs=[vmem], out_specs=vmem,
        out_shape=jax.ShapeDtypeStruct((r_n, c_n), F32),
        scratch_shapes=[pltpu.VMEM((8, r_n, c_n), F32), pltpu.SemaphoreType.DMA((7,)), pltpu.SemaphoreType.DMA((7,))],
    )(block)


def _first(accs, extras):
    return [accs[0]] if isinstance(accs, list) else [accs]


def _ffn_fwd(tag, x_in, h, wg, wu, wd):
    def act(accs, extras):
        a, b = accs
        return [a, b, a * _sigmoid(a) * b]

    a, b, s = _mm_nn(f"ffn_up_{tag}", h, [wg, wu], [], act, [BF16, BF16, BF16])
    x_out, = _mm_nn(f"ffn_down_{tag}", s, [wd], [x_in], lambda accs, ex: [ex[0] + accs[0]], [F32], tk=8192)
    return x_out, a, b, s


def _ffn_bwd(tag, layer, dx_out, h, a, b, s, wg, wu, wd, into):
    def mid(acc, extras):
        av, bv = extras[0].astype(F32), extras[1].astype(F32)
        sg = _sigmoid(av)
        return [acc * bv * _dsilu(av, sg), acc * (av * sg)]

    dxb = dx_out.astype(BF16)
    da, db = _mm_nt(f"ffn_dact_{tag}", [(dxb, wd)], [a, b], mid, [BF16, BF16], tm=1024)
    dh, = _mm_nt(f"ffn_dh_{tag}", [(da, wg), (db, wu)], [], _first, [F32], tm=1024, tr=2816)
    d_n, f_n = wg.shape[1], wg.shape[2]
    ns = f_n // N_CHIPS
    tki = _pick(d_n // 2, 512)
    ih = (d_n // 2) // tki
    col_shape = (N_CHIPS, 2, 2, d_n // 2, ns)
    col_block = (None, None, None, tki, ns)
    col_index = lambda g, i, j: (j, i // ih, layer, i % ih, 0)
    dwg = _mm_tn(f"ffn_dwg_{tag}", h, da, 1, col_shape, col_block, col_index, tki=tki, tn=ns, into=into[0])
    dwu = _mm_tn(f"ffn_dwu_{tag}", h, db, 1, col_shape, col_block, col_index, tki=tki, tn=ns, into=into[1])
    tn = _pick(d_n // 2, 512)
    jh = (d_n // 2) // tn
    dwd = _mm_tn(f"ffn_dwd_{tag}", s, dxb, 1, (N_CHIPS, 2, 2, ns, d_n // 2), (None, None, None, ns, tn),
                 lambda g, i, j: (i, j // jh, layer, 0, j % jh), tki=ns, tn=tn, into=into[2])
    return dh, (dwg, dwu, dwd)


def kernel(x, mix_norm, ffn_norm, final_norm, ab_w_in, lb_logits, hg_out_norm, ab_w_out, pool_w, pool_scale, ffn_w_gate, ffn_w_up, ffn_w_down, loss_target, m_mix_norm, m_ffn_norm, m_final_norm, m_ab_w_in, m_lb_logits, m_hg_out_norm, m_ab_w_out, m_pool_w, m_pool_scale, m_ffn_w_gate, m_ffn_w_up, m_ffn_w_down, v_mix_norm, v_ffn_norm, v_final_norm, v_ab_w_in, v_lb_logits, v_hg_out_norm, v_ab_w_out, v_pool_w, v_pool_scale, v_ffn_w_gate, v_ffn_w_up, v_ffn_w_down):
    xs, target = x[0], loss_target[0]
    s_n, d_n = xs.shape
    h_n = d_n // 2 // HEAD
    hw = h_n * HEAD
    n_grp = len(POOL_WINDOWS)
    grp = d_n // n_grp
    c_idx = lax.axis_index("c").astype(jnp.int32).reshape(1)
    chip = 2 * lax.axis_index("x") + lax.axis_index("y")

    units = [
        _Unit(ab_w_in.shape[1:], 1, 0),
        _Unit(ab_w_out.shape[1:], 0, 0),
        _Unit(pool_w.shape[1:], 1, 0),
        _Unit(ffn_w_gate.shape, 2, 1),
        _Unit(ffn_w_up.shape, 2, 1),
        _Unit(ffn_w_down.shape, 1, 2),
    ]
    chip_idx = chip.astype(jnp.int32).reshape(1)
    shards = [ab_w_in[0], ab_w_out[0], pool_w[0], ffn_w_gate, ffn_w_up, ffn_w_down]
    placed = [_place_shard(f"place_{n}", t, u, chip_idx) for n, (t, u) in enumerate(zip(shards, units))]
    w_in, w_out, scale_full = _gather_weights(units[:2], placed[:2], pool_scale)
    w_in3, w_out3 = w_in[None], w_out[None]
    row = lambda t: t.reshape(1, -1)

    h0 = _rms_fwd("norm_mix0", xs, row(mix_norm[0]))
    proj, projb = _mm_nn("proj_in", h0, [w_in3], [], lambda accs, ex: [accs[0], accs[0]], [F32, BF16])
    oraw, o_a, states = _hg_fwd("hgrn_fwd", proj, lb_logits, hg_out_norm, h_n)
    o_b, ltot, (w_pool, w_gate, w_up, w_down) = _att_fwd("attn_fwd", projb, h_n, units[2:], placed[2:])
    cat = jnp.concatenate([o_a, o_b], axis=1)
    x1, = _mm_nn("proj_out", cat, [w_out3], [xs], lambda accs, ex: [ex[0] + accs[0]], [F32])
    h1 = _rms_fwd("norm_ffn0", x1, row(ffn_norm[0]))
    x2, a0, b0, s0 = _ffn_fwd("l0", x1, h1, w_gate[0:1], w_up[0:1], w_down[0:1])
    pooled = _pool_fwd("pool_fwd", x2, row(mix_norm[1]))
    x3, mixed = _mm_nn("pool_mix", pooled, [w_pool], [x2, scale_full],
                       lambda accs, ex: [ex[0] + accs[0] * ex[1], accs[0]], [F32, F32], tk=grp, tn=grp)
    h3 = _rms_fwd("norm_ffn1", x3, row(ffn_norm[1]))
    x4, a1, b1, s1 = _ffn_fwd("l1", x3, h3, w_gate[1:2], w_up[1:2], w_down[1:2])

    dx4, d_final, loss = _loss_bwd("loss_bwd", x4, target, row(final_norm))
    dh3, ffn_grads = _ffn_bwd("l1", 1, dx4, h3, a1, b1, s1, w_gate[1:2], w_up[1:2], w_down[1:2], (None, None, None))
    dx3, d_ffn1 = _rms_bwd("norm_ffn1_bwd", dh3, x3, row(ffn_norm[1]), dx4)
    dmixed, d_scale = _scale_bwd("pool_scale_bwd", dx3, mixed, scale_full)
    dpooled, = _mm_nt("pool_dpooled", [(dmixed, w_pool)], [], _first, [F32], to=grp, tr=grp)
    slab_rows = grp // N_CHIPS
    d_pool = _mm_tn("pool_dw", pooled, dmixed, n_grp, (N_CHIPS, 2, n_grp // 2, slab_rows, grp),
                    (None, None, None, slab_rows, grp), lambda g, i, j: (i, g // 2, g % 2, 0, 0),
                    tki=slab_rows, tn=grp)
    dx2, d_mix1 = _pool_bwd("pool_bwd", dpooled, x2, row(mix_norm[1]), dx3)
    dh1, ffn_grads = _ffn_bwd("l0", 0, dx2, h1, a0, b0, s0, w_gate[0:1], w_up[0:1], w_down[0:1], ffn_grads)
    dx1, d_ffn0 = _rms_bwd("norm_ffn0_bwd", dh1, x1, row(ffn_norm[0]), dx2)
    dx1b = dx1.astype(BF16)
    dcat, = _mm_nt("proj_out_dcat", [(dx1b, w_out3)], [], _first, [F32], tm=1024)
    d_wout = _mm_tn("proj_out_dw", cat, dx1b, 1, (1, 2 * hw, d_n), (None, _pick(2 * hw, 512), _pick(d_n, 512)),
                    lambda g, i, j: (g, i, j), tki=_pick(2 * hw, 512), tn=_pick(d_n, 512))
    as4 = lambda g, u: g.reshape(N_CHIPS, 2, -1, u.half_shape[-1])
    early4 = [as4(g, u) for g, u in zip([d_wout, d_pool, *ffn_grads], units[1:])]
    early_sib = _to_sibling("grads_to_sibling", early4)
    early_part = [_add_mine(f"add_sibling_{n + 1}", g, r, c_idx) for n, (g, r) in enumerate(zip(early4, early_sib))]
    dqb, dkb, dvb, early_slots = _att_bwd("attn_bwd", projb, dcat, ltot, h_n, early_part)
    dqa, dfa, dia, dga, d_lb, d_hgn = _hg_bwd("hgrn_bwd", proj, dcat, oraw, states, lb_logits, hg_out_norm, h_n)
    dproj = jnp.concatenate([dqa, dfa, dia, dga, dqb, dkb, dvb], axis=1)
    ns_in = 7 * hw // N_CHIPS
    tki_in, tn_in = _pick(d_n // 2, 512), _pick(ns_in, 1792)
    ih_in, jps_in = (d_n // 2) // tki_in, ns_in // tn_in
    d_win = _mm_tn("proj_in_dw", h0, dproj, 1, (N_CHIPS, 2, d_n // 2, ns_in), (None, None, tki_in, tn_in),
                   lambda g, i, j: (j // jps_in, i // ih_in, i % ih_in, j % jps_in), tki=tki_in, tn=tn_in)
    win4 = as4(d_win, units[0])
    win_part = _add_mine("add_sibling_0", win4, _to_sibling("grad_in_to_sibling", [win4])[0], c_idx)
    dh0, win_slots = _mm_nt("proj_in_dh", [(dproj, w_in3)], [], _first, [F32], tm=1024, tr=3584,
                            exchange=[win_part])
    dx0, d_mix0 = _rms_bwd("norm_mix0_bwd", dh0, xs, row(mix_norm[0]), dx1)

    partials = [win_part, *early_part]
    slots = [win_slots, *early_slots]
    mine = [_add_slots(f"add_chips_{n}", p, s, chip_idx) for n, (p, s) in enumerate(zip(partials, slots))]
    theirs = _share_halves(mine)

    lanes = 2 * d_n
    pad = lambda t: jnp.pad(t.reshape(1, -1), ((0, 0), (0, lanes - t.size)))
    small = jnp.concatenate([
        pad(jnp.concatenate([d_mix0, d_mix1], axis=0)), pad(jnp.concatenate([d_ffn0, d_ffn1], axis=0)),
        pad(d_final), pad(d_lb), pad(d_hgn), pad(d_scale), jnp.zeros((2, lanes), F32)], axis=0)
    small = _allreduce_small(small)
    g_mix = small[0, :2 * d_n].reshape(2, d_n)
    g_ffn = small[1, :2 * d_n].reshape(2, d_n)
    g_final = small[2, :d_n]
    g_lb = small[3, :2 * hw].reshape(2, hw)
    g_hgn = small[4, :HEAD].reshape(1, HEAD)
    g_scale = lax.dynamic_slice(small[5, :d_n], (chip * grp,), (grp,)).reshape(1, grp)
    loss = lax.psum(loss[0, 0], ("x", "y", "c"))

    small_grads = {0: g_mix, 1: g_ffn, 2: g_final, 4: g_lb, 5: g_hgn, 8: g_scale}
    unit_of = {3: 0, 6: 1, 7: 2, 9: 3, 10: 4, 11: 5}
    weights = [mix_norm, ffn_norm, final_norm, ab_w_in, lb_logits, hg_out_norm, ab_w_out, pool_w, pool_scale,
               ffn_w_gate, ffn_w_up, ffn_w_down]
    ms = [m_mix_norm, m_ffn_norm, m_final_norm, m_ab_w_in, m_lb_logits, m_hg_out_norm, m_ab_w_out, m_pool_w,
          m_pool_scale, m_ffn_w_gate, m_ffn_w_up, m_ffn_w_down]
    vs = [v_mix_norm, v_ffn_norm, v_final_norm, v_ab_w_in, v_lb_logits, v_hg_out_norm, v_ab_w_out, v_pool_w,
          v_pool_scale, v_ffn_w_gate, v_ffn_w_up, v_ffn_w_down]
    grads, deltas, new_ms, new_vs = [], [], [], []
    for n, (w, m, v) in enumerate(zip(weights, ms, vs)):
        if n in unit_of:
            u = unit_of[n]
            g, d, nm, nv = _adamw_halves(f"adamw_{n}", units[u], w, m, v, mine[u], theirs[u], c_idx)
        else:
            w2 = w.reshape(1, -1) if w.ndim == 1 else w
            g = small_grads[n].reshape(w2.shape)
            d, nm, nv = _adamw_nd(f"adamw_{n}", w2, g, m.reshape(w2.shape), v.reshape(w2.shape))
        grads.append(g.reshape(w.shape))
        deltas.append(d.reshape(w.shape))
        new_ms.append(nm.reshape(w.shape))
        new_vs.append(nv.reshape(w.shape))
    return (loss, dx0[None], *grads, *deltas, *new_ms, *new_vs)
```

```python
import functools
import math

import jax
import jax.numpy as jnp
from jax import lax
from jax.experimental import pallas as pl
from jax.experimental.pallas import tpu as pltpu

F32 = jnp.float32
BF16 = jnp.bfloat16
HIGHEST = lax.Precision.HIGHEST
MESH = pl.DeviceIdType.MESH
ANY = pl.BlockSpec(memory_space=pl.ANY)

RMS_EPS = 1e-6
LOG2_E = 1.4426950408889634
HEAD = 128
HG_CHUNK = 64
HG_MID = HG_CHUNK // 2 - 1
HG_BLOCK = 512
HG_PAR = 8
ATT_BLOCK = 256
ATT_PAR_FWD = 4
ATT_PAR_BWD = 2
POOL_WINDOWS = (2, 4, 8, 16)
POOL_HALO = 16
N_CHIPS = 4
ADAM_LR, ADAM_B1, ADAM_B2, ADAM_EPS, ADAM_WD, ADAM_STEP = 0.001, 0.9, 0.999, 1e-08, 0.01, 10
VMEM_LIMIT = 56 * 1024 * 1024
MM_CHUNK_ROWS = 256

NT_DIMS = (((1,), (1,)), ((), ()))
TN_DIMS = (((0,), (0,)), ((), ()))


def _params(*sem):
    return pltpu.CompilerParams(dimension_semantics=sem, vmem_limit_bytes=VMEM_LIMIT)


def _pick(dim, pref, unit=128):
    best = None
    for t in range(unit, min(dim, pref) + 1, unit):
        if dim % t == 0:
            best = t
    return dim if best is None else best


def _row_chunks(tm, rows):
    n = 1
    while n < 4 and tm % (2 * n) == 0 and tm // (2 * n) >= rows:
        n *= 2
    return n


def _sigmoid(z):
    return 1.0 / (1.0 + jnp.exp(-z))


def _dsilu(a, sg):
    return sg * (1.0 + a * (1.0 - sg))


def _mm_nn(name, a, bs, extras, epilogue, out_dtypes, *, tm=1024, tn=512, tk=2048, weights_stay=False):
    g_n, k_n, n_n = bs[0].shape
    m_n = a.shape[0]
    tm, tn, tk = _pick(m_n, tm, 8), _pick(n_n, tn), _pick(k_n, tk)
    i_n, j_n, kt = m_n // tm, n_n // tn, k_n // tk
    nb, ne, no = len(bs), len(extras), len(out_dtypes)
    split = _row_chunks(tm, MM_CHUNK_ROWS)

    def body(*refs):
        a_ref, b_refs, e_refs = refs[0], refs[1:1 + nb], refs[1 + nb:1 + nb + ne]
        o_refs, acc_refs = refs[1 + nb + ne:1 + nb + ne + no], refs[1 + nb + ne + no:]
        k = pl.program_id(3)

        def finish(accs, rows=slice(None)):
            outs = epilogue(accs, [e[...] if e.shape[0] == 1 else e[rows, :] for e in e_refs])
            for o_ref, o in zip(o_refs, outs):
                o_ref[rows, :] = o.astype(o_ref.dtype)

        if kt == 1:
            chunks = [slice(s * (tm // split), (s + 1) * (tm // split)) for s in range(split)]
            prods = [[jnp.dot(a_ref[rows, :], b_ref[...], preferred_element_type=F32) for b_ref in b_refs]
                     for rows in chunks]
            for rows, p in zip(chunks, prods):
                finish(p, rows)
        else:
            av = a_ref[...]
            prods = [jnp.dot(av, b_ref[...], preferred_element_type=F32) for b_ref in b_refs]

            @pl.when(k == 0)
            def _():
                for acc, p in zip(acc_refs, prods):
                    acc[...] = p

            @pl.when(k > 0)
            def _():
                for acc, p in zip(acc_refs, prods):
                    acc[...] += p

            @pl.when(k == kt - 1)
            def _():
                finish([acc[...] for acc in acc_refs])

    at = (lambda f: lambda g, j, i, k: f(g, i, j, k)) if weights_stay else (lambda f: f)
    in_specs = [pl.BlockSpec((tm, tk), at(lambda g, i, j, k: (i, g * kt + k)))]
    in_specs += [pl.BlockSpec((None, tk, tn), at(lambda g, i, j, k: (g, k, j))) for _ in bs]
    for e in extras:
        if e.shape[0] == 1:
            in_specs.append(pl.BlockSpec((1, tn), at(lambda g, i, j, k: (0, g * j_n + j))))
        else:
            in_specs.append(pl.BlockSpec((tm, tn), at(lambda g, i, j, k: (i, g * j_n + j))))
    out_specs = [pl.BlockSpec((tm, tn), at(lambda g, i, j, k: (i, g * j_n + j))) for _ in out_dtypes]
    out_shape = [jax.ShapeDtypeStruct((m_n, g_n * n_n), dt) for dt in out_dtypes]
    scratch = [] if kt == 1 else [pltpu.VMEM((tm, tn), F32) for _ in bs]
    grid = (g_n, j_n, i_n, kt) if weights_stay else (g_n, i_n, j_n, kt)
    return pl.pallas_call(
        body, name=name, grid=grid, in_specs=in_specs, out_specs=out_specs, out_shape=out_shape,
        scratch_shapes=scratch, compiler_params=_params("parallel", "parallel", "parallel", "arbitrary"),
    )(a, *bs, *extras)


def _mm_nt(name, pairs, extras, epilogue, out_dtypes, *, tm=1024, to=512, tr=2048, exchange=(),
           weights_stay=False):
    g_n, kd, n_n = pairs[0][1].shape
    m_n = pairs[0][0].shape[0]
    tm, to, tr = _pick(m_n, tm, 8), _pick(kd, to), _pick(n_n, tr)
    i_n, j_n, rt = m_n // tm, kd // to, n_n // tr
    npairs, ne, no, nx = len(pairs), len(extras), len(out_dtypes), len(exchange)
    n_in = 2 * npairs + ne
    n_acc = 0 if rt == 1 else 1
    split = _row_chunks(tm, 2 * MM_CHUNK_ROWS)

    def body(*refs):
        ab_refs, e_refs = refs[:2 * npairs], refs[2 * npairs:n_in]
        o_refs = refs[n_in + nx:n_in + nx + no]
        acc_refs = refs[n_in + 2 * nx + no:n_in + 2 * nx + no + n_acc]
        if nx:
            plan = _OwnersPlan(refs[n_in:n_in + nx], refs[n_in + nx + no:n_in + 2 * nx + no],
                               refs[n_in + 2 * nx + no + n_acc:])
            step = ((pl.program_id(0) * grid[1] + pl.program_id(1)) * grid[2] + pl.program_id(2)) * grid[3] \
                + pl.program_id(3)
            pl.when(step == 0)(plan.start)
            pl.when(step == math.prod(grid) - 1)(plan.finish)
        r = pl.program_id(3)

        def product(rows):
            prod = None
            for p in range(npairs):
                t = lax.dot_general(ab_refs[2 * p][rows, :], ab_refs[2 * p + 1][...], NT_DIMS,
                                    preferred_element_type=F32)
                prod = t if prod is None else prod + t
            return prod

        def finish(acc, rows=slice(None)):
            outs = epilogue(acc, [e[rows, :] for e in e_refs])
            for o_ref, o in zip(o_refs, outs):
                o_ref[rows, :] = o.astype(o_ref.dtype)

        if rt == 1:
            chunks = [slice(s * (tm // split), (s + 1) * (tm // split)) for s in range(split)]
            prods = [product(rows) for rows in chunks]
            for rows, p in zip(chunks, prods):
                finish(p, rows)
        else:
            prod = product(slice(None))
            acc = acc_refs[0]

            @pl.when(r == 0)
            def _():
                acc[...] = prod

            @pl.when(r > 0)
            def _():
                acc[...] += prod

            @pl.when(r == rt - 1)
            def _():
                finish(acc[...])

    at = (lambda f: lambda g, j, i, r: f(g, i, j, r)) if weights_stay else (lambda f: f)
    in_specs, args = [], []
    for a, b in pairs:
        in_specs.append(pl.BlockSpec((tm, tr), at(lambda g, i, j, r: (i, g * rt + r))))
        in_specs.append(pl.BlockSpec((None, to, tr), at(lambda g, i, j, r: (g, j, r))))
        args += [a, b]
    in_specs += [pl.BlockSpec((tm, to), at(lambda g, i, j, r: (i, g * j_n + j))) for _ in extras]
    out_specs = [pl.BlockSpec((tm, to), at(lambda g, i, j, r: (i, g * j_n + j))) for _ in out_dtypes]
    out_shape = [jax.ShapeDtypeStruct((m_n, g_n * kd), dt) for dt in out_dtypes]
    scratch = [] if rt == 1 else [pltpu.VMEM((tm, to), F32)]
    grid = (g_n, j_n, i_n, rt) if weights_stay else (g_n, i_n, j_n, rt)
    if not nx:
        return pl.pallas_call(
            body, name=name, grid=grid, in_specs=in_specs, out_specs=out_specs, out_shape=out_shape,
            scratch_shapes=scratch, compiler_params=_params("parallel", "parallel", "parallel", "arbitrary"),
        )(*args, *extras)
    return pl.pallas_call(
        body, name=name, grid=grid, in_specs=in_specs + [ANY] * nx, out_specs=out_specs + [ANY] * nx,
        out_shape=out_shape + _OwnersPlan.out_shapes(exchange), scratch_shapes=scratch + _OwnersPlan.scratch(nx),
        compiler_params=_params("arbitrary", "arbitrary", "arbitrary", "arbitrary"),
    )(*args, *extras, *exchange)


def _mm_tn(name, a, b, g_n, out_shape, out_block, out_index, *, tki, tn, tm=2048, into=None):
    m_n = a.shape[0]
    k_n, n_n = a.shape[1] // g_n, b.shape[1] // g_n
    tm = _pick(m_n, tm, 8)
    i_n, j_n, mt = k_n // tki, n_n // tn, m_n // tm
    assert k_n % tki == 0 and n_n % tn == 0

    def body(*refs):
        a_ref, b_ref = refs[0], refs[1]
        o_ref, acc = refs[-2], refs[-1]
        m = pl.program_id(3)
        prod = lax.dot_general(a_ref[...], b_ref[...], TN_DIMS, preferred_element_type=F32)

        @pl.when(m == 0)
        def _():
            acc[...] = prod

        @pl.when(m > 0)
        def _():
            acc[...] += prod

        @pl.when(m == mt - 1)
        def _():
            o_ref[...] = acc[...].reshape(o_ref.shape)

    in_specs = [pl.BlockSpec((tm, tki), lambda g, i, j, m: (m, g * i_n + i)),
                pl.BlockSpec((tm, tn), lambda g, i, j, m: (m, g * j_n + j))]
    args = [a, b]
    aliases = {}
    if into is not None:
        in_specs.append(ANY)
        args.append(into)
        aliases = {2: 0}
    return pl.pallas_call(
        body, name=name, grid=(g_n, i_n, j_n, mt), in_specs=in_specs,
        out_specs=pl.BlockSpec(out_block, lambda g, i, j, m: out_index(g, i, j)),
        out_shape=jax.ShapeDtypeStruct(out_shape, F32), scratch_shapes=[pltpu.VMEM((tki, tn), F32)],
        input_output_aliases=aliases,
        compiler_params=_params("parallel", "parallel", "parallel", "arbitrary"),
    )(*args)


def _rstd(xv):
    return lax.rsqrt(jnp.mean(xv * xv, axis=-1, keepdims=True) + RMS_EPS)


def _rms_bwd_rows(dh, xv, gain, r):
    dy = dh * gain
    c = jnp.mean(dy * xv, axis=-1, keepdims=True)
    return r * dy - xv * (r * r * r) * c, dh * xv * r


def _fold8(t):
    return t.reshape(t.shape[0] // 8, 8, t.shape[1]).sum(axis=0)


def _rms_fwd(name, x, gain, tm=256):
    s_n, d_n = x.shape
    tm = _pick(s_n, tm, 8)

    def body(x_ref, g_ref, h_ref):
        xv = x_ref[...]
        h_ref[...] = (xv * _rstd(xv) * g_ref[...]).astype(h_ref.dtype)

    return pl.pallas_call(
        body, name=name, grid=(s_n // tm,),
        in_specs=[pl.BlockSpec((tm, d_n), lambda i: (i, 0)), pl.BlockSpec((1, d_n), lambda i: (0, 0))],
        out_specs=pl.BlockSpec((tm, d_n), lambda i: (i, 0)), out_shape=jax.ShapeDtypeStruct((s_n, d_n), BF16),
        compiler_params=_params("parallel"),
    )(x, gain)


def _rms_bwd(name, dh, x, gain, dres, tm=256):
    s_n, d_n = x.shape
    tm = _pick(s_n, tm, 8)
    nblk = s_n // tm

    def body(dh_ref, x_ref, g_ref, dres_ref, dx_ref, dg_ref, acc):
        i = pl.program_id(0)

        @pl.when(i == 0)
        def _():
            acc[...] = jnp.zeros_like(acc)

        xv = x_ref[...]
        dxv, dgt = _rms_bwd_rows(dh_ref[...].astype(F32), xv, g_ref[...], _rstd(xv))
        dx_ref[...] = dres_ref[...] + dxv
        acc[...] += _fold8(dgt)

        @pl.when(i == nblk - 1)
        def _():
            dg_ref[...] = jnp.sum(acc[...], axis=0, keepdims=True)

    row = pl.BlockSpec((tm, d_n), lambda i: (i, 0))
    vec = pl.BlockSpec((1, d_n), lambda i: (0, 0))
    return pl.pallas_call(
        body, name=name, grid=(nblk,), in_specs=[row, row, vec, row], out_specs=[row, vec],
        out_shape=[jax.ShapeDtypeStruct((s_n, d_n), F32), jax.ShapeDtypeStruct((1, d_n), F32)],
        scratch_shapes=[pltpu.VMEM((8, d_n), F32)], compiler_params=_params("arbitrary"),
    )(dh, x, gain, dres)


def _loss_bwd(name, x, target, gain, tm=256):
    s_n, d_n = x.shape
    tm = _pick(s_n, tm, 8)
    nblk = s_n // tm

    def body(x_ref, t_ref, g_ref, dx_ref, dg_ref, loss_ref, acc, lacc):
        i = pl.program_id(0)

        @pl.when(i == 0)
        def _():
            acc[...] = jnp.zeros_like(acc)
            lacc[...] = jnp.zeros_like(lacc)

        xv = x_ref[...]
        gain = g_ref[...]
        r = _rstd(xv)
        diff = xv * r * gain - t_ref[...]
        lacc[...] += _fold8(diff * diff)
        dxv, dgt = _rms_bwd_rows(diff * (1.0 / d_n), xv, gain, r)
        dx_ref[...] = dxv
        acc[...] += _fold8(dgt)

        @pl.when(i == nblk - 1)
        def _():
            dg_ref[...] = jnp.sum(acc[...], axis=0, keepdims=True)
            loss_ref[...] = jnp.sum(lacc[...], keepdims=True) * (0.5 / d_n)

    row = pl.BlockSpec((tm, d_n), lambda i: (i, 0))
    vec = pl.BlockSpec((1, d_n), lambda i: (0, 0))
    return pl.pallas_call(
        body, name=name, grid=(nblk,), in_specs=[row, row, vec],
        out_specs=[row, vec, pl.BlockSpec((1, 1), lambda i: (0, 0))],
        out_shape=[jax.ShapeDtypeStruct((s_n, d_n), F32), jax.ShapeDtypeStruct((1, d_n), F32),
                   jax.ShapeDtypeStruct((1, 1), F32)],
        scratch_shapes=[pltpu.VMEM((8, d_n), F32), pltpu.VMEM((8, d_n), F32)], compiler_params=_params("arbitrary"),
    )(x, target, gain)


def _pool_counts(t_idx, d_n):
    grp = d_n // len(POOL_WINDOWS)
    lane = lax.broadcasted_iota(jnp.int32, (1, d_n), 1) // grp
    win = jnp.zeros((1, d_n), jnp.int32)
    for gi, w in enumerate(POOL_WINDOWS):
        win = jnp.where(lane == gi, w, win)
    return jnp.minimum(t_idx + 1, win).astype(F32), lane


def _window_sums(rows, lane, backward):
    n = rows.shape[0]
    out = rows
    acc = rows
    width = 1
    for gi in range(len(POOL_WINDOWS)):
        shift = (n - width) if backward else width
        acc = acc + pltpu.roll(acc, shift, 0)
        width *= 2
        out = jnp.where(lane >= gi, acc, out)
    return out


def _pool_fwd(name, x, gain, tm=256):
    s_n, d_n = x.shape
    tm = _pick(s_n, tm, POOL_HALO)
    per = tm // POOL_HALO

    def body(x_ref, halo_ref, g_ref, o_ref):
        i = pl.program_id(0)
        halo = jnp.where(i == 0, 0.0, halo_ref[...])
        rows = jnp.concatenate([halo, x_ref[...]], axis=0)
        h = rows * _rstd(rows) * g_ref[...]
        t_idx = i * tm - POOL_HALO + lax.broadcasted_iota(jnp.int32, (tm + POOL_HALO, 1), 0)
        cnt, lane = _pool_counts(t_idx, d_n)
        pooled = _window_sums(h, lane, False) / cnt - h
        o_ref[...] = pooled[POOL_HALO:, :].astype(o_ref.dtype)

    return pl.pallas_call(
        body, name=name, grid=(s_n // tm,),
        in_specs=[pl.BlockSpec((tm, d_n), lambda i: (i, 0)),
                  pl.BlockSpec((POOL_HALO, d_n), lambda i: (jnp.maximum(i * per - 1, 0), 0)),
                  pl.BlockSpec((1, d_n), lambda i: (0, 0))],
        out_specs=pl.BlockSpec((tm, d_n), lambda i: (i, 0)), out_shape=jax.ShapeDtypeStruct((s_n, d_n), BF16),
        compiler_params=_params("parallel"),
    )(x, x, gain)


def _pool_bwd(name, dpooled, x, gain, dres, tm=256):
    s_n, d_n = x.shape
    tm = _pick(s_n, tm, POOL_HALO)
    per = tm // POOL_HALO
    nblk = s_n // tm
    last_halo = s_n // POOL_HALO - 1

    def body(dp_ref, halo_ref, x_ref, g_ref, dres_ref, dx_ref, dg_ref, acc):
        i = pl.program_id(0)

        @pl.when(i == 0)
        def _():
            acc[...] = jnp.zeros_like(acc)

        halo = jnp.where(i == nblk - 1, 0.0, halo_ref[...])
        rows = jnp.concatenate([dp_ref[...], halo], axis=0)
        t_idx = i * tm + lax.broadcasted_iota(jnp.int32, (tm + POOL_HALO, 1), 0)
        cnt, lane = _pool_counts(t_idx, d_n)
        dh = (_window_sums(rows / cnt, lane, True) - rows)[:tm, :]
        xv = x_ref[...]
        dxv, dgt = _rms_bwd_rows(dh, xv, g_ref[...], _rstd(xv))
        dx_ref[...] = dres_ref[...] + dxv
        acc[...] += _fold8(dgt)

        @pl.when(i == nblk - 1)
        def _():
            dg_ref[...] = jnp.sum(acc[...], axis=0, keepdims=True)

    row = pl.BlockSpec((tm, d_n), lambda i: (i, 0))
    vec = pl.BlockSpec((1, d_n), lambda i: (0, 0))
    return pl.pallas_call(
        body, name=name, grid=(nblk,),
        in_specs=[row, pl.BlockSpec((POOL_HALO, d_n), lambda i: (jnp.minimum((i + 1) * per, last_halo), 0)),
                  row, vec, row],
        out_specs=[row, vec],
        out_shape=[jax.ShapeDtypeStruct((s_n, d_n), F32), jax.ShapeDtypeStruct((1, d_n), F32)],
        scratch_shapes=[pltpu.VMEM((8, d_n), F32)], compiler_params=_params("arbitrary"),
    )(dpooled, dpooled, x, gain, dres)


def _scale_bwd(name, dx, mixed, scale, tm=256):
    s_n, d_n = dx.shape
    tm = _pick(s_n, tm, 8)
    nblk = s_n // tm

    def body(dx_ref, mx_ref, sc_ref, dm_ref, ds_ref, acc):
        i = pl.program_id(0)

        @pl.when(i == 0)
        def _():
            acc[...] = jnp.zeros_like(acc)

        dxv = dx_ref[...]
        dm_ref[...] = (dxv * sc_ref[...]).astype(dm_ref.dtype)
        acc[...] += _fold8(dxv * mx_ref[...])

        @pl.when(i == nblk - 1)
        def _():
            ds_ref[...] = jnp.sum(acc[...], axis=0, keepdims=True)

    row = pl.BlockSpec((tm, d_n), lambda i: (i, 0))
    vec = pl.BlockSpec((1, d_n), lambda i: (0, 0))
    return pl.pallas_call(
        body, name=name, grid=(nblk,), in_specs=[row, row, vec], out_specs=[row, vec],
        out_shape=[jax.ShapeDtypeStruct((s_n, d_n), BF16), jax.ShapeDtypeStruct((1, d_n), F32)],
        scratch_shapes=[pltpu.VMEM((8, d_n), F32)], compiler_params=_params("arbitrary"),
    )(dx, mixed, scale)


def _hg_gates(qa, fa, lb):
    sig = _sigmoid(fa)
    f = lb + (1.0 - lb) * sig
    sq = _sigmoid(qa)
    return sig, f, jnp.log(f), 1.0 - f, sq, qa * sq


def _hg_chunk_terms(q, k, g, lincl):
    gc = jnp.dot(lincl, g, precision=HIGHEST, preferred_element_type=F32)
    glast = gc[HG_CHUNK - 1:HG_CHUNK, :]
    gm = gc[HG_MID:HG_MID + 1, :]
    e_q, e_l = jnp.exp(gc), jnp.exp(glast - gc)
    e_m, e_mi = jnp.exp(gc - gm), jnp.exp(gm - gc)
    return glast, (e_q, e_l, e_m, e_mi), (q * e_q, k * e_l, q * e_m, k * e_mi)


def _hg_setup(s_n, n_heads):
    tb = _pick(s_n, HG_BLOCK, HG_CHUNK)
    par = HG_PAR if n_heads % HG_PAR == 0 else 1
    cols = [slice(p * HEAD, (p + 1) * HEAD) for p in range(par)]
    return tb, s_n // tb, tb // HG_CHUNK, par, cols


def _hg_fwd(name, proj, lb_logits, hgain, n_heads):
    s_n = proj.shape[0]
    h_n = n_heads
    tb, nblk, ncb, par, cols = _hg_setup(s_n, h_n)
    c_n = HG_CHUNK
    heads = range(par)

    def body(qa_ref, fa_ref, ia_ref, ga_ref, l_ref, gn_ref, oraw_ref, oa_ref, st_ref, state):
        @pl.when(pl.program_id(1) == 0)
        def _():
            state[...] = jnp.zeros_like(state)

        lv = l_ref[...]
        lbs = [_sigmoid(lv[0:1, c] - lv[1:2, c]) for c in cols]
        row = lax.broadcasted_iota(jnp.int32, (c_n, c_n), 0)
        col = lax.broadcasted_iota(jnp.int32, (c_n, c_n), 1)
        causal = col <= row
        lincl = causal.astype(F32)
        gn = gn_ref[...]

        def chunk(ci, carry):
            sl = pl.ds(pl.multiple_of(ci * c_n, c_n), c_n)
            gates = [_hg_gates(qa_ref[sl, cols[p]], fa_ref[sl, cols[p]], lbs[p]) for p in heads]
            terms = [_hg_chunk_terms(gates[p][5], gates[p][3], gates[p][2], lincl) for p in heads]
            vbs = [ia_ref[sl, cols[p]].astype(BF16) for p in heads]
            sts = [state[p] for p in heads]
            atts = [lax.dot_general(terms[p][2][2].astype(BF16), terms[p][2][3].astype(BF16), NT_DIMS,
                                    preferred_element_type=F32) for p in heads]
            inter = [lax.dot_general(terms[p][2][0].astype(BF16), sts[p].astype(BF16), NT_DIMS,
                                     preferred_element_type=F32) for p in heads]
            grown = [lax.dot_general(vbs[p], terms[p][2][1].astype(BF16), TN_DIMS, preferred_element_type=F32)
                     for p in heads]
            attb = [jnp.where(causal, atts[p], 0.0).astype(BF16) for p in heads]
            outs = [inter[p] + jnp.dot(attb[p], vbs[p], preferred_element_type=F32) for p in heads]
            for p in heads:
                st_ref[p, ci] = sts[p]
                state[p] = sts[p] * jnp.exp(terms[p][0]) + grown[p]
                o = outs[p]
                oraw_ref[sl, cols[p]] = o
                ga = ga_ref[sl, cols[p]]
                oa_ref[sl, cols[p]] = (o * _rstd(o) * gn * (ga * _sigmoid(ga))).astype(oa_ref.dtype)
            return carry

        lax.fori_loop(0, ncb, chunk, 0)

    wide = par * HEAD
    blk = lambda off: pl.BlockSpec((tb, wide), lambda h, c: (c, off // par + h))
    return pl.pallas_call(
        body, name=name, grid=(h_n // par, nblk),
        in_specs=[blk(0), blk(h_n), blk(2 * h_n), blk(3 * h_n),
                  pl.BlockSpec((2, wide), lambda h, c: (0, h)), pl.BlockSpec((1, HEAD), lambda h, c: (0, 0))],
        out_specs=[blk(0), blk(0), pl.BlockSpec((par, ncb, HEAD, HEAD), lambda h, c: (h, c, 0, 0))],
        out_shape=[jax.ShapeDtypeStruct((s_n, h_n * HEAD), F32), jax.ShapeDtypeStruct((s_n, h_n * HEAD), BF16),
                   jax.ShapeDtypeStruct((h_n, s_n // c_n, HEAD, HEAD), F32)],
        scratch_shapes=[pltpu.VMEM((par, HEAD, HEAD), F32)], compiler_params=_params("parallel", "arbitrary"),
    )(proj, proj, proj, proj, lb_logits, hgain)


def _hg_bwd(name, proj, dcat, oraw, states, lb_logits, hgain, n_heads):
    s_n = proj.shape[0]
    h_n = n_heads
    tb, nblk, ncb, par, cols = _hg_setup(s_n, h_n)
    c_n = HG_CHUNK
    n_steps = h_n // par

    def body(qa_ref, fa_ref, ia_ref, ga_ref, doa_ref, oraw_ref, st_ref, l_ref, gn_ref,
             dqa_ref, dfa_ref, dia_ref, dga_ref, dl_ref, dgn_ref, dstate, dlb_acc, dgn_acc):
        h, c = pl.program_id(0), pl.program_id(1)

        @pl.when(c == 0)
        def _():
            dstate[...] = jnp.zeros_like(dstate)
            dlb_acc[...] = jnp.zeros_like(dlb_acc)

        @pl.when((c == 0) & (h == 0))
        def _():
            dgn_acc[...] = jnp.zeros_like(dgn_acc)

        lv = l_ref[...]
        lbs = [_sigmoid(lv[0:1, cc] - lv[1:2, cc]) for cc in cols]
        row = lax.broadcasted_iota(jnp.int32, (c_n, c_n), 0)
        col = lax.broadcasted_iota(jnp.int32, (c_n, c_n), 1)
        causal = col <= row
        lincl = causal.astype(F32)
        uincl = (col >= row).astype(F32)
        is_last = lax.broadcasted_iota(jnp.int32, (c_n, 1), 0) == c_n - 1
        gn = gn_ref[...]

        def head_chunk(p, sl, ci):
            cc, lb = cols[p], lbs[p]
            qa = qa_ref[sl, cc]
            sig, f, g, k, sq, q = _hg_gates(qa, fa_ref[sl, cc], lb)
            glast, (e_q, e_l, e_m, e_mi), (qe, kl, qm, km) = _hg_chunk_terms(q, k, g, lincl)
            yield
            v = ia_ref[sl, cc]
            vb = v.astype(BF16)
            qmb, kmb, qeb, klb = qm.astype(BF16), km.astype(BF16), qe.astype(BF16), kl.astype(BF16)
            att = lax.dot_general(qmb, kmb, NT_DIMS, preferred_element_type=F32)

            o = oraw_ref[sl, cc]
            ga = ga_ref[sl, cc]
            sg = _sigmoid(ga)
            r = _rstd(o)
            doa = doa_ref[sl, cc]
            dn = doa * (ga * sg)
            dga_ref[sl, cc] = (doa * (o * r * gn) * _dsilu(ga, sg)).astype(dga_ref.dtype)
            yield
            attb = jnp.where(causal, att, 0.0).astype(BF16)
            do, dgt = _rms_bwd_rows(dn, o, gn, r)
            dgn_acc[...] += dgt
            dob = do.astype(BF16)

            st0 = st_ref[p, ci]
            ds1 = dstate[p]
            st0b, ds1b = st0.astype(BF16), ds1.astype(BF16)
            datt = lax.dot_general(dob, vb, NT_DIMS, preferred_element_type=F32)
            dv = lax.dot_general(attb, dob, TN_DIMS, preferred_element_type=F32)
            dv = dv + lax.dot_general(klb, ds1b, NT_DIMS, preferred_element_type=F32)
            dqe = jnp.dot(dob, st0b, preferred_element_type=F32)
            dkl = jnp.dot(vb, ds1b, preferred_element_type=F32)
            eg = jnp.exp(glast)
            dstate[p] = ds1 * eg + lax.dot_general(dob, qeb, TN_DIMS, preferred_element_type=F32)
            yield
            dattb = jnp.where(causal, datt, 0.0).astype(BF16)
            dqm = jnp.dot(dattb, kmb, preferred_element_type=F32)
            dkm = lax.dot_general(dattb, qmb, TN_DIMS, preferred_element_type=F32)
            dia_ref[sl, cc] = dv.astype(dia_ref.dtype)
            yield
            dq = dqm * e_m + dqe * e_q
            dk = dkm * e_mi + dkl * e_l
            dgc = dqm * qmb.astype(F32) - dkm * kmb.astype(F32) + dqe * qe - dkl * kl
            dglast = jnp.sum(dkl * kl, axis=0, keepdims=True) + eg * jnp.sum(ds1 * st0, axis=0, keepdims=True)
            dgc = dgc + jnp.where(is_last, dglast, 0.0)
            dg = jnp.dot(uincl, dgc, precision=HIGHEST, preferred_element_type=F32)
            dqa_ref[sl, cc] = (dq * _dsilu(qa, sq)).astype(dqa_ref.dtype)
            yield
            df = dg / f - dk
            dfa_ref[sl, cc] = (df * (1.0 - lb) * sig * (1.0 - sig)).astype(dfa_ref.dtype)
            dlb_acc[p] += df * (1.0 - sig)

        def chunk(idx, carry):
            ci = ncb - 1 - idx
            sl = pl.ds(pl.multiple_of(ci * c_n, c_n), c_n)
            running = [head_chunk(p, sl, ci) for p in range(par)]
            while running:
                running = [g for g in running if next(g, True) is None]
            return carry

        lax.fori_loop(0, ncb, chunk, 0)

        @pl.when(c == nblk - 1)
        def _():
            first = lax.broadcasted_iota(jnp.int32, (2, HEAD), 0) == 0
            for p, cc in enumerate(cols):
                dl0 = jnp.sum(dlb_acc[p], axis=0, keepdims=True) * lbs[p] * (1.0 - lbs[p])
                dl_ref[:, cc] = jnp.where(first, dl0, -dl0)

        @pl.when((c == nblk - 1) & (h == n_steps - 1))
        def _():
            dgn_ref[...] = jnp.sum(dgn_acc[...], axis=0, keepdims=True)

    wide = par * HEAD
    blk = lambda off: pl.BlockSpec((tb, wide), lambda h, c: (nblk - 1 - c, off // par + h))
    out_act = jax.ShapeDtypeStruct((s_n, h_n * HEAD), BF16)
    return pl.pallas_call(
        body, name=name, grid=(n_steps, nblk),
        in_specs=[blk(0), blk(h_n), blk(2 * h_n), blk(3 * h_n), blk(0), blk(0),
                  pl.BlockSpec((par, ncb, HEAD, HEAD), lambda h, c: (h, nblk - 1 - c, 0, 0)),
                  pl.BlockSpec((2, wide), lambda h, c: (0, h)), pl.BlockSpec((1, HEAD), lambda h, c: (0, 0))],
        out_specs=[blk(0), blk(0), blk(0), blk(0), pl.BlockSpec((2, wide), lambda h, c: (0, h)),
                   pl.BlockSpec((1, HEAD), lambda h, c: (0, 0))],
        out_shape=[out_act, out_act, out_act, out_act, jax.ShapeDtypeStruct((2, h_n * HEAD), F32),
                   jax.ShapeDtypeStruct((1, HEAD), F32)],
        scratch_shapes=[pltpu.VMEM((par, HEAD, HEAD), F32), pltpu.VMEM((par, c_n, HEAD), F32),
                        pltpu.VMEM((c_n, HEAD), F32)],
        compiler_params=_params("arbitrary", "arbitrary"),
    )(proj, proj, proj, proj, dcat, oraw, states, lb_logits, hgain)


def _split_dot(t, ones_b):
    return jnp.dot(t.astype(BF16), ones_b, preferred_element_type=F32)


def _softplus(z):
    return jnp.maximum(z, 0.0) + jnp.log(1.0 + jnp.exp2(jnp.abs(z) * (-LOG2_E)))


def _att_setup(projb, n_heads, want):
    s_n = projb.shape[0]
    t_n = _pick(s_n, ATT_BLOCK, 8)
    par = want if n_heads % want == 0 else 1
    cols = [slice(p * HEAD, (p + 1) * HEAD) for p in range(par)]
    wide = par * HEAD
    full = lambda off: pl.BlockSpec((s_n, wide), lambda h, i: (0, off // par + h))
    tile = lambda off: pl.BlockSpec((t_n, wide), lambda h, i: (i, off // par + h))
    return s_n, t_n, par, cols, full, tile


def _att_fwd(name, projb, n_heads, gather_units, gather_fulls):
    h_n = n_heads
    s_n, t_n, par, cols, full, tile_spec = _att_setup(projb, h_n, ATT_PAR_FWD)
    scale = 1.0 / math.sqrt(HEAD)
    ng = len(gather_fulls)
    h_steps, i_steps = h_n // par, s_n // t_n

    def body(*refs):
        q_ref, k_ref, v_ref = refs[:3]
        o_ref, lt_ref = refs[3 + ng:5 + ng]
        plan = _GatherPlan(gather_units, refs[5 + ng:5 + 2 * ng], refs[5 + 2 * ng:])
        h, i = pl.program_id(0), pl.program_id(1)
        pl.when((h == 0) & (i == 0))(plan.start)
        pl.when((h == h_steps - 1) & (i == (5 * i_steps) // 8))(plan.forward)
        pl.when((h == h_steps - 1) & (i == i_steps - 1))(plan.finish)
        row = lax.broadcasted_iota(jnp.int32, (t_n, t_n), 0)
        col = lax.broadcasted_iota(jnp.int32, (t_n, t_n), 1)
        from_here = (row >= col).astype(BF16)
        tri = col < row
        qs = [(q_ref[:, c].astype(F32) * scale).astype(BF16) for c in cols]

        def tile(j, carry, diagonal):
            sl = pl.ds(pl.multiple_of(j * t_n, t_n), t_n)
            zs = [lax.dot_general(qs[p], k_ref[sl, c], NT_DIMS, preferred_element_type=F32)
                  for p, c in enumerate(cols)]
            mid = []
            for p in range(par):
                z = zs[p]
                sp = _softplus(z)
                if diagonal:
                    sp = jnp.where(tri, sp, 0.0)
                mid.append((z - carry[p][1], _split_dot(sp, from_here)))
            out = []
            for p, c in enumerate(cols):
                zr, spent = mid[p]
                w = jnp.exp(zr - spent)
                if diagonal:
                    w = jnp.where(tri, w, 0.0)
                acc = carry[p][0] + jnp.dot(w.astype(BF16), v_ref[sl, c], preferred_element_type=F32)
                out.append((acc, carry[p][1] + spent[:, 0:1]))
            return tuple(out)

        init = tuple((jnp.zeros((t_n, HEAD), F32), jnp.zeros((t_n, 1), F32)) for _ in cols)
        carry = tile(i, init, True)
        carry = lax.fori_loop(0, i, lambda jj, cr: tile(i - 1 - jj, cr, False), carry)
        for p, c in enumerate(cols):
            o_ref[:, c] = carry[p][0].astype(o_ref.dtype)
            lt_ref[:, c] = jnp.broadcast_to(carry[p][1], (t_n, HEAD))

    outs = pl.pallas_call(
        body, name=name, grid=(h_steps, i_steps),
        in_specs=[tile_spec(4 * h_n), full(5 * h_n), full(6 * h_n)] + [ANY] * ng,
        out_specs=[tile_spec(0), tile_spec(0)] + [ANY] * ng,
        out_shape=[jax.ShapeDtypeStruct((s_n, h_n * HEAD), BF16), jax.ShapeDtypeStruct((s_n, h_n * HEAD), F32)]
        + [jax.ShapeDtypeStruct(f.shape, f.dtype) for f in gather_fulls],
        input_output_aliases={3 + u: 2 + u for u in range(ng)},
        scratch_shapes=_GatherPlan.scratch(ng), compiler_params=_params("arbitrary", "arbitrary"),
    )(projb, projb, projb, *gather_fulls)
    return outs[0], outs[1], outs[2:]


def _att_bwd(name, projb, dcat, spent_all, n_heads, partials):
    h_n = n_heads
    s_n, t_n, par, cols, full, tile_spec = _att_setup(projb, h_n, ATT_PAR_BWD)
    scale = 1.0 / math.sqrt(HEAD)
    npart = len(partials)
    h_steps, i_steps = h_n // par, s_n // t_n

    def body(*refs):
        q_ref, k_ref, v_ref, do_ref, lt_ref = refs[:5]
        dq_ref, dk_ref, dv_ref = refs[5 + npart:8 + npart]
        plan = _OwnersPlan(refs[5:5 + npart], refs[8 + npart:8 + 2 * npart], refs[8 + 2 * npart:10 + 2 * npart])
        dkt, dvt = refs[10 + 2 * npart:]
        h, i = pl.program_id(0), pl.program_id(1)
        pl.when((h == 0) & (i == 0))(plan.start)
        pl.when((h == h_steps - 1) & (i == i_steps - 1))(plan.finish)

        @pl.when(i == 0)
        def _():
            dkt[...] = jnp.zeros_like(dkt)
            dvt[...] = jnp.zeros_like(dvt)

        row = lax.broadcasted_iota(jnp.int32, (t_n, t_n), 0)
        col = lax.broadcasted_iota(jnp.int32, (t_n, t_n), 1)
        before = (row < col).astype(BF16)
        upto = (row <= col).astype(BF16)
        tri = col < row
        q32 = [q_ref[:, c].astype(F32) * scale for c in cols]
        qs = [t.astype(BF16) for t in q32]
        qts = [t.T.astype(BF16) for t in q32]
        do32 = [do_ref[:, c] for c in cols]
        dos = [t.astype(BF16) for t in do32]
        dots = [t.T.astype(BF16) for t in do32]
        last = slice(t_n - 1, t_n)

        def tile(j, carry, diagonal):
            sl = pl.ds(pl.multiple_of(j * t_n, t_n), t_n)
            zs = [lax.dot_general(qs[p], k_ref[sl, c], NT_DIMS, preferred_element_type=F32)
                  for p, c in enumerate(cols)]
            dws = [lax.dot_general(dos[p], v_ref[sl, c], NT_DIMS, preferred_element_type=F32)
                   for p, c in enumerate(cols)]
            mid1 = []
            for p in range(par):
                z = zs[p]
                sp = _softplus(z)
                sg = jnp.exp(z - sp)
                if diagonal:
                    sp = jnp.where(tri, sp, 0.0)
                prior = _split_dot(sp, before)
                mid1.append((z - carry[p][1], sg, prior, prior[:, last] + sp[:, last]))
            mid2 = []
            for p in range(par):
                zb, sg, prior, sp_sum = mid1[p]
                w = jnp.exp(zb + prior)
                if diagonal:
                    w = jnp.where(tri, w, 0.0)
                e = dws[p] * w
                mid2.append((w.astype(BF16), e, sg, _split_dot(e, upto), sp_sum))
            out = []
            for p, c in enumerate(cols):
                wb, e, sg, e_upto, sp_sum = mid2[p]
                dz = e - sg * (carry[p][2] + e_upto)
                if diagonal:
                    dz = jnp.where(tri, dz, 0.0)
                dz = dz.astype(BF16)
                dq = carry[p][0] + jnp.dot(dz, k_ref[sl, c], preferred_element_type=F32)
                dkt[p, j] += jnp.dot(qts[p], dz, preferred_element_type=F32)
                dvt[p, j] += jnp.dot(dots[p], wb, preferred_element_type=F32)
                out.append((dq, carry[p][1] - sp_sum, carry[p][2] + e_upto[:, last]))
            return tuple(out)

        init = tuple((jnp.zeros((t_n, HEAD), F32), lt_ref[:, c][:, 0:1], jnp.zeros((t_n, 1), F32)) for c in cols)
        carry = lax.fori_loop(0, i, lambda j, cr: tile(j, cr, False), init)
        carry = tile(i, carry, True)
        for p, c in enumerate(cols):
            dq_ref[:, c] = (carry[p][0] * scale).astype(dq_ref.dtype)

        @pl.when(i == i_steps - 1)
        def _():
            def put(j, _):
                sl = pl.ds(pl.multiple_of(j * t_n, t_n), t_n)
                for p, c in enumerate(cols):
                    dk_ref[sl, c] = dkt[p, j].T.astype(dk_ref.dtype)
                    dv_ref[sl, c] = dvt[p, j].T.astype(dv_ref.dtype)
                return 0

            lax.fori_loop(0, i_steps, put, 0)

    act = jax.ShapeDtypeStruct((s_n, h_n * HEAD), BF16)
    acc_t = pltpu.VMEM((par, i_steps, HEAD, t_n), F32)
    outs = pl.pallas_call(
        body, name=name, grid=(h_steps, i_steps),
        in_specs=[tile_spec(4 * h_n), full(5 * h_n), full(6 * h_n), tile_spec(h_n), tile_spec(0)] + [ANY] * npart,
        out_specs=[tile_spec(0), full(0), full(0)] + [ANY] * npart,
        out_shape=[act, act, act] + _OwnersPlan.out_shapes(partials),
        scratch_shapes=_OwnersPlan.scratch(npart) + [acc_t, acc_t],
        compiler_params=_params("arbitrary", "arbitrary"),
    )(projb, projb, projb, dcat, spent_all, *partials)
    return outs[0], outs[1], outs[2], outs[3:]


def _adamw(name, w, g, m, v, tr=256):
    r_n, c_n = w.shape
    tr = _pick(r_n, tr, 8)
    c1 = 1.0 - ADAM_B1 ** ADAM_STEP
    c2 = 1.0 - ADAM_B2 ** ADAM_STEP

    def body(w_ref, g_ref, m_ref, v_ref, d_ref, nm_ref, nv_ref):
        gv = g_ref[...]
        nm = ADAM_B1 * m_ref[...] + (1.0 - ADAM_B1) * gv
        nv = ADAM_B2 * v_ref[...] + (1.0 - ADAM_B2) * (gv * gv)
        d_ref[...] = -ADAM_LR * ((nm / c1) / (jnp.sqrt(nv / c2) + ADAM_EPS) + ADAM_WD * w_ref[...])
        nm_ref[...] = nm
        nv_ref[...] = nv

    blk = pl.BlockSpec((tr, c_n), lambda i: (i, 0))
    sds = jax.ShapeDtypeStruct((r_n, c_n), F32)
    return pl.pallas_call(
        body, name=name, grid=(r_n // tr,), in_specs=[blk] * 4, out_specs=[blk] * 3, out_shape=[sds] * 3,
        compiler_params=_params("parallel"),
    )(w, g, m, v)


def _adamw_nd(name, w, g, m, v):
    shape = w.shape
    flat = lambda t: t.reshape(-1, shape[-1])
    return tuple(t.reshape(shape) for t in _adamw(name, flat(w), flat(g.reshape(shape)), flat(m), flat(v)))


def _mesh_pos():
    x, y, c = lax.axis_index("x"), lax.axis_index("y"), lax.axis_index("c")
    chips = [(1 - x, y), (x, 1 - y), (1 - x, 1 - y)]
    return x, y, c, chips, 2 * x + y, [2 * cx + cy for cx, cy in chips]


class _Unit:
    def __init__(self, shard_shape, axis, half_axis):
        self.shard_shape = tuple(shard_shape)
        self.axis = axis
        self.half_axis = half_axis
        self.full_shape = tuple(n * N_CHIPS if a == axis else n for a, n in enumerate(shard_shape))
        self.half_shape = tuple(n // 2 if a == half_axis else n for a, n in enumerate(shard_shape))

    def _window(self, ref, k, c, with_slab):
        idx = []
        for a, n in enumerate(self.shard_shape):
            start, size = 0, n
            if a == self.half_axis:
                size = n // 2
                start = c * size
            if with_slab and a == self.axis:
                start = start + k * n
            idx.append(pl.ds(start, size))
        return ref.at[tuple(idx)]

    def full_half(self, ref, k, c):
        return self._window(ref, k, c, True)

    def place_view(self):
        s = self.shard_shape
        if self.axis == len(s) - 1:
            return math.prod(s[:-2]), s[-2], s[-1], True
        assert self.axis == len(s) - 2
        return math.prod(s[:self.axis]), s[self.axis], s[-1], False

    def half_view(self):
        s, h = self.shard_shape, self.half_axis
        if h == len(s) - 1:
            return math.prod(s[:-2]), s[-2], s[-1] // 2, True
        return math.prod(s[:h]), (s[h] // 2) * math.prod(s[h + 1:-1]), s[-1], False


def _place_shard(name, shard, unit, chip_idx):
    l_n, r_n, c_n, by_cols = unit.place_view()
    tr = _pick(r_n, 256, 16)
    per = r_n // tr

    def body(k_ref, s_ref, o_ref):
        o_ref[...] = s_ref[...].astype(o_ref.dtype)

    if by_cols:
        full3, out_index = (l_n, r_n, N_CHIPS * c_n), (lambda l, i, k_ref: (l, i, k_ref[0]))
    else:
        full3, out_index = (l_n, N_CHIPS * r_n, c_n), (lambda l, i, k_ref: (l, k_ref[0] * per + i, 0))
    out = pl.pallas_call(
        body, name=name,
        grid_spec=pltpu.PrefetchScalarGridSpec(
            num_scalar_prefetch=1, grid=(l_n, per),
            in_specs=[pl.BlockSpec((None, tr, c_n), lambda l, i, k_ref: (l, i, 0))],
            out_specs=pl.BlockSpec((None, tr, c_n), out_index)),
        out_shape=jax.ShapeDtypeStruct(full3, BF16), compiler_params=_params("parallel", "parallel"),
    )(chip_idx, shard.reshape(l_n, r_n, c_n))
    return out.reshape(unit.full_shape)


def _gather_weights(units, fulls, scale_shard):
    nu = len(units)
    ps = scale_shard.shape[1]

    def body(*refs):
        sc_in = refs[nu]
        outs, sc_out = refs[nu + 1:2 * nu + 1], refs[2 * nu + 1]
        send3, recv3, lsem = refs[2 * nu + 2:2 * nu + 5]
        plan = _GatherPlan(units, outs, refs[2 * nu + 5:])
        x, y, c, chips, me, others = _mesh_pos()
        local = pltpu.make_async_copy(sc_in, sc_out.at[:, pl.ds(me * ps, ps)], lsem.at[0])
        local.start()
        plan.start()
        sends = [pltpu.make_async_remote_copy(
            src_ref=sc_in, dst_ref=sc_out.at[:, pl.ds(me * ps, ps)], send_sem=send3.at[j], recv_sem=recv3.at[j],
            device_id=(*chip, c), device_id_type=MESH) for j, chip in enumerate(chips)]
        for cp in sends:
            cp.start()
        plan.forward()
        plan.finish()
        for j in range(3):
            dst = sc_out.at[:, pl.ds(others[j] * ps, ps)]
            pltpu.make_async_remote_copy(src_ref=dst, dst_ref=dst, send_sem=send3.at[j], recv_sem=recv3.at[j],
                                         device_id=(x, y, c), device_id_type=MESH).wait_recv()
        for cp in sends:
            cp.wait_send()
        local.wait()

    out_shape = [jax.ShapeDtypeStruct(f.shape, f.dtype) for f in fulls]
    out_shape.append(jax.ShapeDtypeStruct((1, N_CHIPS * ps), scale_shard.dtype))
    dma = pltpu.SemaphoreType.DMA
    return pl.pallas_call(
        body, name="gather_weights", in_specs=[ANY] * (nu + 1), out_specs=[ANY] * (nu + 1), out_shape=out_shape,
        input_output_aliases={u: u for u in range(nu)},
        scratch_shapes=[dma((3,)), dma((3,)), dma((1,))] + _GatherPlan.scratch(nu),
    )(*fulls, scale_shard)


class _GatherPlan:
    def __init__(self, units, outs, sems):
        self.units, self.outs = units, outs
        self.ici, self.d2d = (sems[0], sems[1]), (sems[2], sems[3])
        self.x, self.y, self.c, self.chips, self.me, self.others = _mesh_pos()
        self.pairs = [(u, j) for u in range(len(units)) for j in range(3)]

    @staticmethod
    def scratch(nu):
        return [pltpu.SemaphoreType.DMA((3 * nu,)) for _ in range(4)]

    def _copy(self, window, sems, u, j, to):
        return pltpu.make_async_remote_copy(src_ref=window, dst_ref=window, send_sem=sems[0].at[3 * u + j],
                                            recv_sem=sems[1].at[3 * u + j], device_id=to, device_id_type=MESH)

    def _half(self, u, chip, core):
        return self.units[u].full_half(self.outs[u], chip, core)

    def start(self):
        for u, j in self.pairs:
            self._copy(self._half(u, self.me, self.c), self.ici, u, j, (*self.chips[j], self.c)).start()

    def forward(self):
        here, sibling = (self.x, self.y, self.c), (self.x, self.y, 1 - self.c)
        for u, j in self.pairs:
            landed = self._half(u, self.others[j], self.c)
            self._copy(landed, self.ici, u, j, here).wait_recv()
            self._copy(landed, self.d2d, u, j, sibling).start()

    def finish(self):
        here = (self.x, self.y, self.c)
        for u, j in self.pairs:
            self._copy(self._half(u, self.others[j], 1 - self.c), self.d2d, u, j, here).wait_recv()
        for u, j in self.pairs:
            self._copy(self._half(u, self.me, self.c), self.ici, u, j, here).wait_send()
            self._copy(self._half(u, self.others[j], self.c), self.d2d, u, j, here).wait_send()


class _OwnersPlan:
    def __init__(self, ins, outs, sems):
        self.ins, self.outs, self.send, self.recv = ins, outs, sems[0], sems[1]
        _, _, self.c, self.chips, _, self.others = _mesh_pos()

    @staticmethod
    def scratch(nu):
        return [pltpu.SemaphoreType.DMA((3 * nu,)) for _ in range(2)]

    @staticmethod
    def out_shapes(partials):
        return [jax.ShapeDtypeStruct((3,) + p.shape[1:], p.dtype) for p in partials]

    def _copies(self):
        return [pltpu.make_async_remote_copy(
            src_ref=self.ins[u].at[self.others[j]], dst_ref=self.outs[u].at[j], send_sem=self.send.at[3 * u + j],
            recv_sem=self.recv.at[3 * u + j], device_id=(*self.chips[j], self.c), device_id_type=MESH)
            for u in range(len(self.ins)) for j in range(3)]

    def start(self):
        for cp in self._copies():
            cp.start()

    def finish(self):
        for cp in self._copies():
            cp.wait()


def _to_sibling(name, grads):
    nu = len(grads)

    def body(*refs):
        ins, outs = refs[:nu], refs[nu:2 * nu]
        send, recv = refs[2 * nu:]
        x, y, c, _, _, _ = _mesh_pos()
        cps = [pltpu.make_async_remote_copy(
            src_ref=ins[u].at[:, 1 - c], dst_ref=outs[u], send_sem=send.at[u], recv_sem=recv.at[u],
            device_id=(x, y, 1 - c), device_id_type=MESH) for u in range(nu)]
        for cp in cps:
            cp.start()
        for cp in cps:
            cp.wait()

    out_shape = [jax.ShapeDtypeStruct((g.shape[0],) + g.shape[2:], g.dtype) for g in grads]
    dma = pltpu.SemaphoreType.DMA
    return pl.pallas_call(
        body, name=name, in_specs=[ANY] * nu, out_specs=[ANY] * nu, out_shape=out_shape,
        scratch_shapes=[dma((nu,)), dma((nu,))],
    )(*grads)


def _share_halves(halves):
    nu = len(halves)

    def body(*refs):
        ins, outs = refs[:nu], refs[nu:2 * nu]
        send, recv = refs[2 * nu:]
        x, y, c, _, _, _ = _mesh_pos()
        cps = [pltpu.make_async_remote_copy(
            src_ref=ins[u], dst_ref=outs[u], send_sem=send.at[u], recv_sem=recv.at[u],
            device_id=(x, y, 1 - c), device_id_type=MESH) for u in range(nu)]
        for cp in cps:
            cp.start()
        for cp in cps:
            cp.wait()

    out_shape = [jax.ShapeDtypeStruct(h.shape, h.dtype) for h in halves]
    dma = pltpu.SemaphoreType.DMA
    return pl.pallas_call(
        body, name="share_halves", in_specs=[ANY] * nu, out_specs=[ANY] * nu, out_shape=out_shape,
        scratch_shapes=[dma((nu,)), dma((nu,))],
    )(*halves)


def _add_mine(name, grad, recv, c_idx):
    _, _, r_n, c_n = grad.shape
    tr = _pick(r_n, 256, 16)

    def body(c_ref, g_ref, r_ref, o_ref):
        o_ref[...] = (g_ref[...] + r_ref[...]).astype(o_ref.dtype)

    return pl.pallas_call(
        body, name=name,
        grid_spec=pltpu.PrefetchScalarGridSpec(
            num_scalar_prefetch=1, grid=(N_CHIPS, r_n // tr),
            in_specs=[pl.BlockSpec((None, None, tr, c_n), lambda k, i, c_ref: (k, c_ref[0], i, 0)),
                      pl.BlockSpec((None, tr, c_n), lambda k, i, c_ref: (k, i, 0))],
            out_specs=pl.BlockSpec((None, tr, c_n), lambda k, i, c_ref: (k, i, 0))),
        out_shape=jax.ShapeDtypeStruct(recv.shape, BF16), compiler_params=_params("parallel", "parallel"),
    )(c_idx, grad, recv)


def _add_slots(name, partial, slots, chip_idx):
    _, r_n, c_n = slots.shape
    tr = _pick(r_n, 256, 16)

    def body(k_ref, p_ref, s_ref, o_ref):
        own = p_ref[...].astype(F32)
        o_ref[...] = ((own + s_ref[0].astype(F32)) + s_ref[1].astype(F32)) + s_ref[2].astype(F32)

    return pl.pallas_call(
        body, name=name,
        grid_spec=pltpu.PrefetchScalarGridSpec(
            num_scalar_prefetch=1, grid=(r_n // tr,),
            in_specs=[pl.BlockSpec((None, tr, c_n), lambda i, k_ref: (k_ref[0], i, 0)),
                      pl.BlockSpec((3, tr, c_n), lambda i, k_ref: (0, i, 0))],
            out_specs=pl.BlockSpec((tr, c_n), lambda i, k_ref: (i, 0))),
        out_shape=jax.ShapeDtypeStruct((r_n, c_n), F32), compiler_params=_params("parallel"),
    )(chip_idx, partial, slots)


def _adamw_halves(name, unit, w, m, v, mine, theirs, c_idx, tr=256):
    l_n, r_n, c_n, by_cols = unit.half_view()
    tr = _pick(r_n, tr, 8)
    c1 = 1.0 - ADAM_B1 ** ADAM_STEP
    c2 = 1.0 - ADAM_B2 ** ADAM_STEP

    def body(c_ref, w_ref, m_ref, v_ref, mine_ref, theirs_ref, g_ref, d_ref, nm_ref, nv_ref):
        gv = jnp.where(pl.program_id(1) == c_ref[0], mine_ref[...], theirs_ref[...])
        nm = ADAM_B1 * m_ref[...] + (1.0 - ADAM_B1) * gv
        nv = ADAM_B2 * v_ref[...] + (1.0 - ADAM_B2) * (gv * gv)
        d_ref[...] = -ADAM_LR * ((nm / c1) / (jnp.sqrt(nv / c2) + ADAM_EPS) + ADAM_WD * w_ref[...])
        g_ref[...] = gv
        nm_ref[...] = nm
        nv_ref[...] = nv

    if by_cols:
        view = (l_n, r_n, 2 * c_n)
        whole = pl.BlockSpec((None, tr, c_n), lambda l, h, i, c_ref: (l, i, h))
    else:
        view = (l_n, 2, r_n, c_n)
        whole = pl.BlockSpec((None, None, tr, c_n), lambda l, h, i, c_ref: (l, h, i, 0))
    mine_spec = pl.BlockSpec((None, tr, c_n), lambda l, h, i, c_ref: (l, jnp.where(h == c_ref[0], i, 0), 0))
    theirs_spec = pl.BlockSpec((None, tr, c_n), lambda l, h, i, c_ref: (l, jnp.where(h == c_ref[0], 0, i), 0))
    sds = jax.ShapeDtypeStruct(view, F32)
    outs = pl.pallas_call(
        body, name=name,
        grid_spec=pltpu.PrefetchScalarGridSpec(
            num_scalar_prefetch=1, grid=(l_n, 2, r_n // tr),
            in_specs=[whole, whole, whole, mine_spec, theirs_spec], out_specs=[whole] * 4),
        out_shape=[sds] * 4, compiler_params=_params("parallel", "parallel", "parallel"),
    )(c_idx, w.reshape(view), m.reshape(view), v.reshape(view),
      mine.reshape(l_n, r_n, c_n), theirs.reshape(l_n, r_n, c_n))
    return tuple(t.reshape(w.shape) for t in outs)


def _allreduce_small(block):
    r_n, c_n = block.shape

    def body(in_ref, out_ref, slots, send, recv):
        x, y, c = lax.axis_index("x"), lax.axis_index("y"), lax.axis_index("c")
        me = 4 * x + 2 * y + c
        slots[me] = in_ref[...]
        flips = [(fx, fy, fc) for fx in (0, 1) for fy in (0, 1) for fc in (0, 1)][1:]
        peers = [(x ^ fx, y ^ fy, c ^ fc) for fx, fy, fc in flips]
        cps = [pltpu.make_async_remote_copy(src_ref=in_ref, dst_ref=slots.at[me], send_sem=send.at[j],
                                            recv_sem=recv.at[j], device_id=peers[j], device_id_type=MESH)
               for j in range(7)]
        for cp in cps:
            cp.start()
        for j, (px, py, pc) in enumerate(peers):
            slot = slots.at[4 * px + 2 * py + pc]
            pltpu.make_async_remote_copy(src_ref=slot, dst_ref=slot, send_sem=send.at[j], recv_sem=recv.at[j],
                                         device_id=(x, y, c), device_id_type=MESH).wait_recv()
        for cp in cps:
            cp.wait_send()
        total = slots[0]
        for d in range(1, 8):
            total = total + slots[d]
        out_ref[...] = total

    vmem = pl.BlockSpec(memory_space=pltpu.VMEM)
    return pl.pallas_call(
        body, name="allreduce_small", in_specs=[vmem], out_specs=vmem,
        out_shape=jax.ShapeDtypeStruct((r_n, c_n), F32),
        scratch_shapes=[pltpu.VMEM((8, r_n, c_n), F32), pltpu.SemaphoreType.DMA((7,)), pltpu.SemaphoreType.DMA((7,))],
    )(block)


def _first(accs, extras):
    return [accs[0]] if isinstance(accs, list) else [accs]


def _ffn_fwd(tag, x_in, h, wg, wu, wd):
    def act(accs, extras):
        a, b = accs
        return [a, b, a * _sigmoid(a) * b]

    a, b, s = _mm_nn(f"ffn_up_{tag}", h, [wg, wu], [], act, [BF16, BF16, BF16])
    x_out, = _mm_nn(f"ffn_down_{tag}", s, [wd], [x_in], lambda accs, ex: [ex[0] + accs[0]], [F32], tk=8192)
    return x_out, a, b, s


def _ffn_bwd(tag, layer, dx_out, h, a, b, s, wg, wu, wd, into):
    def mid(acc, extras):
        av, bv = extras[0].astype(F32), extras[1].astype(F32)
        sg = _sigmoid(av)
        return [acc * bv * _dsilu(av, sg), acc * (av * sg)]

    dxb = dx_out.astype(BF16)
    da, db = _mm_nt(f"ffn_dact_{tag}", [(dxb, wd)], [a, b], mid, [BF16, BF16], tm=512, to=wd.shape[1] // N_CHIPS,
                    weights_stay=True)
    dh, = _mm_nt(f"ffn_dh_{tag}", [(da, wg), (db, wu)], [], _first, [F32], tm=1024, tr=2816)
    d_n, f_n = wg.shape[1], wg.shape[2]
    ns = f_n // N_CHIPS
    tki = _pick(d_n // 2, 512)
    ih = (d_n // 2) // tki
    col_shape = (N_CHIPS, 2, 2, d_n // 2, ns)
    col_block = (None, None, None, tki, ns)
    col_index = lambda g, i, j: (j, i // ih, layer, i % ih, 0)
    dwg = _mm_tn(f"ffn_dwg_{tag}", h, da, 1, col_shape, col_block, col_index, tki=tki, tn=ns, into=into[0])
    dwu = _mm_tn(f"ffn_dwu_{tag}", h, db, 1, col_shape, col_block, col_index, tki=tki, tn=ns, into=into[1])
    tn = _pick(d_n // 2, 512)
    jh = (d_n // 2) // tn
    dwd = _mm_tn(f"ffn_dwd_{tag}", s, dxb, 1, (N_CHIPS, 2, 2, ns, d_n // 2), (None, None, None, ns, tn),
                 lambda g, i, j: (i, j // jh, layer, 0, j % jh), tki=ns, tn=tn, into=into[2])
    return dh, (dwg, dwu, dwd)


def kernel(x, mix_norm, ffn_norm, final_norm, ab_w_in, lb_logits, hg_out_norm, ab_w_out, pool_w, pool_scale, ffn_w_gate, ffn_w_up, ffn_w_down, loss_target, m_mix_norm, m_ffn_norm, m_final_norm, m_ab_w_in, m_lb_logits, m_hg_out_norm, m_ab_w_out, m_pool_w, m_pool_scale, m_ffn_w_gate, m_ffn_w_up, m_ffn_w_down, v_mix_norm, v_ffn_norm, v_final_norm, v_ab_w_in, v_lb_logits, v_hg_out_norm, v_ab_w_out, v_pool_w, v_pool_scale, v_ffn_w_gate, v_ffn_w_up, v_ffn_w_down):
    xs, target = x[0], loss_target[0]
    s_n, d_n = xs.shape
    h_n = d_n // 2 // HEAD
    hw = h_n * HEAD
    n_grp = len(POOL_WINDOWS)
    grp = d_n // n_grp
    c_idx = lax.axis_index("c").astype(jnp.int32).reshape(1)
    chip = 2 * lax.axis_index("x") + lax.axis_index("y")

    units = [
        _Unit(ab_w_in.shape[1:], 1, 0),
        _Unit(ab_w_out.shape[1:], 0, 0),
        _Unit(pool_w.shape[1:], 1, 0),
        _Unit(ffn_w_gate.shape, 2, 1),
        _Unit(ffn_w_up.shape, 2, 1),
        _Unit(ffn_w_down.shape, 1, 2),
    ]
    chip_idx = chip.astype(jnp.int32).reshape(1)
    shards = [ab_w_in[0], ab_w_out[0], pool_w[0], ffn_w_gate, ffn_w_up, ffn_w_down]
    placed = [_place_shard(f"place_{n}", t, u, chip_idx) for n, (t, u) in enumerate(zip(shards, units))]
    w_in, w_out, scale_full = _gather_weights(units[:2], placed[:2], pool_scale)
    w_in3, w_out3 = w_in[None], w_out[None]
    row = lambda t: t.reshape(1, -1)

    h0 = _rms_fwd("norm_mix0", xs, row(mix_norm[0]))
    proj, projb = _mm_nn("proj_in", h0, [w_in3], [], lambda accs, ex: [accs[0], accs[0]], [F32, BF16],
                         tm=512, tn=7 * hw // N_CHIPS, weights_stay=True)
    oraw, o_a, states = _hg_fwd("hgrn_fwd", proj, lb_logits, hg_out_norm, h_n)
    o_b, ltot, (w_pool, w_gate, w_up, w_down) = _att_fwd("attn_fwd", projb, h_n, units[2:], placed[2:])
    cat = jnp.concatenate([o_a, o_b], axis=1)
    x1, = _mm_nn("proj_out", cat, [w_out3], [xs], lambda accs, ex: [ex[0] + accs[0]], [F32])
    h1 = _rms_fwd("norm_ffn0", x1, row(ffn_norm[0]))
    x2, a0, b0, s0 = _ffn_fwd("l0", x1, h1, w_gate[0:1], w_up[0:1], w_down[0:1])
    pooled = _pool_fwd("pool_fwd", x2, row(mix_norm[1]))
    x3, mixed = _mm_nn("pool_mix", pooled, [w_pool], [x2, scale_full],
                       lambda accs, ex: [ex[0] + accs[0] * ex[1], accs[0]], [F32, F32], tk=grp, tn=grp)
    h3 = _rms_fwd("norm_ffn1", x3, row(ffn_norm[1]))
    x4, a1, b1, s1 = _ffn_fwd("l1", x3, h3, w_gate[1:2], w_up[1:2], w_down[1:2])

    dx4, d_final, loss = _loss_bwd("loss_bwd", x4, target, row(final_norm))
    dh3, ffn_grads = _ffn_bwd("l1", 1, dx4, h3, a1, b1, s1, w_gate[1:2], w_up[1:2], w_down[1:2], (None, None, None))
    dx3, d_ffn1 = _rms_bwd("norm_ffn1_bwd", dh3, x3, row(ffn_norm[1]), dx4)
    dmixed, d_scale = _scale_bwd("pool_scale_bwd", dx3, mixed, scale_full)
    dpooled, = _mm_nt("pool_dpooled", [(dmixed, w_pool)], [], _first, [F32], to=grp, tr=grp)
    slab_rows = grp // N_CHIPS
    d_pool = _mm_tn("pool_dw", pooled, dmixed, n_grp, (N_CHIPS, 2, n_grp // 2, slab_rows, grp),
                    (None, None, None, slab_rows, grp), lambda g, i, j: (i, g // 2, g % 2, 0, 0),
                    tki=slab_rows, tn=grp)
    dx2, d_mix1 = _pool_bwd("pool_bwd", dpooled, x2, row(mix_norm[1]), dx3)
    dh1, ffn_grads = _ffn_bwd("l0", 0, dx2, h1, a0, b0, s0, w_gate[0:1], w_up[0:1], w_down[0:1], ffn_grads)
    dx1, d_ffn0 = _rms_bwd("norm_ffn0_bwd", dh1, x1, row(ffn_norm[0]), dx2)
    dx1b = dx1.astype(BF16)
    dcat, = _mm_nt("proj_out_dcat", [(dx1b, w_out3)], [], _first, [F32], tm=1024)
    d_wout = _mm_tn("proj_out_dw", cat, dx1b, 1, (1, 2 * hw, d_n), (None, _pick(2 * hw, 512), _pick(d_n, 512)),
                    lambda g, i, j: (g, i, j), tki=_pick(2 * hw, 512), tn=_pick(d_n, 512))
    as4 = lambda g, u: g.reshape(N_CHIPS, 2, -1, u.half_shape[-1])
    early4 = [as4(g, u) for g, u in zip([d_wout, d_pool, *ffn_grads], units[1:])]
    early_sib = _to_sibling("grads_to_sibling", early4)
    early_part = [_add_mine(f"add_sibling_{n + 1}", g, r, c_idx) for n, (g, r) in enumerate(zip(early4, early_sib))]
    dqb, dkb, dvb, early_slots = _att_bwd("attn_bwd", projb, dcat, ltot, h_n, early_part)
    dqa, dfa, dia, dga, d_lb, d_hgn = _hg_bwd("hgrn_bwd", proj, dcat, oraw, states, lb_logits, hg_out_norm, h_n)
    dproj = jnp.concatenate([dqa, dfa, dia, dga, dqb, dkb, dvb], axis=1)
    ns_in = 7 * hw // N_CHIPS
    tki_in, tn_in = _pick(d_n // 2, 512), _pick(ns_in, 1792)
    ih_in, jps_in = (d_n // 2) // tki_in, ns_in // tn_in
    d_win = _mm_tn("proj_in_dw", h0, dproj, 1, (N_CHIPS, 2, d_n // 2, ns_in), (None, None, tki_in, tn_in),
                   lambda g, i, j: (j // jps_in, i // ih_in, i % ih_in, j % jps_in), tki=tki_in, tn=tn_in)
    win4 = as4(d_win, units[0])
    win_part = _add_mine("add_sibling_0", win4, _to_sibling("grad_in_to_sibling", [win4])[0], c_idx)
    dh0, win_slots = _mm_nt("proj_in_dh", [(dproj, w_in3)], [], _first, [F32], tm=1024, tr=3584,
                            exchange=[win_part])
    dx0, d_mix0 = _rms_bwd("norm_mix0_bwd", dh0, xs, row(mix_norm[0]), dx1)

    partials = [win_part, *early_part]
    slots = [win_slots, *early_slots]
    mine = [_add_slots(f"add_chips_{n}", p, s, chip_idx) for n, (p, s) in enumerate(zip(partials, slots))]
    theirs = _share_halves(mine)

    lanes = 2 * d_n
    pad = lambda t: jnp.pad(t.reshape(1, -1), ((0, 0), (0, lanes - t.size)))
    small = jnp.concatenate([
        pad(jnp.concatenate([d_mix0, d_mix1], axis=0)), pad(jnp.concatenate([d_ffn0, d_ffn1], axis=0)),
        pad(d_final), pad(d_lb), pad(d_hgn), pad(d_scale), jnp.zeros((2, lanes), F32)], axis=0)
    small = _allreduce_small(small)
    g_mix = small[0, :2 * d_n].reshape(2, d_n)
    g_ffn = small[1, :2 * d_n].reshape(2, d_n)
    g_final = small[2, :d_n]
    g_lb = small[3, :2 * hw].reshape(2, hw)
    g_hgn = small[4, :HEAD].reshape(1, HEAD)
    g_scale = lax.dynamic_slice(small[5, :d_n], (chip * grp,), (grp,)).reshape(1, grp)
    loss = lax.psum(loss[0, 0], ("x", "y", "c"))

    small_grads = {0: g_mix, 1: g_ffn, 2: g_final, 4: g_lb, 5: g_hgn, 8: g_scale}
    unit_of = {3: 0, 6: 1, 7: 2, 9: 3, 10: 4, 11: 5}
    weights = [mix_norm, ffn_norm, final_norm, ab_w_in, lb_logits, hg_out_norm, ab_w_out, pool_w, pool_scale,
               ffn_w_gate, ffn_w_up, ffn_w_down]
    ms = [m_mix_norm, m_ffn_norm, m_final_norm, m_ab_w_in, m_lb_logits, m_hg_out_norm, m_ab_w_out, m_pool_w,
          m_pool_scale, m_ffn_w_gate, m_ffn_w_up, m_ffn_w_down]
    vs = [v_mix_norm, v_ffn_norm, v_final_norm, v_ab_w_in, v_lb_logits, v_hg_out_norm, v_ab_w_out, v_pool_w,
          v_pool_scale, v_ffn_w_gate, v_ffn_w_up, v_ffn_w_down]
    grads, deltas, new_ms, new_vs = [], [], [], []
    for n, (w, m, v) in enumerate(zip(weights, ms, vs)):
        if n in unit_of:
            u = unit_of[n]
            g, d, nm, nv = _adamw_halves(f"adamw_{n}", units[u], w, m, v, mine[u], theirs[u], c_idx)
        else:
            w2 = w.reshape(1, -1) if w.ndim == 1 else w
            g = small_grads[n].reshape(w2.shape)
            d, nm, nv = _adamw_nd(f"adamw_{n}", w2, g, m.reshape(w2.shape), v.reshape(w2.shape))
        grads.append(g.reshape(w.shape))
        deltas.append(d.reshape(w.shape))
        new_ms.append(nm.reshape(w.shape))
        new_vs.append(nv.reshape(w.shape))
    return (loss, dx0[None], *grads, *deltas, *new_ms, *new_vs)
```

```python
import functools
import math

import jax
import jax.numpy as jnp
from jax import lax
from jax.experimental import pallas as pl
from jax.experimental.pallas import tpu as pltpu

F32 = jnp.float32
BF16 = jnp.bfloat16
HIGHEST = lax.Precision.HIGHEST
MESH = pl.DeviceIdType.MESH
ANY = pl.BlockSpec(memory_space=pl.ANY)

RMS_EPS = 1e-6
LOG2_E = 1.4426950408889634
HEAD = 128
HG_CHUNK = 64
HG_MID = HG_CHUNK // 2 - 1
HG_BLOCK = 512
HG_PAR = 8
ATT_BLOCK = 256
ATT_PAR_FWD = 4
ATT_PAR_BWD = 2
POOL_WINDOWS = (2, 4, 8, 16)
POOL_HALO = 16
N_CHIPS = 4
ADAM_LR, ADAM_B1, ADAM_B2, ADAM_EPS, ADAM_WD, ADAM_STEP = 0.001, 0.9, 0.999, 1e-08, 0.01, 10
VMEM_LIMIT = 56 * 1024 * 1024
MM_CHUNK_ROWS = 256

NT_DIMS = (((1,), (1,)), ((), ()))
TN_DIMS = (((0,), (0,)), ((), ()))


def _params(*sem):
    return pltpu.CompilerParams(dimension_semantics=sem, vmem_limit_bytes=VMEM_LIMIT)


def _pick(dim, pref, unit=128):
    best = None
    for t in range(unit, min(dim, pref) + 1, unit):
        if dim % t == 0:
            best = t
    return dim if best is None else best


def _row_chunks(tm, rows):
    n = 1
    while n < 4 and tm % (2 * n) == 0 and tm // (2 * n) >= rows:
        n *= 2
    return n


def _sigmoid(z):
    return 1.0 / (1.0 + jnp.exp(-z))


def _dsilu(a, sg):
    return sg * (1.0 + a * (1.0 - sg))


def _mm_nn(name, a, bs, extras, epilogue, out_dtypes, *, tm=1024, tn=512, tk=2048, weights_stay=False):
    g_n, k_n, n_n = bs[0].shape
    m_n = a.shape[0]
    tm, tn, tk = _pick(m_n, tm, 8), _pick(n_n, tn), _pick(k_n, tk)
    i_n, j_n, kt = m_n // tm, n_n // tn, k_n // tk
    nb, ne, no = len(bs), len(extras), len(out_dtypes)
    split = _row_chunks(tm, MM_CHUNK_ROWS)

    def body(*refs):
        a_ref, b_refs, e_refs = refs[0], refs[1:1 + nb], refs[1 + nb:1 + nb + ne]
        o_refs, acc_refs = refs[1 + nb + ne:1 + nb + ne + no], refs[1 + nb + ne + no:]
        k = pl.program_id(3)

        def finish(accs, rows=slice(None)):
            outs = epilogue(accs, [e[...] if e.shape[0] == 1 else e[rows, :] for e in e_refs])
            for o_ref, o in zip(o_refs, outs):
                o_ref[rows, :] = o.astype(o_ref.dtype)

        if kt == 1:
            chunks = [slice(s * (tm // split), (s + 1) * (tm // split)) for s in range(split)]
            prods = [[jnp.dot(a_ref[rows, :], b_ref[...], preferred_element_type=F32) for b_ref in b_refs]
                     for rows in chunks]
            for rows, p in zip(chunks, prods):
                finish(p, rows)
        else:
            av = a_ref[...]
            prods = [jnp.dot(av, b_ref[...], preferred_element_type=F32) for b_ref in b_refs]

            @pl.when(k == 0)
            def _():
                for acc, p in zip(acc_refs, prods):
                    acc[...] = p

            @pl.when(k > 0)
            def _():
                for acc, p in zip(acc_refs, prods):
                    acc[...] += p

            @pl.when(k == kt - 1)
            def _():
                finish([acc[...] for acc in acc_refs])

    at = (lambda f: lambda g, j, i, k: f(g, i, j, k)) if weights_stay else (lambda f: f)
    in_specs = [pl.BlockSpec((tm, tk), at(lambda g, i, j, k: (i, g * kt + k)))]
    in_specs += [pl.BlockSpec((None, tk, tn), at(lambda g, i, j, k: (g, k, j))) for _ in bs]
    for e in extras:
        if e.shape[0] == 1:
            in_specs.append(pl.BlockSpec((1, tn), at(lambda g, i, j, k: (0, g * j_n + j))))
        else:
            in_specs.append(pl.BlockSpec((tm, tn), at(lambda g, i, j, k: (i, g * j_n + j))))
    out_specs = [pl.BlockSpec((tm, tn), at(lambda g, i, j, k: (i, g * j_n + j))) for _ in out_dtypes]
    out_shape = [jax.ShapeDtypeStruct((m_n, g_n * n_n), dt) for dt in out_dtypes]
    scratch = [] if kt == 1 else [pltpu.VMEM((tm, tn), F32) for _ in bs]
    grid = (g_n, j_n, i_n, kt) if weights_stay else (g_n, i_n, j_n, kt)
    return pl.pallas_call(
        body, name=name, grid=grid, in_specs=in_specs, out_specs=out_specs, out_shape=out_shape,
        scratch_shapes=scratch, compiler_params=_params("parallel", "parallel", "parallel", "arbitrary"),
    )(a, *bs, *extras)


def _mm_nt(name, pairs, extras, epilogue, out_dtypes, *, tm=1024, to=512, tr=2048, exchange=(),
           weights_stay=False, plan_cls=None):
    plan_cls = plan_cls or _OwnersPlan
    g_n, kd, n_n = pairs[0][1].shape
    m_n = pairs[0][0].shape[0]
    tm, to, tr = _pick(m_n, tm, 8), _pick(kd, to), _pick(n_n, tr)
    i_n, j_n, rt = m_n // tm, kd // to, n_n // tr
    npairs, ne, no, nx = len(pairs), len(extras), len(out_dtypes), len(exchange)
    n_in = 2 * npairs + ne
    n_acc = 0 if rt == 1 else 1
    split = _row_chunks(tm, 2 * MM_CHUNK_ROWS)

    def body(*refs):
        ab_refs, e_refs = refs[:2 * npairs], refs[2 * npairs:n_in]
        o_refs = refs[n_in + nx:n_in + nx + no]
        acc_refs = refs[n_in + 2 * nx + no:n_in + 2 * nx + no + n_acc]
        if nx:
            plan = plan_cls(refs[n_in:n_in + nx], refs[n_in + nx + no:n_in + 2 * nx + no],
                            refs[n_in + 2 * nx + no + n_acc:])
            step = ((pl.program_id(0) * grid[1] + pl.program_id(1)) * grid[2] + pl.program_id(2)) * grid[3] \
                + pl.program_id(3)
            pl.when(step == 0)(plan.start)
            pl.when(step == math.prod(grid) - 1)(plan.finish)
        r = pl.program_id(3)

        def product(rows):
            prod = None
            for p in range(npairs):
                t = lax.dot_general(ab_refs[2 * p][rows, :], ab_refs[2 * p + 1][...], NT_DIMS,
                                    preferred_element_type=F32)
                prod = t if prod is None else prod + t
            return prod

        def finish(acc, rows=slice(None)):
            outs = epilogue(acc, [e[rows, :] for e in e_refs])
            for o_ref, o in zip(o_refs, outs):
                o_ref[rows, :] = o.astype(o_ref.dtype)

        if rt == 1:
            chunks = [slice(s * (tm // split), (s + 1) * (tm // split)) for s in range(split)]
            prods = [product(rows) for rows in chunks]
            for rows, p in zip(chunks, prods):
                finish(p, rows)
        else:
            prod = product(slice(None))
            acc = acc_refs[0]

            @pl.when(r == 0)
            def _():
                acc[...] = prod

            @pl.when(r > 0)
            def _():
                acc[...] += prod

            @pl.when(r == rt - 1)
            def _():
                finish(acc[...])

    at = (lambda f: lambda g, j, i, r: f(g, i, j, r)) if weights_stay else (lambda f: f)
    in_specs, args = [], []
    for a, b in pairs:
        in_specs.append(pl.BlockSpec((tm, tr), at(lambda g, i, j, r: (i, g * rt + r))))
        in_specs.append(pl.BlockSpec((None, to, tr), at(lambda g, i, j, r: (g, j, r))))
        args += [a, b]
    in_specs += [pl.BlockSpec((tm, to), at(lambda g, i, j, r: (i, g * j_n + j))) for _ in extras]
    out_specs = [pl.BlockSpec((tm, to), at(lambda g, i, j, r: (i, g * j_n + j))) for _ in out_dtypes]
    out_shape = [jax.ShapeDtypeStruct((m_n, g_n * kd), dt) for dt in out_dtypes]
    scratch = [] if rt == 1 else [pltpu.VMEM((tm, to), F32)]
    grid = (g_n, j_n, i_n, rt) if weights_stay else (g_n, i_n, j_n, rt)
    if not nx:
        return pl.pallas_call(
            body, name=name, grid=grid, in_specs=in_specs, out_specs=out_specs, out_shape=out_shape,
            scratch_shapes=scratch, compiler_params=_params("parallel", "parallel", "parallel", "arbitrary"),
        )(*args, *extras)
    return pl.pallas_call(
        body, name=name, grid=grid, in_specs=in_specs + [ANY] * nx, out_specs=out_specs + [ANY] * nx,
        out_shape=out_shape + plan_cls.out_shapes(exchange), scratch_shapes=scratch + plan_cls.scratch(nx),
        compiler_params=_params("arbitrary", "arbitrary", "arbitrary", "arbitrary"),
    )(*args, *extras, *exchange)


def _mm_tn(name, a, b, g_n, out_shape, out_block, out_index, *, tki, tn, tm=2048, into=None):
    m_n = a.shape[0]
    k_n, n_n = a.shape[1] // g_n, b.shape[1] // g_n
    tm = _pick(m_n, tm, 8)
    i_n, j_n, mt = k_n // tki, n_n // tn, m_n // tm
    assert k_n % tki == 0 and n_n % tn == 0

    def body(*refs):
        a_ref, b_ref = refs[0], refs[1]
        o_ref, acc = refs[-2], refs[-1]
        m = pl.program_id(3)
        prod = lax.dot_general(a_ref[...], b_ref[...], TN_DIMS, preferred_element_type=F32)

        @pl.when(m == 0)
        def _():
            acc[...] = prod

        @pl.when(m > 0)
        def _():
            acc[...] += prod

        @pl.when(m == mt - 1)
        def _():
            o_ref[...] = acc[...].reshape(o_ref.shape)

    in_specs = [pl.BlockSpec((tm, tki), lambda g, i, j, m: (m, g * i_n + i)),
                pl.BlockSpec((tm, tn), lambda g, i, j, m: (m, g * j_n + j))]
    args = [a, b]
    aliases = {}
    if into is not None:
        in_specs.append(ANY)
        args.append(into)
        aliases = {2: 0}
    return pl.pallas_call(
        body, name=name, grid=(g_n, i_n, j_n, mt), in_specs=in_specs,
        out_specs=pl.BlockSpec(out_block, lambda g, i, j, m: out_index(g, i, j)),
        out_shape=jax.ShapeDtypeStruct(out_shape, F32), scratch_shapes=[pltpu.VMEM((tki, tn), F32)],
        input_output_aliases=aliases,
        compiler_params=_params("parallel", "parallel", "parallel", "arbitrary"),
    )(*args)


def _rstd(xv):
    return lax.rsqrt(jnp.mean(xv * xv, axis=-1, keepdims=True) + RMS_EPS)


def _rms_bwd_rows(dh, xv, gain, r):
    dy = dh * gain
    c = jnp.mean(dy * xv, axis=-1, keepdims=True)
    return r * dy - xv * (r * r * r) * c, dh * xv * r


def _fold8(t):
    return t.reshape(t.shape[0] // 8, 8, t.shape[1]).sum(axis=0)


def _rms_fwd(name, x, gain, tm=256):
    s_n, d_n = x.shape
    tm = _pick(s_n, tm, 8)

    def body(x_ref, g_ref, h_ref):
        xv = x_ref[...]
        h_ref[...] = (xv * _rstd(xv) * g_ref[...]).astype(h_ref.dtype)

    return pl.pallas_call(
        body, name=name, grid=(s_n // tm,),
        in_specs=[pl.BlockSpec((tm, d_n), lambda i: (i, 0)), pl.BlockSpec((1, d_n), lambda i: (0, 0))],
        out_specs=pl.BlockSpec((tm, d_n), lambda i: (i, 0)), out_shape=jax.ShapeDtypeStruct((s_n, d_n), BF16),
        compiler_params=_params("parallel"),
    )(x, gain)


def _rms_bwd(name, dh, x, gain, dres, tm=256):
    s_n, d_n = x.shape
    tm = _pick(s_n, tm, 8)
    nblk = s_n // tm

    def body(dh_ref, x_ref, g_ref, dres_ref, dx_ref, dg_ref, acc):
        i = pl.program_id(0)

        @pl.when(i == 0)
        def _():
            acc[...] = jnp.zeros_like(acc)

        xv = x_ref[...]
        dxv, dgt = _rms_bwd_rows(dh_ref[...].astype(F32), xv, g_ref[...], _rstd(xv))
        dx_ref[...] = dres_ref[...] + dxv
        acc[...] += _fold8(dgt)

        @pl.when(i == nblk - 1)
        def _():
            dg_ref[...] = jnp.sum(acc[...], axis=0, keepdims=True)

    row = pl.BlockSpec((tm, d_n), lambda i: (i, 0))
    vec = pl.BlockSpec((1, d_n), lambda i: (0, 0))
    return pl.pallas_call(
        body, name=name, grid=(nblk,), in_specs=[row, row, vec, row], out_specs=[row, vec],
        out_shape=[jax.ShapeDtypeStruct((s_n, d_n), F32), jax.ShapeDtypeStruct((1, d_n), F32)],
        scratch_shapes=[pltpu.VMEM((8, d_n), F32)], compiler_params=_params("arbitrary"),
    )(dh, x, gain, dres)


def _loss_bwd(name, x, target, gain, tm=256):
    s_n, d_n = x.shape
    tm = _pick(s_n, tm, 8)
    nblk = s_n // tm

    def body(x_ref, t_ref, g_ref, dx_ref, dg_ref, loss_ref, acc, lacc):
        i = pl.program_id(0)

        @pl.when(i == 0)
        def _():
            acc[...] = jnp.zeros_like(acc)
            lacc[...] = jnp.zeros_like(lacc)

        xv = x_ref[...]
        gain = g_ref[...]
        r = _rstd(xv)
        diff = xv * r * gain - t_ref[...]
        lacc[...] += _fold8(diff * diff)
        dxv, dgt = _rms_bwd_rows(diff * (1.0 / d_n), xv, gain, r)
        dx_ref[...] = dxv
        acc[...] += _fold8(dgt)

        @pl.when(i == nblk - 1)
        def _():
            dg_ref[...] = jnp.sum(acc[...], axis=0, keepdims=True)
            loss_ref[...] = jnp.sum(lacc[...], keepdims=True) * (0.5 / d_n)

    row = pl.BlockSpec((tm, d_n), lambda i: (i, 0))
    vec = pl.BlockSpec((1, d_n), lambda i: (0, 0))
    return pl.pallas_call(
        body, name=name, grid=(nblk,), in_specs=[row, row, vec],
        out_specs=[row, vec, pl.BlockSpec((1, 1), lambda i: (0, 0))],
        out_shape=[jax.ShapeDtypeStruct((s_n, d_n), F32), jax.ShapeDtypeStruct((1, d_n), F32),
                   jax.ShapeDtypeStruct((1, 1), F32)],
        scratch_shapes=[pltpu.VMEM((8, d_n), F32), pltpu.VMEM((8, d_n), F32)], compiler_params=_params("arbitrary"),
    )(x, target, gain)


def _pool_counts(t_idx, d_n):
    grp = d_n // len(POOL_WINDOWS)
    lane = lax.broadcasted_iota(jnp.int32, (1, d_n), 1) // grp
    win = jnp.zeros((1, d_n), jnp.int32)
    for gi, w in enumerate(POOL_WINDOWS):
        win = jnp.where(lane == gi, w, win)
    return jnp.minimum(t_idx + 1, win).astype(F32), lane


def _window_sums(rows, lane, backward):
    n = rows.shape[0]
    out = rows
    acc = rows
    width = 1
    for gi in range(len(POOL_WINDOWS)):
        shift = (n - width) if backward else width
        acc = acc + pltpu.roll(acc, shift, 0)
        width *= 2
        out = jnp.where(lane >= gi, acc, out)
    return out


def _pool_fwd(name, x, gain, tm=256):
    s_n, d_n = x.shape
    tm = _pick(s_n, tm, POOL_HALO)
    per = tm // POOL_HALO

    def body(x_ref, halo_ref, g_ref, o_ref):
        i = pl.program_id(0)
        halo = jnp.where(i == 0, 0.0, halo_ref[...])
        rows = jnp.concatenate([halo, x_ref[...]], axis=0)
        h = rows * _rstd(rows) * g_ref[...]
        t_idx = i * tm - POOL_HALO + lax.broadcasted_iota(jnp.int32, (tm + POOL_HALO, 1), 0)
        cnt, lane = _pool_counts(t_idx, d_n)
        pooled = _window_sums(h, lane, False) / cnt - h
        o_ref[...] = pooled[POOL_HALO:, :].astype(o_ref.dtype)

    return pl.pallas_call(
        body, name=name, grid=(s_n // tm,),
        in_specs=[pl.BlockSpec((tm, d_n), lambda i: (i, 0)),
                  pl.BlockSpec((POOL_HALO, d_n), lambda i: (jnp.maximum(i * per - 1, 0), 0)),
                  pl.BlockSpec((1, d_n), lambda i: (0, 0))],
        out_specs=pl.BlockSpec((tm, d_n), lambda i: (i, 0)), out_shape=jax.ShapeDtypeStruct((s_n, d_n), BF16),
        compiler_params=_params("parallel"),
    )(x, x, gain)


def _pool_bwd(name, dpooled, x, gain, dres, tm=256):
    s_n, d_n = x.shape
    tm = _pick(s_n, tm, POOL_HALO)
    per = tm // POOL_HALO
    nblk = s_n // tm
    last_halo = s_n // POOL_HALO - 1

    def body(dp_ref, halo_ref, x_ref, g_ref, dres_ref, dx_ref, dg_ref, acc):
        i = pl.program_id(0)

        @pl.when(i == 0)
        def _():
            acc[...] = jnp.zeros_like(acc)

        halo = jnp.where(i == nblk - 1, 0.0, halo_ref[...])
        rows = jnp.concatenate([dp_ref[...], halo], axis=0)
        t_idx = i * tm + lax.broadcasted_iota(jnp.int32, (tm + POOL_HALO, 1), 0)
        cnt, lane = _pool_counts(t_idx, d_n)
        dh = (_window_sums(rows / cnt, lane, True) - rows)[:tm, :]
        xv = x_ref[...]
        dxv, dgt = _rms_bwd_rows(dh, xv, g_ref[...], _rstd(xv))
        dx_ref[...] = dres_ref[...] + dxv
        acc[...] += _fold8(dgt)

        @pl.when(i == nblk - 1)
        def _():
            dg_ref[...] = jnp.sum(acc[...], axis=0, keepdims=True)

    row = pl.BlockSpec((tm, d_n), lambda i: (i, 0))
    vec = pl.BlockSpec((1, d_n), lambda i: (0, 0))
    return pl.pallas_call(
        body, name=name, grid=(nblk,),
        in_specs=[row, pl.BlockSpec((POOL_HALO, d_n), lambda i: (jnp.minimum((i + 1) * per, last_halo), 0)),
                  row, vec, row],
        out_specs=[row, vec],
        out_shape=[jax.ShapeDtypeStruct((s_n, d_n), F32), jax.ShapeDtypeStruct((1, d_n), F32)],
        scratch_shapes=[pltpu.VMEM((8, d_n), F32)], compiler_params=_params("arbitrary"),
    )(dpooled, dpooled, x, gain, dres)


def _scale_bwd(name, dx, mixed, scale, tm=256):
    s_n, d_n = dx.shape
    tm = _pick(s_n, tm, 8)
    nblk = s_n // tm

    def body(dx_ref, mx_ref, sc_ref, dm_ref, ds_ref, acc):
        i = pl.program_id(0)

        @pl.when(i == 0)
        def _():
            acc[...] = jnp.zeros_like(acc)

        dxv = dx_ref[...]
        dm_ref[...] = (dxv * sc_ref[...]).astype(dm_ref.dtype)
        acc[...] += _fold8(dxv * mx_ref[...])

        @pl.when(i == nblk - 1)
        def _():
            ds_ref[...] = jnp.sum(acc[...], axis=0, keepdims=True)

    row = pl.BlockSpec((tm, d_n), lambda i: (i, 0))
    vec = pl.BlockSpec((1, d_n), lambda i: (0, 0))
    return pl.pallas_call(
        body, name=name, grid=(nblk,), in_specs=[row, row, vec], out_specs=[row, vec],
        out_shape=[jax.ShapeDtypeStruct((s_n, d_n), BF16), jax.ShapeDtypeStruct((1, d_n), F32)],
        scratch_shapes=[pltpu.VMEM((8, d_n), F32)], compiler_params=_params("arbitrary"),
    )(dx, mixed, scale)


def _hg_gates(qa, fa, lb):
    sig = _sigmoid(fa)
    f = lb + (1.0 - lb) * sig
    sq = _sigmoid(qa)
    return sig, f, jnp.log(f), 1.0 - f, sq, qa * sq


def _hg_chunk_terms(q, k, g, lincl):
    gc = jnp.dot(lincl, g, precision=HIGHEST, preferred_element_type=F32)
    glast = gc[HG_CHUNK - 1:HG_CHUNK, :]
    gm = gc[HG_MID:HG_MID + 1, :]
    e_q, e_l = jnp.exp(gc), jnp.exp(glast - gc)
    e_m, e_mi = jnp.exp(gc - gm), jnp.exp(gm - gc)
    return glast, (e_q, e_l, e_m, e_mi), (q * e_q, k * e_l, q * e_m, k * e_mi)


def _hg_setup(s_n, n_heads):
    tb = _pick(s_n, HG_BLOCK, HG_CHUNK)
    par = HG_PAR if n_heads % HG_PAR == 0 else 1
    cols = [slice(p * HEAD, (p + 1) * HEAD) for p in range(par)]
    return tb, s_n // tb, tb // HG_CHUNK, par, cols


def _hg_fwd(name, proj, lb_logits, hgain, n_heads):
    s_n = proj.shape[0]
    h_n = n_heads
    tb, nblk, ncb, par, cols = _hg_setup(s_n, h_n)
    c_n = HG_CHUNK
    heads = range(par)

    def body(qa_ref, fa_ref, ia_ref, ga_ref, l_ref, gn_ref, oraw_ref, oa_ref, st_ref, state):
        @pl.when(pl.program_id(1) == 0)
        def _():
            state[...] = jnp.zeros_like(state)

        lv = l_ref[...]
        lbs = [_sigmoid(lv[0:1, c] - lv[1:2, c]) for c in cols]
        row = lax.broadcasted_iota(jnp.int32, (c_n, c_n), 0)
        col = lax.broadcasted_iota(jnp.int32, (c_n, c_n), 1)
        causal = col <= row
        lincl = causal.astype(F32)
        gn = gn_ref[...]

        def chunk(ci, carry):
            sl = pl.ds(pl.multiple_of(ci * c_n, c_n), c_n)
            gates = [_hg_gates(qa_ref[sl, cols[p]], fa_ref[sl, cols[p]], lbs[p]) for p in heads]
            terms = [_hg_chunk_terms(gates[p][5], gates[p][3], gates[p][2], lincl) for p in heads]
            vbs = [ia_ref[sl, cols[p]].astype(BF16) for p in heads]
            sts = [state[p] for p in heads]
            atts = [lax.dot_general(terms[p][2][2].astype(BF16), terms[p][2][3].astype(BF16), NT_DIMS,
                                    preferred_element_type=F32) for p in heads]
            inter = [lax.dot_general(terms[p][2][0].astype(BF16), sts[p].astype(BF16), NT_DIMS,
                                     preferred_element_type=F32) for p in heads]
            grown = [lax.dot_general(vbs[p], terms[p][2][1].astype(BF16), TN_DIMS, preferred_element_type=F32)
                     for p in heads]
            attb = [jnp.where(causal, atts[p], 0.0).astype(BF16) for p in heads]
            outs = [inter[p] + jnp.dot(attb[p], vbs[p], preferred_element_type=F32) for p in heads]
            for p in heads:
                st_ref[p, ci] = sts[p]
                state[p] = sts[p] * jnp.exp(terms[p][0]) + grown[p]
                o = outs[p]
                oraw_ref[sl, cols[p]] = o
                ga = ga_ref[sl, cols[p]]
                oa_ref[sl, cols[p]] = (o * _rstd(o) * gn * (ga * _sigmoid(ga))).astype(oa_ref.dtype)
            return carry

        lax.fori_loop(0, ncb, chunk, 0)

    wide = par * HEAD
    blk = lambda off: pl.BlockSpec((tb, wide), lambda h, c: (c, off // par + h))
    return pl.pallas_call(
        body, name=name, grid=(h_n // par, nblk),
        in_specs=[blk(0), blk(h_n), blk(2 * h_n), blk(3 * h_n),
                  pl.BlockSpec((2, wide), lambda h, c: (0, h)), pl.BlockSpec((1, HEAD), lambda h, c: (0, 0))],
        out_specs=[blk(0), blk(0), pl.BlockSpec((par, ncb, HEAD, HEAD), lambda h, c: (h, c, 0, 0))],
        out_shape=[jax.ShapeDtypeStruct((s_n, h_n * HEAD), F32), jax.ShapeDtypeStruct((s_n, h_n * HEAD), BF16),
                   jax.ShapeDtypeStruct((h_n, s_n // c_n, HEAD, HEAD), F32)],
        scratch_shapes=[pltpu.VMEM((par, HEAD, HEAD), F32)], compiler_params=_params("parallel", "arbitrary"),
    )(proj, proj, proj, proj, lb_logits, hgain)


def _hg_bwd(name, proj, dcat, oraw, states, lb_logits, hgain, n_heads):
    s_n = proj.shape[0]
    h_n = n_heads
    tb, nblk, ncb, par, cols = _hg_setup(s_n, h_n)
    c_n = HG_CHUNK
    n_steps = h_n // par

    def body(qa_ref, fa_ref, ia_ref, ga_ref, doa_ref, oraw_ref, st_ref, l_ref, gn_ref,
             dqa_ref, dfa_ref, dia_ref, dga_ref, dl_ref, dgn_ref, dstate, dlb_acc, dgn_acc):
        h, c = pl.program_id(0), pl.program_id(1)

        @pl.when(c == 0)
        def _():
            dstate[...] = jnp.zeros_like(dstate)
            dlb_acc[...] = jnp.zeros_like(dlb_acc)

        @pl.when((c == 0) & (h == 0))
        def _():
            dgn_acc[...] = jnp.zeros_like(dgn_acc)

        lv = l_ref[...]
        lbs = [_sigmoid(lv[0:1, cc] - lv[1:2, cc]) for cc in cols]
        row = lax.broadcasted_iota(jnp.int32, (c_n, c_n), 0)
        col = lax.broadcasted_iota(jnp.int32, (c_n, c_n), 1)
        causal = col <= row
        lincl = causal.astype(F32)
        uincl = (col >= row).astype(F32)
        is_last = lax.broadcasted_iota(jnp.int32, (c_n, 1), 0) == c_n - 1
        gn = gn_ref[...]

        def head_chunk(p, sl, ci):
            cc, lb = cols[p], lbs[p]
            qa = qa_ref[sl, cc]
            sig, f, g, k, sq, q = _hg_gates(qa, fa_ref[sl, cc], lb)
            glast, (e_q, e_l, e_m, e_mi), (qe, kl, qm, km) = _hg_chunk_terms(q, k, g, lincl)
            yield
            v = ia_ref[sl, cc]
            vb = v.astype(BF16)
            qmb, kmb, qeb, klb = qm.astype(BF16), km.astype(BF16), qe.astype(BF16), kl.astype(BF16)
            att = lax.dot_general(qmb, kmb, NT_DIMS, preferred_element_type=F32)

            o = oraw_ref[sl, cc]
            ga = ga_ref[sl, cc]
            sg = _sigmoid(ga)
            r = _rstd(o)
            doa = doa_ref[sl, cc]
            dn = doa * (ga * sg)
            dga_ref[sl, cc] = (doa * (o * r * gn) * _dsilu(ga, sg)).astype(dga_ref.dtype)
            yield
            attb = jnp.where(causal, att, 0.0).astype(BF16)
            do, dgt = _rms_bwd_rows(dn, o, gn, r)
            dgn_acc[...] += dgt
            dob = do.astype(BF16)

            st0 = st_ref[p, ci]
            ds1 = dstate[p]
            st0b, ds1b = st0.astype(BF16), ds1.astype(BF16)
            datt = lax.dot_general(dob, vb, NT_DIMS, preferred_element_type=F32)
            dv = lax.dot_general(attb, dob, TN_DIMS, preferred_element_type=F32)
            dv = dv + lax.dot_general(klb, ds1b, NT_DIMS, preferred_element_type=F32)
            dqe = jnp.dot(dob, st0b, preferred_element_type=F32)
            dkl = jnp.dot(vb, ds1b, preferred_element_type=F32)
            eg = jnp.exp(glast)
            dstate[p] = ds1 * eg + lax.dot_general(dob, qeb, TN_DIMS, preferred_element_type=F32)
            yield
            dattb = jnp.where(causal, datt, 0.0).astype(BF16)
            dqm = jnp.dot(dattb, kmb, preferred_element_type=F32)
            dkm = lax.dot_general(dattb, qmb, TN_DIMS, preferred_element_type=F32)
            dia_ref[sl, cc] = dv.astype(dia_ref.dtype)
            yield
            dq = dqm * e_m + dqe * e_q
            dk = dkm * e_mi + dkl * e_l
            dgc = dqm * qmb.astype(F32) - dkm * kmb.astype(F32) + dqe * qe - dkl * kl
            dglast = jnp.sum(dkl * kl, axis=0, keepdims=True) + eg * jnp.sum(ds1 * st0, axis=0, keepdims=True)
            dgc = dgc + jnp.where(is_last, dglast, 0.0)
            dg = jnp.dot(uincl, dgc, precision=HIGHEST, preferred_element_type=F32)
            dqa_ref[sl, cc] = (dq * _dsilu(qa, sq)).astype(dqa_ref.dtype)
            yield
            df = dg / f - dk
            dfa_ref[sl, cc] = (df * (1.0 - lb) * sig * (1.0 - sig)).astype(dfa_ref.dtype)
            dlb_acc[p] += df * (1.0 - sig)

        def chunk(idx, carry):
            ci = ncb - 1 - idx
            sl = pl.ds(pl.multiple_of(ci * c_n, c_n), c_n)
            running = [head_chunk(p, sl, ci) for p in range(par)]
            while running:
                running = [g for g in running if next(g, True) is None]
            return carry

        lax.fori_loop(0, ncb, chunk, 0)

        @pl.when(c == nblk - 1)
        def _():
            first = lax.broadcasted_iota(jnp.int32, (2, HEAD), 0) == 0
            for p, cc in enumerate(cols):
                dl0 = jnp.sum(dlb_acc[p], axis=0, keepdims=True) * lbs[p] * (1.0 - lbs[p])
                dl_ref[:, cc] = jnp.where(first, dl0, -dl0)

        @pl.when((c == nblk - 1) & (h == n_steps - 1))
        def _():
            dgn_ref[...] = jnp.sum(dgn_acc[...], axis=0, keepdims=True)

    wide = par * HEAD
    blk = lambda off: pl.BlockSpec((tb, wide), lambda h, c: (nblk - 1 - c, off // par + h))
    out_act = jax.ShapeDtypeStruct((s_n, h_n * HEAD), BF16)
    return pl.pallas_call(
        body, name=name, grid=(n_steps, nblk),
        in_specs=[blk(0), blk(h_n), blk(2 * h_n), blk(3 * h_n), blk(0), blk(0),
                  pl.BlockSpec((par, ncb, HEAD, HEAD), lambda h, c: (h, nblk - 1 - c, 0, 0)),
                  pl.BlockSpec((2, wide), lambda h, c: (0, h)), pl.BlockSpec((1, HEAD), lambda h, c: (0, 0))],
        out_specs=[blk(0), blk(0), blk(0), blk(0), pl.BlockSpec((2, wide), lambda h, c: (0, h)),
                   pl.BlockSpec((1, HEAD), lambda h, c: (0, 0))],
        out_shape=[out_act, out_act, out_act, out_act, jax.ShapeDtypeStruct((2, h_n * HEAD), F32),
                   jax.ShapeDtypeStruct((1, HEAD), F32)],
        scratch_shapes=[pltpu.VMEM((par, HEAD, HEAD), F32), pltpu.VMEM((par, c_n, HEAD), F32),
                        pltpu.VMEM((c_n, HEAD), F32)],
        compiler_params=_params("arbitrary", "arbitrary"),
    )(proj, proj, proj, proj, dcat, oraw, states, lb_logits, hgain)


def _split_dot(t, ones_b):
    return jnp.dot(t.astype(BF16), ones_b, preferred_element_type=F32)


def _softplus(z):
    return jnp.maximum(z, 0.0) + jnp.log(1.0 + jnp.exp2(jnp.abs(z) * (-LOG2_E)))


def _att_setup(projb, n_heads, want):
    s_n = projb.shape[0]
    t_n = _pick(s_n, ATT_BLOCK, 8)
    par = want if n_heads % want == 0 else 1
    cols = [slice(p * HEAD, (p + 1) * HEAD) for p in range(par)]
    wide = par * HEAD
    full = lambda off: pl.BlockSpec((s_n, wide), lambda h, i: (0, off // par + h))
    tile = lambda off: pl.BlockSpec((t_n, wide), lambda h, i: (i, off // par + h))
    return s_n, t_n, par, cols, full, tile


def _att_fwd(name, projb, n_heads, gather_units, gather_fulls):
    h_n = n_heads
    s_n, t_n, par, cols, full, tile_spec = _att_setup(projb, h_n, ATT_PAR_FWD)
    scale = 1.0 / math.sqrt(HEAD)
    ng = len(gather_fulls)
    h_steps, i_steps = h_n // par, s_n // t_n

    def body(*refs):
        q_ref, k_ref, v_ref = refs[:3]
        o_ref, lt_ref = refs[3 + ng:5 + ng]
        plan = _GatherPlan(gather_units, refs[5 + ng:5 + 2 * ng], refs[5 + 2 * ng:])
        h, i = pl.program_id(0), pl.program_id(1)
        pl.when((h == 0) & (i == 0))(plan.start)
        pl.when((h == h_steps - 1) & (i == (5 * i_steps) // 8))(plan.forward)
        pl.when((h == h_steps - 1) & (i == i_steps - 1))(plan.finish)
        row = lax.broadcasted_iota(jnp.int32, (t_n, t_n), 0)
        col = lax.broadcasted_iota(jnp.int32, (t_n, t_n), 1)
        from_here = (row >= col).astype(BF16)
        tri = col < row
        qs = [(q_ref[:, c].astype(F32) * scale).astype(BF16) for c in cols]

        def tile(j, carry, diagonal):
            sl = pl.ds(pl.multiple_of(j * t_n, t_n), t_n)
            zs = [lax.dot_general(qs[p], k_ref[sl, c], NT_DIMS, preferred_element_type=F32)
                  for p, c in enumerate(cols)]
            mid = []
            for p in range(par):
                z = zs[p]
                sp = _softplus(z)
                if diagonal:
                    sp = jnp.where(tri, sp, 0.0)
                mid.append((z - carry[p][1], _split_dot(sp, from_here)))
            out = []
            for p, c in enumerate(cols):
                zr, spent = mid[p]
                w = jnp.exp(zr - spent)
                if diagonal:
                    w = jnp.where(tri, w, 0.0)
                acc = carry[p][0] + jnp.dot(w.astype(BF16), v_ref[sl, c], preferred_element_type=F32)
                out.append((acc, carry[p][1] + spent[:, 0:1]))
            return tuple(out)

        init = tuple((jnp.zeros((t_n, HEAD), F32), jnp.zeros((t_n, 1), F32)) for _ in cols)
        carry = tile(i, init, True)
        carry = lax.fori_loop(0, i, lambda jj, cr: tile(i - 1 - jj, cr, False), carry)
        for p, c in enumerate(cols):
            o_ref[:, c] = carry[p][0].astype(o_ref.dtype)
            lt_ref[:, c] = jnp.broadcast_to(carry[p][1], (t_n, HEAD))

    outs = pl.pallas_call(
        body, name=name, grid=(h_steps, i_steps),
        in_specs=[tile_spec(4 * h_n), full(5 * h_n), full(6 * h_n)] + [ANY] * ng,
        out_specs=[tile_spec(0), tile_spec(0)] + [ANY] * ng,
        out_shape=[jax.ShapeDtypeStruct((s_n, h_n * HEAD), BF16), jax.ShapeDtypeStruct((s_n, h_n * HEAD), F32)]
        + [jax.ShapeDtypeStruct(f.shape, f.dtype) for f in gather_fulls],
        input_output_aliases={3 + u: 2 + u for u in range(ng)},
        scratch_shapes=_GatherPlan.scratch(ng), compiler_params=_params("arbitrary", "arbitrary"),
    )(projb, projb, projb, *gather_fulls)
    return outs[0], outs[1], outs[2:]


def _att_bwd(name, projb, dcat, spent_all, n_heads, partials):
    h_n = n_heads
    s_n, t_n, par, cols, full, tile_spec = _att_setup(projb, h_n, ATT_PAR_BWD)
    scale = 1.0 / math.sqrt(HEAD)
    npart = len(partials)
    h_steps, i_steps = h_n // par, s_n // t_n

    def body(*refs):
        q_ref, k_ref, v_ref, do_ref, lt_ref = refs[:5]
        dq_ref, dk_ref, dv_ref = refs[5 + npart:8 + npart]
        plan = _OwnersPlan(refs[5:5 + npart], refs[8 + npart:8 + 2 * npart], refs[8 + 2 * npart:10 + 2 * npart])
        dkt, dvt = refs[10 + 2 * npart:]
        h, i = pl.program_id(0), pl.program_id(1)
        pl.when((h == 0) & (i == 0))(plan.start)
        pl.when((h == h_steps - 1) & (i == i_steps - 1))(plan.finish)

        @pl.when(i == 0)
        def _():
            dkt[...] = jnp.zeros_like(dkt)
            dvt[...] = jnp.zeros_like(dvt)

        row = lax.broadcasted_iota(jnp.int32, (t_n, t_n), 0)
        col = lax.broadcasted_iota(jnp.int32, (t_n, t_n), 1)
        before = (row < col).astype(BF16)
        upto = (row <= col).astype(BF16)
        tri = col < row
        q32 = [q_ref[:, c].astype(F32) * scale for c in cols]
        qs = [t.astype(BF16) for t in q32]
        qts = [t.T.astype(BF16) for t in q32]
        do32 = [do_ref[:, c] for c in cols]
        dos = [t.astype(BF16) for t in do32]
        dots = [t.T.astype(BF16) for t in do32]
        last = slice(t_n - 1, t_n)

        def tile(j, carry, diagonal):
            sl = pl.ds(pl.multiple_of(j * t_n, t_n), t_n)
            zs = [lax.dot_general(qs[p], k_ref[sl, c], NT_DIMS, preferred_element_type=F32)
                  for p, c in enumerate(cols)]
            dws = [lax.dot_general(dos[p], v_ref[sl, c], NT_DIMS, preferred_element_type=F32)
                   for p, c in enumerate(cols)]
            mid1 = []
            for p in range(par):
                z = zs[p]
                sp = _softplus(z)
                sg = jnp.exp(z - sp)
                if diagonal:
                    sp = jnp.where(tri, sp, 0.0)
                prior = _split_dot(sp, before)
                mid1.append((z - carry[p][1], sg, prior, prior[:, last] + sp[:, last]))
            mid2 = []
            for p in range(par):
                zb, sg, prior, sp_sum = mid1[p]
                w = jnp.exp(zb + prior)
                if diagonal:
                    w = jnp.where(tri, w, 0.0)
                e = dws[p] * w
                mid2.append((w.astype(BF16), e, sg, _split_dot(e, upto), sp_sum))
            out = []
            for p, c in enumerate(cols):
                wb, e, sg, e_upto, sp_sum = mid2[p]
                dz = e - sg * (carry[p][2] + e_upto)
                if diagonal:
                    dz = jnp.where(tri, dz, 0.0)
                dz = dz.astype(BF16)
                dq = carry[p][0] + jnp.dot(dz, k_ref[sl, c], preferred_element_type=F32)
                dkt[p, j] += jnp.dot(qts[p], dz, preferred_element_type=F32)
                dvt[p, j] += jnp.dot(dots[p], wb, preferred_element_type=F32)
                out.append((dq, carry[p][1] - sp_sum, carry[p][2] + e_upto[:, last]))
            return tuple(out)

        init = tuple((jnp.zeros((t_n, HEAD), F32), lt_ref[:, c][:, 0:1], jnp.zeros((t_n, 1), F32)) for c in cols)
        carry = lax.fori_loop(0, i, lambda j, cr: tile(j, cr, False), init)
        carry = tile(i, carry, True)
        for p, c in enumerate(cols):
            dq_ref[:, c] = (carry[p][0] * scale).astype(dq_ref.dtype)

        @pl.when(i == i_steps - 1)
        def _():
            def put(j, _):
                sl = pl.ds(pl.multiple_of(j * t_n, t_n), t_n)
                for p, c in enumerate(cols):
                    dk_ref[sl, c] = dkt[p, j].T.astype(dk_ref.dtype)
                    dv_ref[sl, c] = dvt[p, j].T.astype(dv_ref.dtype)
                return 0

            lax.fori_loop(0, i_steps, put, 0)

    act = jax.ShapeDtypeStruct((s_n, h_n * HEAD), BF16)
    acc_t = pltpu.VMEM((par, i_steps, HEAD, t_n), F32)
    outs = pl.pallas_call(
        body, name=name, grid=(h_steps, i_steps),
        in_specs=[tile_spec(4 * h_n), full(5 * h_n), full(6 * h_n), tile_spec(h_n), tile_spec(0)] + [ANY] * npart,
        out_specs=[tile_spec(0), full(0), full(0)] + [ANY] * npart,
        out_shape=[act, act, act] + _OwnersPlan.out_shapes(partials),
        scratch_shapes=_OwnersPlan.scratch(npart) + [acc_t, acc_t],
        compiler_params=_params("arbitrary", "arbitrary"),
    )(projb, projb, projb, dcat, spent_all, *partials)
    return outs[0], outs[1], outs[2], outs[3:]


def _adamw(name, w, g, m, v, tr=256):
    r_n, c_n = w.shape
    tr = _pick(r_n, tr, 8)
    c1 = 1.0 - ADAM_B1 ** ADAM_STEP
    c2 = 1.0 - ADAM_B2 ** ADAM_STEP

    def body(w_ref, g_ref, m_ref, v_ref, d_ref, nm_ref, nv_ref):
        gv = g_ref[...]
        nm = ADAM_B1 * m_ref[...] + (1.0 - ADAM_B1) * gv
        nv = ADAM_B2 * v_ref[...] + (1.0 - ADAM_B2) * (gv * gv)
        d_ref[...] = -ADAM_LR * ((nm / c1) / (jnp.sqrt(nv / c2) + ADAM_EPS) + ADAM_WD * w_ref[...])
        nm_ref[...] = nm
        nv_ref[...] = nv

    blk = pl.BlockSpec((tr, c_n), lambda i: (i, 0))
    sds = jax.ShapeDtypeStruct((r_n, c_n), F32)
    return pl.pallas_call(
        body, name=name, grid=(r_n // tr,), in_specs=[blk] * 4, out_specs=[blk] * 3, out_shape=[sds] * 3,
        compiler_params=_params("parallel"),
    )(w, g, m, v)


def _adamw_nd(name, w, g, m, v):
    shape = w.shape
    flat = lambda t: t.reshape(-1, shape[-1])
    return tuple(t.reshape(shape) for t in _adamw(name, flat(w), flat(g.reshape(shape)), flat(m), flat(v)))


def _mesh_pos():
    x, y, c = lax.axis_index("x"), lax.axis_index("y"), lax.axis_index("c")
    chips = [(1 - x, y), (x, 1 - y), (1 - x, 1 - y)]
    return x, y, c, chips, 2 * x + y, [2 * cx + cy for cx, cy in chips]


class _Unit:
    def __init__(self, shard_shape, axis, half_axis):
        self.shard_shape = tuple(shard_shape)
        self.axis = axis
        self.half_axis = half_axis
        self.full_shape = tuple(n * N_CHIPS if a == axis else n for a, n in enumerate(shard_shape))
        self.half_shape = tuple(n // 2 if a == half_axis else n for a, n in enumerate(shard_shape))

    def _window(self, ref, k, c, with_slab):
        idx = []
        for a, n in enumerate(self.shard_shape):
            start, size = 0, n
            if a == self.half_axis:
                size = n // 2
                start = c * size
            if with_slab and a == self.axis:
                start = start + k * n
            idx.append(pl.ds(start, size))
        return ref.at[tuple(idx)]

    def full_half(self, ref, k, c):
        return self._window(ref, k, c, True)

    def place_view(self):
        s = self.shard_shape
        if self.axis == len(s) - 1:
            return math.prod(s[:-2]), s[-2], s[-1], True
        assert self.axis == len(s) - 2
        return math.prod(s[:self.axis]), s[self.axis], s[-1], False

    def half_view(self):
        s, h = self.shard_shape, self.half_axis
        if h == len(s) - 1:
            return math.prod(s[:-2]), s[-2], s[-1] // 2, True
        return math.prod(s[:h]), (s[h] // 2) * math.prod(s[h + 1:-1]), s[-1], False


def _place_shard(name, shard, unit, chip_idx):
    l_n, r_n, c_n, by_cols = unit.place_view()
    tr = _pick(r_n, 256, 16)
    per = r_n // tr

    def body(k_ref, s_ref, o_ref):
        o_ref[...] = s_ref[...].astype(o_ref.dtype)

    if by_cols:
        full3, out_index = (l_n, r_n, N_CHIPS * c_n), (lambda l, i, k_ref: (l, i, k_ref[0]))
    else:
        full3, out_index = (l_n, N_CHIPS * r_n, c_n), (lambda l, i, k_ref: (l, k_ref[0] * per + i, 0))
    out = pl.pallas_call(
        body, name=name,
        grid_spec=pltpu.PrefetchScalarGridSpec(
            num_scalar_prefetch=1, grid=(l_n, per),
            in_specs=[pl.BlockSpec((None, tr, c_n), lambda l, i, k_ref: (l, i, 0))],
            out_specs=pl.BlockSpec((None, tr, c_n), out_index)),
        out_shape=jax.ShapeDtypeStruct(full3, BF16), compiler_params=_params("parallel", "parallel"),
    )(chip_idx, shard.reshape(l_n, r_n, c_n))
    return out.reshape(unit.full_shape)


def _gather_weights(units, fulls, scale_shard):
    nu = len(units)
    ps = scale_shard.shape[1]

    def body(*refs):
        sc_in = refs[nu]
        outs, sc_out = refs[nu + 1:2 * nu + 1], refs[2 * nu + 1]
        send3, recv3, lsem = refs[2 * nu + 2:2 * nu + 5]
        plan = _GatherPlan(units, outs, refs[2 * nu + 5:])
        x, y, c, chips, me, others = _mesh_pos()
        local = pltpu.make_async_copy(sc_in, sc_out.at[:, pl.ds(me * ps, ps)], lsem.at[0])
        local.start()
        plan.start()
        sends = [pltpu.make_async_remote_copy(
            src_ref=sc_in, dst_ref=sc_out.at[:, pl.ds(me * ps, ps)], send_sem=send3.at[j], recv_sem=recv3.at[j],
            device_id=(*chip, c), device_id_type=MESH) for j, chip in enumerate(chips)]
        for cp in sends:
            cp.start()
        plan.forward()
        plan.finish()
        for j in range(3):
            dst = sc_out.at[:, pl.ds(others[j] * ps, ps)]
            pltpu.make_async_remote_copy(src_ref=dst, dst_ref=dst, send_sem=send3.at[j], recv_sem=recv3.at[j],
                                         device_id=(x, y, c), device_id_type=MESH).wait_recv()
        for cp in sends:
            cp.wait_send()
        local.wait()

    out_shape = [jax.ShapeDtypeStruct(f.shape, f.dtype) for f in fulls]
    out_shape.append(jax.ShapeDtypeStruct((1, N_CHIPS * ps), scale_shard.dtype))
    dma = pltpu.SemaphoreType.DMA
    return pl.pallas_call(
        body, name="gather_weights", in_specs=[ANY] * (nu + 1), out_specs=[ANY] * (nu + 1), out_shape=out_shape,
        input_output_aliases={u: u for u in range(nu)},
        scratch_shapes=[dma((3,)), dma((3,)), dma((1,))] + _GatherPlan.scratch(nu),
    )(*fulls, scale_shard)


class _GatherPlan:
    def __init__(self, units, outs, sems):
        self.units, self.outs = units, outs
        self.ici, self.d2d = (sems[0], sems[1]), (sems[2], sems[3])
        self.x, self.y, self.c, self.chips, self.me, self.others = _mesh_pos()
        self.pairs = [(u, j) for u in range(len(units)) for j in range(3)]

    @staticmethod
    def scratch(nu):
        return [pltpu.SemaphoreType.DMA((3 * nu,)) for _ in range(4)]

    def _copy(self, window, sems, u, j, to):
        return pltpu.make_async_remote_copy(src_ref=window, dst_ref=window, send_sem=sems[0].at[3 * u + j],
                                            recv_sem=sems[1].at[3 * u + j], device_id=to, device_id_type=MESH)

    def _half(self, u, chip, core):
        return self.units[u].full_half(self.outs[u], chip, core)

    def start(self):
        for u, j in self.pairs:
            self._copy(self._half(u, self.me, self.c), self.ici, u, j, (*self.chips[j], self.c)).start()

    def forward(self):
        here, sibling = (self.x, self.y, self.c), (self.x, self.y, 1 - self.c)
        for u, j in self.pairs:
            landed = self._half(u, self.others[j], self.c)
            self._copy(landed, self.ici, u, j, here).wait_recv()
            self._copy(landed, self.d2d, u, j, sibling).start()

    def finish(self):
        here = (self.x, self.y, self.c)
        for u, j in self.pairs:
            self._copy(self._half(u, self.others[j], 1 - self.c), self.d2d, u, j, here).wait_recv()
        for u, j in self.pairs:
            self._copy(self._half(u, self.me, self.c), self.ici, u, j, here).wait_send()
            self._copy(self._half(u, self.others[j], self.c), self.d2d, u, j, here).wait_send()


class _OwnersPlan:
    def __init__(self, ins, outs, sems):
        self.ins, self.outs, self.send, self.recv = ins, outs, sems[0], sems[1]
        _, _, self.c, self.chips, _, self.others = _mesh_pos()

    @staticmethod
    def scratch(nu):
        return [pltpu.SemaphoreType.DMA((3 * nu,)) for _ in range(2)]

    @staticmethod
    def out_shapes(partials):
        return [jax.ShapeDtypeStruct((3,) + p.shape[1:], p.dtype) for p in partials]

    def _copies(self):
        return [pltpu.make_async_remote_copy(
            src_ref=self.ins[u].at[self.others[j]], dst_ref=self.outs[u].at[j], send_sem=self.send.at[3 * u + j],
            recv_sem=self.recv.at[3 * u + j], device_id=(*self.chips[j], self.c), device_id_type=MESH)
            for u in range(len(self.ins)) for j in range(3)]

    def start(self):
        for cp in self._copies():
            cp.start()

    def finish(self):
        for cp in self._copies():
            cp.wait()


class _SiblingPlan:
    def __init__(self, ins, outs, sems):
        self.ins, self.outs, self.send, self.recv = ins, outs, sems[0], sems[1]
        self.x, self.y, self.c, _, _, _ = _mesh_pos()

    @staticmethod
    def scratch(nu):
        return [pltpu.SemaphoreType.DMA((nu,)) for _ in range(2)]

    @staticmethod
    def out_shapes(grads):
        return [jax.ShapeDtypeStruct((g.shape[0],) + g.shape[2:], g.dtype) for g in grads]

    def _copies(self):
        return [pltpu.make_async_remote_copy(
            src_ref=self.ins[u].at[:, 1 - self.c], dst_ref=self.outs[u], send_sem=self.send.at[u],
            recv_sem=self.recv.at[u], device_id=(self.x, self.y, 1 - self.c), device_id_type=MESH)
            for u in range(len(self.ins))]

    def start(self):
        for cp in self._copies():
            cp.start()

    def finish(self):
        for cp in self._copies():
            cp.wait()


def _to_sibling(name, grads):
    nu = len(grads)

    def body(*refs):
        plan = _SiblingPlan(refs[:nu], refs[nu:2 * nu], refs[2 * nu:])
        plan.start()
        plan.finish()

    return pl.pallas_call(
        body, name=name, in_specs=[ANY] * nu, out_specs=[ANY] * nu, out_shape=_SiblingPlan.out_shapes(grads),
        scratch_shapes=_SiblingPlan.scratch(nu),
    )(*grads)


def _share_halves(halves):
    nu = len(halves)

    def body(*refs):
        ins, outs = refs[:nu], refs[nu:2 * nu]
        send, recv = refs[2 * nu:]
        x, y, c, _, _, _ = _mesh_pos()
        cps = [pltpu.make_async_remote_copy(
            src_ref=ins[u], dst_ref=outs[u], send_sem=send.at[u], recv_sem=recv.at[u],
            device_id=(x, y, 1 - c), device_id_type=MESH) for u in range(nu)]
        for cp in cps:
            cp.start()
        for cp in cps:
            cp.wait()

    out_shape = [jax.ShapeDtypeStruct(h.shape, h.dtype) for h in halves]
    dma = pltpu.SemaphoreType.DMA
    return pl.pallas_call(
        body, name="share_halves", in_specs=[ANY] * nu, out_specs=[ANY] * nu, out_shape=out_shape,
        scratch_shapes=[dma((nu,)), dma((nu,))],
    )(*halves)


def _add_mine(name, grad, recv, c_idx):
    _, _, r_n, c_n = grad.shape
    tr = _pick(r_n, 256, 16)

    def body(c_ref, g_ref, r_ref, o_ref):
        o_ref[...] = (g_ref[...] + r_ref[...]).astype(o_ref.dtype)

    return pl.pallas_call(
        body, name=name,
        grid_spec=pltpu.PrefetchScalarGridSpec(
            num_scalar_prefetch=1, grid=(N_CHIPS, r_n // tr),
            in_specs=[pl.BlockSpec((None, None, tr, c_n), lambda k, i, c_ref: (k, c_ref[0], i, 0)),
                      pl.BlockSpec((None, tr, c_n), lambda k, i, c_ref: (k, i, 0))],
            out_specs=pl.BlockSpec((None, tr, c_n), lambda k, i, c_ref: (k, i, 0))),
        out_shape=jax.ShapeDtypeStruct(recv.shape, BF16), compiler_params=_params("parallel", "parallel"),
    )(c_idx, grad, recv)


def _add_slots(name, partial, slots, chip_idx):
    _, r_n, c_n = slots.shape
    tr = _pick(r_n, 256, 16)

    def body(k_ref, p_ref, s_ref, o_ref):
        own = p_ref[...].astype(F32)
        o_ref[...] = ((own + s_ref[0].astype(F32)) + s_ref[1].astype(F32)) + s_ref[2].astype(F32)

    return pl.pallas_call(
        body, name=name,
        grid_spec=pltpu.PrefetchScalarGridSpec(
            num_scalar_prefetch=1, grid=(r_n // tr,),
            in_specs=[pl.BlockSpec((None, tr, c_n), lambda i, k_ref: (k_ref[0], i, 0)),
                      pl.BlockSpec((3, tr, c_n), lambda i, k_ref: (0, i, 0))],
            out_specs=pl.BlockSpec((tr, c_n), lambda i, k_ref: (i, 0))),
        out_shape=jax.ShapeDtypeStruct((r_n, c_n), F32), compiler_params=_params("parallel"),
    )(chip_idx, partial, slots)


def _adamw_halves(name, unit, w, m, v, mine, theirs, c_idx, tr=256):
    l_n, r_n, c_n, by_cols = unit.half_view()
    tr = _pick(r_n, tr, 8)
    c1 = 1.0 - ADAM_B1 ** ADAM_STEP
    c2 = 1.0 - ADAM_B2 ** ADAM_STEP

    def body(c_ref, w_ref, m_ref, v_ref, mine_ref, theirs_ref, g_ref, d_ref, nm_ref, nv_ref):
        gv = jnp.where(pl.program_id(1) == c_ref[0], mine_ref[...], theirs_ref[...])
        nm = ADAM_B1 * m_ref[...] + (1.0 - ADAM_B1) * gv
        nv = ADAM_B2 * v_ref[...] + (1.0 - ADAM_B2) * (gv * gv)
        d_ref[...] = -ADAM_LR * ((nm / c1) / (jnp.sqrt(nv / c2) + ADAM_EPS) + ADAM_WD * w_ref[...])
        g_ref[...] = gv
        nm_ref[...] = nm
        nv_ref[...] = nv

    if by_cols:
        view = (l_n, r_n, 2 * c_n)
        whole = pl.BlockSpec((None, tr, c_n), lambda l, h, i, c_ref: (l, i, h))
    else:
        view = (l_n, 2, r_n, c_n)
        whole = pl.BlockSpec((None, None, tr, c_n), lambda l, h, i, c_ref: (l, h, i, 0))
    mine_spec = pl.BlockSpec((None, tr, c_n), lambda l, h, i, c_ref: (l, jnp.where(h == c_ref[0], i, 0), 0))
    theirs_spec = pl.BlockSpec((None, tr, c_n), lambda l, h, i, c_ref: (l, jnp.where(h == c_ref[0], 0, i), 0))
    sds = jax.ShapeDtypeStruct(view, F32)
    outs = pl.pallas_call(
        body, name=name,
        grid_spec=pltpu.PrefetchScalarGridSpec(
            num_scalar_prefetch=1, grid=(l_n, 2, r_n // tr),
            in_specs=[whole, whole, whole, mine_spec, theirs_spec], out_specs=[whole] * 4),
        out_shape=[sds] * 4, compiler_params=_params("parallel", "parallel", "parallel"),
    )(c_idx, w.reshape(view), m.reshape(view), v.reshape(view),
      mine.reshape(l_n, r_n, c_n), theirs.reshape(l_n, r_n, c_n))
    return tuple(t.reshape(w.shape) for t in outs)


def _allreduce_small(block):
    r_n, c_n = block.shape

    def body(in_ref, out_ref, slots, send, recv):
        x, y, c = lax.axis_index("x"), lax.axis_index("y"), lax.axis_index("c")
        me = 4 * x + 2 * y + c
        slots[me] = in_ref[...]
        flips = [(fx, fy, fc) for fx in (0, 1) for fy in (0, 1) for fc in (0, 1)][1:]
        peers = [(x ^ fx, y ^ fy, c ^ fc) for fx, fy, fc in flips]
        cps = [pltpu.make_async_remote_copy(src_ref=in_ref, dst_ref=slots.at[me], send_sem=send.at[j],
                                            recv_sem=recv.at[j], device_id=peers[j], device_id_type=MESH)
               for j in range(7)]
        for cp in cps:
            cp.start()
        for j, (px, py, pc) in enumerate(peers):
            slot = slots.at[4 * px + 2 * py + pc]
            pltpu.make_async_remote_copy(src_ref=slot, dst_ref=slot, send_sem=send.at[j], recv_sem=recv.at[j],
                                         device_id=(x, y, c), device_id_type=MESH).wait_recv()
        for cp in cps:
            cp.wait_send()
        total = slots[0]
        for d in range(1, 8):
            total = total + slots[d]
        out_ref[...] = total

    vmem = pl.BlockSpec(memory_space=pltpu.VMEM)
    return pl.pallas_call(
        body, name="allreduce_small", in_specs=[vmem], out_specs=vmem,
        out_shape=jax.ShapeDtypeStruct((r_n, c_n), F32),
        scratch_shapes=[pltpu.VMEM((8, r_n, c_n), F32), pltpu.SemaphoreType.DMA((7,)), pltpu.SemaphoreType.DMA((7,))],
    )(block)


def _first(accs, extras):
    return [accs[0]] if isinstance(accs, list) else [accs]


def _ffn_fwd(tag, x_in, h, wg, wu, wd):
    def act(accs, extras):
        a, b = accs
        return [a, b, a * _sigmoid(a) * b]

    a, b, s = _mm_nn(f"ffn_up_{tag}", h, [wg, wu], [], act, [BF16, BF16, BF16])
    x_out, = _mm_nn(f"ffn_down_{tag}", s, [wd], [x_in], lambda accs, ex: [ex[0] + accs[0]], [F32], tk=8192)
    return x_out, a, b, s


def _ffn_bwd(tag, layer, dx_out, h, a, b, s, wg, wu, wd, into):
    def mid(acc, extras):
        av, bv = extras[0].astype(F32), extras[1].astype(F32)
        sg = _sigmoid(av)
        return [acc * bv * _dsilu(av, sg), acc * (av * sg)]

    dxb = dx_out.astype(BF16)
    da, db = _mm_nt(f"ffn_dact_{tag}", [(dxb, wd)], [a, b], mid, [BF16, BF16], tm=512, to=wd.shape[1] // N_CHIPS,
                    weights_stay=True)
    dh, = _mm_nt(f"ffn_dh_{tag}", [(da, wg), (db, wu)], [], _first, [F32], tm=1024, tr=2816)
    d_n, f_n = wg.shape[1], wg.shape[2]
    ns = f_n // N_CHIPS
    tki = _pick(d_n // 2, 512)
    ih = (d_n // 2) // tki
    col_shape = (N_CHIPS, 2, 2, d_n // 2, ns)
    col_block = (None, None, None, tki, ns)
    col_index = lambda g, i, j: (j, i // ih, layer, i % ih, 0)
    dwg = _mm_tn(f"ffn_dwg_{tag}", h, da, 1, col_shape, col_block, col_index, tki=tki, tn=ns, into=into[0])
    dwu = _mm_tn(f"ffn_dwu_{tag}", h, db, 1, col_shape, col_block, col_index, tki=tki, tn=ns, into=into[1])
    tn = _pick(d_n // 2, 512)
    jh = (d_n // 2) // tn
    dwd = _mm_tn(f"ffn_dwd_{tag}", s, dxb, 1, (N_CHIPS, 2, 2, ns, d_n // 2), (None, None, None, ns, tn),
                 lambda g, i, j: (i, j // jh, layer, 0, j % jh), tki=ns, tn=tn, into=into[2])
    return dh, (dwg, dwu, dwd)


def kernel(x, mix_norm, ffn_norm, final_norm, ab_w_in, lb_logits, hg_out_norm, ab_w_out, pool_w, pool_scale, ffn_w_gate, ffn_w_up, ffn_w_down, loss_target, m_mix_norm, m_ffn_norm, m_final_norm, m_ab_w_in, m_lb_logits, m_hg_out_norm, m_ab_w_out, m_pool_w, m_pool_scale, m_ffn_w_gate, m_ffn_w_up, m_ffn_w_down, v_mix_norm, v_ffn_norm, v_final_norm, v_ab_w_in, v_lb_logits, v_hg_out_norm, v_ab_w_out, v_pool_w, v_pool_scale, v_ffn_w_gate, v_ffn_w_up, v_ffn_w_down):
    xs, target = x[0], loss_target[0]
    s_n, d_n = xs.shape
    h_n = d_n // 2 // HEAD
    hw = h_n * HEAD
    n_grp = len(POOL_WINDOWS)
    grp = d_n // n_grp
    c_idx = lax.axis_index("c").astype(jnp.int32).reshape(1)
    chip = 2 * lax.axis_index("x") + lax.axis_index("y")

    units = [
        _Unit(ab_w_in.shape[1:], 1, 0),
        _Unit(ab_w_out.shape[1:], 0, 0),
        _Unit(pool_w.shape[1:], 1, 0),
        _Unit(ffn_w_gate.shape, 2, 1),
        _Unit(ffn_w_up.shape, 2, 1),
        _Unit(ffn_w_down.shape, 1, 2),
    ]
    chip_idx = chip.astype(jnp.int32).reshape(1)
    shards = [ab_w_in[0], ab_w_out[0], pool_w[0], ffn_w_gate, ffn_w_up, ffn_w_down]
    placed = [_place_shard(f"place_{n}", t, u, chip_idx) for n, (t, u) in enumerate(zip(shards, units))]
    w_in, w_out, scale_full = _gather_weights(units[:2], placed[:2], pool_scale)
    w_in3, w_out3 = w_in[None], w_out[None]
    row = lambda t: t.reshape(1, -1)

    h0 = _rms_fwd("norm_mix0", xs, row(mix_norm[0]))
    proj, projb = _mm_nn("proj_in", h0, [w_in3], [], lambda accs, ex: [accs[0], accs[0]], [F32, BF16],
                         tm=512, tn=7 * hw // N_CHIPS, weights_stay=True)
    oraw, o_a, states = _hg_fwd("hgrn_fwd", proj, lb_logits, hg_out_norm, h_n)
    o_b, ltot, (w_pool, w_gate, w_up, w_down) = _att_fwd("attn_fwd", projb, h_n, units[2:], placed[2:])
    cat = jnp.concatenate([o_a, o_b], axis=1)
    x1, = _mm_nn("proj_out", cat, [w_out3], [xs], lambda accs, ex: [ex[0] + accs[0]], [F32])
    h1 = _rms_fwd("norm_ffn0", x1, row(ffn_norm[0]))
    x2, a0, b0, s0 = _ffn_fwd("l0", x1, h1, w_gate[0:1], w_up[0:1], w_down[0:1])
    pooled = _pool_fwd("pool_fwd", x2, row(mix_norm[1]))
    x3, mixed = _mm_nn("pool_mix", pooled, [w_pool], [x2, scale_full],
                       lambda accs, ex: [ex[0] + accs[0] * ex[1], accs[0]], [F32, F32], tk=grp, tn=grp)
    h3 = _rms_fwd("norm_ffn1", x3, row(ffn_norm[1]))
    x4, a1, b1, s1 = _ffn_fwd("l1", x3, h3, w_gate[1:2], w_up[1:2], w_down[1:2])

    dx4, d_final, loss = _loss_bwd("loss_bwd", x4, target, row(final_norm))
    dh3, ffn_grads = _ffn_bwd("l1", 1, dx4, h3, a1, b1, s1, w_gate[1:2], w_up[1:2], w_down[1:2], (None, None, None))
    dx3, d_ffn1 = _rms_bwd("norm_ffn1_bwd", dh3, x3, row(ffn_norm[1]), dx4)
    dmixed, d_scale = _scale_bwd("pool_scale_bwd", dx3, mixed, scale_full)
    dpooled, = _mm_nt("pool_dpooled", [(dmixed, w_pool)], [], _first, [F32], to=grp, tr=grp)
    slab_rows = grp // N_CHIPS
    d_pool = _mm_tn("pool_dw", pooled, dmixed, n_grp, (N_CHIPS, 2, n_grp // 2, slab_rows, grp),
                    (None, None, None, slab_rows, grp), lambda g, i, j: (i, g // 2, g % 2, 0, 0),
                    tki=slab_rows, tn=grp)
    dx2, d_mix1 = _pool_bwd("pool_bwd", dpooled, x2, row(mix_norm[1]), dx3)
    dh1, ffn_grads = _ffn_bwd("l0", 0, dx2, h1, a0, b0, s0, w_gate[0:1], w_up[0:1], w_down[0:1], ffn_grads)
    dx1, d_ffn0 = _rms_bwd("norm_ffn0_bwd", dh1, x1, row(ffn_norm[0]), dx2)
    dx1b = dx1.astype(BF16)
    d_wout = _mm_tn("proj_out_dw", cat, dx1b, 1, (1, 2 * hw, d_n), (None, _pick(2 * hw, 512), _pick(d_n, 512)),
                    lambda g, i, j: (g, i, j), tki=_pick(2 * hw, 512), tn=_pick(d_n, 512))
    as4 = lambda g, u: g.reshape(N_CHIPS, 2, -1, u.half_shape[-1])
    early4 = [as4(g, u) for g, u in zip([d_wout, d_pool, *ffn_grads], units[1:])]
    dcat, *early_sib = _mm_nt("proj_out_dcat", [(dx1b, w_out3)], [], _first, [F32], tm=1024,
                              exchange=early4, plan_cls=_SiblingPlan)
    early_part = [_add_mine(f"add_sibling_{n + 1}", g, r, c_idx) for n, (g, r) in enumerate(zip(early4, early_sib))]
    dqb, dkb, dvb, early_slots = _att_bwd("attn_bwd", projb, dcat, ltot, h_n, early_part)
    dqa, dfa, dia, dga, d_lb, d_hgn = _hg_bwd("hgrn_bwd", proj, dcat, oraw, states, lb_logits, hg_out_norm, h_n)
    dproj = jnp.concatenate([dqa, dfa, dia, dga, dqb, dkb, dvb], axis=1)
    ns_in = 7 * hw // N_CHIPS
    tki_in, tn_in = _pick(d_n // 2, 512), _pick(ns_in, 1792)
    ih_in, jps_in = (d_n // 2) // tki_in, ns_in // tn_in
    d_win = _mm_tn("proj_in_dw", h0, dproj, 1, (N_CHIPS, 2, d_n // 2, ns_in), (None, None, tki_in, tn_in),
                   lambda g, i, j: (j // jps_in, i // ih_in, i % ih_in, j % jps_in), tki=tki_in, tn=tn_in)
    win4 = as4(d_win, units[0])
    win_part = _add_mine("add_sibling_0", win4, _to_sibling("grad_in_to_sibling", [win4])[0], c_idx)
    dh0, win_slots = _mm_nt("proj_in_dh", [(dproj, w_in3)], [], _first, [F32], tm=1024, tr=3584,
                            exchange=[win_part])
    dx0, d_mix0 = _rms_bwd("norm_mix0_bwd", dh0, xs, row(mix_norm[0]), dx1)

    partials = [win_part, *early_part]
    slots = [win_slots, *early_slots]
    mine = [_add_slots(f"add_chips_{n}", p, s, chip_idx) for n, (p, s) in enumerate(zip(partials, slots))]
    theirs = _share_halves(mine)

    lanes = 2 * d_n
    pad = lambda t: jnp.pad(t.reshape(1, -1), ((0, 0), (0, lanes - t.size)))
    small = jnp.concatenate([
        pad(jnp.concatenate([d_mix0, d_mix1], axis=0)), pad(jnp.concatenate([d_ffn0, d_ffn1], axis=0)),
        pad(d_final), pad(d_lb), pad(d_hgn), pad(d_scale), jnp.zeros((2, lanes), F32)], axis=0)
    small = _allreduce_small(small)
    g_mix = small[0, :2 * d_n].reshape(2, d_n)
    g_ffn = small[1, :2 * d_n].reshape(2, d_n)
    g_final = small[2, :d_n]
    g_lb = small[3, :2 * hw].reshape(2, hw)
    g_hgn = small[4, :HEAD].reshape(1, HEAD)
    g_scale = lax.dynamic_slice(small[5, :d_n], (chip * grp,), (grp,)).reshape(1, grp)
    loss = lax.psum(loss[0, 0], ("x", "y", "c"))

    small_grads = {0: g_mix, 1: g_ffn, 2: g_final, 4: g_lb, 5: g_hgn, 8: g_scale}
    unit_of = {3: 0, 6: 1, 7: 2, 9: 3, 10: 4, 11: 5}
    weights = [mix_norm, ffn_norm, final_norm, ab_w_in, lb_logits, hg_out_norm, ab_w_out, pool_w, pool_scale,
               ffn_w_gate, ffn_w_up, ffn_w_down]
    ms = [m_mix_norm, m_ffn_norm, m_final_norm, m_ab_w_in, m_lb_logits, m_hg_out_norm, m_ab_w_out, m_pool_w,
          m_pool_scale, m_ffn_w_gate, m_ffn_w_up, m_ffn_w_down]
    vs = [v_mix_norm, v_ffn_norm, v_final_norm, v_ab_w_in, v_lb_logits, v_hg_out_norm, v_ab_w_out, v_pool_w,
          v_pool_scale, v_ffn_w_gate, v_ffn_w_up, v_ffn_w_down]
    grads, deltas, new_ms, new_vs = [], [], [], []
    for n, (w, m, v) in enumerate(zip(weights, ms, vs)):
        if n in unit_of:
            u = unit_of[n]
            g, d, nm, nv = _adamw_halves(f"adamw_{n}", units[u], w, m, v, mine[u], theirs[u], c_idx)
        else:
            w2 = w.reshape(1, -1) if w.ndim == 1 else w
            g = small_grads[n].reshape(w2.shape)
            d, nm, nv = _adamw_nd(f"adamw_{n}", w2, g, m.reshape(w2.shape), v.reshape(w2.shape))
        grads.append(g.reshape(w.shape))
        deltas.append(d.reshape(w.shape))
        new_ms.append(nm.reshape(w.shape))
        new_vs.append(nv.reshape(w.shape))
    return (loss, dx0[None], *grads, *deltas, *new_ms, *new_vs)
```

```python
import functools
import math

import jax
import jax.numpy as jnp
from jax import lax
from jax.experimental import pallas as pl
from jax.experimental.pallas import tpu as pltpu

F32 = jnp.float32
BF16 = jnp.bfloat16
HIGHEST = lax.Precision.HIGHEST
MESH = pl.DeviceIdType.MESH
ANY = pl.BlockSpec(memory_space=pl.ANY)

RMS_EPS = 1e-6
LOG2_E = 1.4426950408889634
HEAD = 128
HG_CHUNK = 64
HG_MID = HG_CHUNK // 2 - 1
HG_BLOCK = 512
HG_PAR = 8
ATT_BLOCK = 256
ATT_PAR_FWD = 4
ATT_PAR_BWD = 2
POOL_WINDOWS = (2, 4, 8, 16)
POOL_HALO = 16
N_CHIPS = 4
ADAM_LR, ADAM_B1, ADAM_B2, ADAM_EPS, ADAM_WD, ADAM_STEP = 0.001, 0.9, 0.999, 1e-08, 0.01, 10
VMEM_LIMIT = 56 * 1024 * 1024
MM_CHUNK_ROWS = 256

NT_DIMS = (((1,), (1,)), ((), ()))
TN_DIMS = (((0,), (0,)), ((), ()))


def _params(*sem):
    return pltpu.CompilerParams(dimension_semantics=sem, vmem_limit_bytes=VMEM_LIMIT)


def _pick(dim, pref, unit=128):
    best = None
    for t in range(unit, min(dim, pref) + 1, unit):
        if dim % t == 0:
            best = t
    return dim if best is None else best


def _row_chunks(tm, rows):
    n = 1
    while n < 4 and tm % (2 * n) == 0 and tm // (2 * n) >= rows:
        n *= 2
    return n


def _sigmoid(z):
    return 1.0 / (1.0 + jnp.exp(-z))


def _dsilu(a, sg):
    return sg * (1.0 + a * (1.0 - sg))


def _mm_nn(name, a, bs, extras, epilogue, out_dtypes, *, tm=1024, tn=512, tk=2048, weights_stay=False):
    g_n, k_n, n_n = bs[0].shape
    m_n = a.shape[0]
    tm, tn, tk = _pick(m_n, tm, 8), _pick(n_n, tn), _pick(k_n, tk)
    i_n, j_n, kt = m_n // tm, n_n // tn, k_n // tk
    nb, ne, no = len(bs), len(extras), len(out_dtypes)
    split = _row_chunks(tm, MM_CHUNK_ROWS)

    def body(*refs):
        a_ref, b_refs, e_refs = refs[0], refs[1:1 + nb], refs[1 + nb:1 + nb + ne]
        o_refs, acc_refs = refs[1 + nb + ne:1 + nb + ne + no], refs[1 + nb + ne + no:]
        k = pl.program_id(3)

        def finish(accs, rows=slice(None)):
            outs = epilogue(accs, [e[...] if e.shape[0] == 1 else e[rows, :] for e in e_refs])
            for o_ref, o in zip(o_refs, outs):
                o_ref[rows, :] = o.astype(o_ref.dtype)

        if kt == 1:
            chunks = [slice(s * (tm // split), (s + 1) * (tm // split)) for s in range(split)]
            prods = [[jnp.dot(a_ref[rows, :], b_ref[...], preferred_element_type=F32) for b_ref in b_refs]
                     for rows in chunks]
            for rows, p in zip(chunks, prods):
                finish(p, rows)
        else:
            av = a_ref[...]
            prods = [jnp.dot(av, b_ref[...], preferred_element_type=F32) for b_ref in b_refs]

            @pl.when(k == 0)
            def _():
                for acc, p in zip(acc_refs, prods):
                    acc[...] = p

            @pl.when(k > 0)
            def _():
                for acc, p in zip(acc_refs, prods):
                    acc[...] += p

            @pl.when(k == kt - 1)
            def _():
                finish([acc[...] for acc in acc_refs])

    at = (lambda f: lambda g, j, i, k: f(g, i, j, k)) if weights_stay else (lambda f: f)
    in_specs = [pl.BlockSpec((tm, tk), at(lambda g, i, j, k: (i, g * kt + k)))]
    in_specs += [pl.BlockSpec((None, tk, tn), at(lambda g, i, j, k: (g, k, j))) for _ in bs]
    for e in extras:
        if e.shape[0] == 1:
            in_specs.append(pl.BlockSpec((1, tn), at(lambda g, i, j, k: (0, g * j_n + j))))
        else:
            in_specs.append(pl.BlockSpec((tm, tn), at(lambda g, i, j, k: (i, g * j_n + j))))
    out_specs = [pl.BlockSpec((tm, tn), at(lambda g, i, j, k: (i, g * j_n + j))) for _ in out_dtypes]
    out_shape = [jax.ShapeDtypeStruct((m_n, g_n * n_n), dt) for dt in out_dtypes]
    scratch = [] if kt == 1 else [pltpu.VMEM((tm, tn), F32) for _ in bs]
    grid = (g_n, j_n, i_n, kt) if weights_stay else (g_n, i_n, j_n, kt)
    return pl.pallas_call(
        body, name=name, grid=grid, in_specs=in_specs, out_specs=out_specs, out_shape=out_shape,
        scratch_shapes=scratch, compiler_params=_params("parallel", "parallel", "parallel", "arbitrary"),
    )(a, *bs, *extras)


def _mm_nt(name, pairs, extras, epilogue, out_dtypes, *, tm=1024, to=512, tr=2048, exchange=(),
           weights_stay=False, plan_cls=None):
    plan_cls = plan_cls or _OwnersPlan
    g_n, kd, n_n = pairs[0][1].shape
    m_n = pairs[0][0].shape[0]
    tm, to, tr = _pick(m_n, tm, 8), _pick(kd, to), _pick(n_n, tr)
    i_n, j_n, rt = m_n // tm, kd // to, n_n // tr
    npairs, ne, no, nx = len(pairs), len(extras), len(out_dtypes), len(exchange)
    n_in = 2 * npairs + ne
    n_acc = 0 if rt == 1 else 1
    split = _row_chunks(tm, 2 * MM_CHUNK_ROWS)

    def body(*refs):
        ab_refs, e_refs = refs[:2 * npairs], refs[2 * npairs:n_in]
        o_refs = refs[n_in + nx:n_in + nx + no]
        acc_refs = refs[n_in + 2 * nx + no:n_in + 2 * nx + no + n_acc]
        if nx:
            plan = plan_cls(refs[n_in:n_in + nx], refs[n_in + nx + no:n_in + 2 * nx + no],
                            refs[n_in + 2 * nx + no + n_acc:])
            step = ((pl.program_id(0) * grid[1] + pl.program_id(1)) * grid[2] + pl.program_id(2)) * grid[3] \
                + pl.program_id(3)
            pl.when(step == 0)(plan.start)
            pl.when(step == math.prod(grid) - 1)(plan.finish)
        r = pl.program_id(3)

        def product(rows):
            prod = None
            for p in range(npairs):
                t = lax.dot_general(ab_refs[2 * p][rows, :], ab_refs[2 * p + 1][...], NT_DIMS,
                                    preferred_element_type=F32)
                prod = t if prod is None else prod + t
            return prod

        def finish(acc, rows=slice(None)):
            outs = epilogue(acc, [e[rows, :] for e in e_refs])
            for o_ref, o in zip(o_refs, outs):
                o_ref[rows, :] = o.astype(o_ref.dtype)

        if rt == 1:
            chunks = [slice(s * (tm // split), (s + 1) * (tm // split)) for s in range(split)]
            prods = [product(rows) for rows in chunks]
            for rows, p in zip(chunks, prods):
                finish(p, rows)
        else:
            prod = product(slice(None))
            acc = acc_refs[0]

            @pl.when(r == 0)
            def _():
                acc[...] = prod

            @pl.when(r > 0)
            def _():
                acc[...] += prod

            @pl.when(r == rt - 1)
            def _():
                finish(acc[...])

    at = (lambda f: lambda g, j, i, r: f(g, i, j, r)) if weights_stay else (lambda f: f)
    in_specs, args = [], []
    for a, b in pairs:
        in_specs.append(pl.BlockSpec((tm, tr), at(lambda g, i, j, r: (i, g * rt + r))))
        in_specs.append(pl.BlockSpec((None, to, tr), at(lambda g, i, j, r: (g, j, r))))
        args += [a, b]
    in_specs += [pl.BlockSpec((tm, to), at(lambda g, i, j, r: (i, g * j_n + j))) for _ in extras]
    out_specs = [pl.BlockSpec((tm, to), at(lambda g, i, j, r: (i, g * j_n + j))) for _ in out_dtypes]
    out_shape = [jax.ShapeDtypeStruct((m_n, g_n * kd), dt) for dt in out_dtypes]
    scratch = [] if rt == 1 else [pltpu.VMEM((tm, to), F32)]
    grid = (g_n, j_n, i_n, rt) if weights_stay else (g_n, i_n, j_n, rt)
    if not nx:
        return pl.pallas_call(
            body, name=name, grid=grid, in_specs=in_specs, out_specs=out_specs, out_shape=out_shape,
            scratch_shapes=scratch, compiler_params=_params("parallel", "parallel", "parallel", "arbitrary"),
        )(*args, *extras)
    return pl.pallas_call(
        body, name=name, grid=grid, in_specs=in_specs + [ANY] * nx, out_specs=out_specs + [ANY] * nx,
        out_shape=out_shape + plan_cls.out_shapes(exchange), scratch_shapes=scratch + plan_cls.scratch(nx),
        compiler_params=_params("arbitrary", "arbitrary", "arbitrary", "arbitrary"),
    )(*args, *extras, *exchange)


def _mm_tn(name, a, b, g_n, out_shape, out_block, out_index, *, tki, tn, tm=2048, into=None):
    m_n = a.shape[0]
    k_n, n_n = a.shape[1] // g_n, b.shape[1] // g_n
    tm = _pick(m_n, tm, 8)
    i_n, j_n, mt = k_n // tki, n_n // tn, m_n // tm
    assert k_n % tki == 0 and n_n % tn == 0

    def body(*refs):
        a_ref, b_ref = refs[0], refs[1]
        o_ref, acc = refs[-2], refs[-1]
        m = pl.program_id(3)
        prod = lax.dot_general(a_ref[...], b_ref[...], TN_DIMS, preferred_element_type=F32)

        @pl.when(m == 0)
        def _():
            acc[...] = prod

        @pl.when(m > 0)
        def _():
            acc[...] += prod

        @pl.when(m == mt - 1)
        def _():
            o_ref[...] = acc[...].reshape(o_ref.shape)

    in_specs = [pl.BlockSpec((tm, tki), lambda g, i, j, m: (m, g * i_n + i)),
                pl.BlockSpec((tm, tn), lambda g, i, j, m: (m, g * j_n + j))]
    args = [a, b]
    aliases = {}
    if into is not None:
        in_specs.append(ANY)
        args.append(into)
        aliases = {2: 0}
    return pl.pallas_call(
        body, name=name, grid=(g_n, i_n, j_n, mt), in_specs=in_specs,
        out_specs=pl.BlockSpec(out_block, lambda g, i, j, m: out_index(g, i, j)),
        out_shape=jax.ShapeDtypeStruct(out_shape, F32), scratch_shapes=[pltpu.VMEM((tki, tn), F32)],
        input_output_aliases=aliases,
        compiler_params=_params("parallel", "parallel", "parallel", "arbitrary"),
    )(*args)


def _rstd(xv):
    return lax.rsqrt(jnp.mean(xv * xv, axis=-1, keepdims=True) + RMS_EPS)


def _rms_bwd_rows(dh, xv, gain, r):
    dy = dh * gain
    c = jnp.mean(dy * xv, axis=-1, keepdims=True)
    return r * dy - xv * (r * r * r) * c, dh * xv * r


def _fold8(t):
    return t.reshape(t.shape[0] // 8, 8, t.shape[1]).sum(axis=0)


def _rms_fwd(name, x, gain, tm=256):
    s_n, d_n = x.shape
    tm = _pick(s_n, tm, 8)

    def body(x_ref, g_ref, h_ref):
        xv = x_ref[...]
        h_ref[...] = (xv * _rstd(xv) * g_ref[...]).astype(h_ref.dtype)

    return pl.pallas_call(
        body, name=name, grid=(s_n // tm,),
        in_specs=[pl.BlockSpec((tm, d_n), lambda i: (i, 0)), pl.BlockSpec((1, d_n), lambda i: (0, 0))],
        out_specs=pl.BlockSpec((tm, d_n), lambda i: (i, 0)), out_shape=jax.ShapeDtypeStruct((s_n, d_n), BF16),
        compiler_params=_params("parallel"),
    )(x, gain)


def _rms_bwd(name, dh, x, gain, dres, tm=256):
    s_n, d_n = x.shape
    tm = _pick(s_n, tm, 8)
    nblk = s_n // tm

    def body(dh_ref, x_ref, g_ref, dres_ref, dx_ref, dg_ref, acc):
        i = pl.program_id(0)

        @pl.when(i == 0)
        def _():
            acc[...] = jnp.zeros_like(acc)

        xv = x_ref[...]
        dxv, dgt = _rms_bwd_rows(dh_ref[...].astype(F32), xv, g_ref[...], _rstd(xv))
        dx_ref[...] = dres_ref[...] + dxv
        acc[...] += _fold8(dgt)

        @pl.when(i == nblk - 1)
        def _():
            dg_ref[...] = jnp.sum(acc[...], axis=0, keepdims=True)

    row = pl.BlockSpec((tm, d_n), lambda i: (i, 0))
    vec = pl.BlockSpec((1, d_n), lambda i: (0, 0))
    return pl.pallas_call(
        body, name=name, grid=(nblk,), in_specs=[row, row, vec, row], out_specs=[row, vec],
        out_shape=[jax.ShapeDtypeStruct((s_n, d_n), F32), jax.ShapeDtypeStruct((1, d_n), F32)],
        scratch_shapes=[pltpu.VMEM((8, d_n), F32)], compiler_params=_params("arbitrary"),
    )(dh, x, gain, dres)


def _loss_bwd(name, x, target, gain, tm=256):
    s_n, d_n = x.shape
    tm = _pick(s_n, tm, 8)
    nblk = s_n // tm

    def body(x_ref, t_ref, g_ref, dx_ref, dg_ref, loss_ref, acc, lacc):
        i = pl.program_id(0)

        @pl.when(i == 0)
        def _():
            acc[...] = jnp.zeros_like(acc)
            lacc[...] = jnp.zeros_like(lacc)

        xv = x_ref[...]
        gain = g_ref[...]
        r = _rstd(xv)
        diff = xv * r * gain - t_ref[...]
        lacc[...] += _fold8(diff * diff)
        dxv, dgt = _rms_bwd_rows(diff * (1.0 / d_n), xv, gain, r)
        dx_ref[...] = dxv
        acc[...] += _fold8(dgt)

        @pl.when(i == nblk - 1)
        def _():
            dg_ref[...] = jnp.sum(acc[...], axis=0, keepdims=True)
            loss_ref[...] = jnp.sum(lacc[...], keepdims=True) * (0.5 / d_n)

    row = pl.BlockSpec((tm, d_n), lambda i: (i, 0))
    vec = pl.BlockSpec((1, d_n), lambda i: (0, 0))
    return pl.pallas_call(
        body, name=name, grid=(nblk,), in_specs=[row, row, vec],
        out_specs=[row, vec, pl.BlockSpec((1, 1), lambda i: (0, 0))],
        out_shape=[jax.ShapeDtypeStruct((s_n, d_n), F32), jax.ShapeDtypeStruct((1, d_n), F32),
                   jax.ShapeDtypeStruct((1, 1), F32)],
        scratch_shapes=[pltpu.VMEM((8, d_n), F32), pltpu.VMEM((8, d_n), F32)], compiler_params=_params("arbitrary"),
    )(x, target, gain)


def _pool_counts(t_idx, d_n):
    grp = d_n // len(POOL_WINDOWS)
    lane = lax.broadcasted_iota(jnp.int32, (1, d_n), 1) // grp
    win = jnp.zeros((1, d_n), jnp.int32)
    for gi, w in enumerate(POOL_WINDOWS):
        win = jnp.where(lane == gi, w, win)
    return jnp.minimum(t_idx + 1, win).astype(F32), lane


def _window_sums(rows, lane, backward):
    n = rows.shape[0]
    out = rows
    acc = rows
    width = 1
    for gi in range(len(POOL_WINDOWS)):
        shift = (n - width) if backward else width
        acc = acc + pltpu.roll(acc, shift, 0)
        width *= 2
        out = jnp.where(lane >= gi, acc, out)
    return out


def _pool_fwd(name, x, gain, tm=256):
    s_n, d_n = x.shape
    tm = _pick(s_n, tm, POOL_HALO)
    per = tm // POOL_HALO

    def body(x_ref, halo_ref, g_ref, o_ref):
        i = pl.program_id(0)
        halo = jnp.where(i == 0, 0.0, halo_ref[...])
        rows = jnp.concatenate([halo, x_ref[...]], axis=0)
        h = rows * _rstd(rows) * g_ref[...]
        t_idx = i * tm - POOL_HALO + lax.broadcasted_iota(jnp.int32, (tm + POOL_HALO, 1), 0)
        cnt, lane = _pool_counts(t_idx, d_n)
        pooled = _window_sums(h, lane, False) / cnt - h
        o_ref[...] = pooled[POOL_HALO:, :].astype(o_ref.dtype)

    return pl.pallas_call(
        body, name=name, grid=(s_n // tm,),
        in_specs=[pl.BlockSpec((tm, d_n), lambda i: (i, 0)),
                  pl.BlockSpec((POOL_HALO, d_n), lambda i: (jnp.maximum(i * per - 1, 0), 0)),
                  pl.BlockSpec((1, d_n), lambda i: (0, 0))],
        out_specs=pl.BlockSpec((tm, d_n), lambda i: (i, 0)), out_shape=jax.ShapeDtypeStruct((s_n, d_n), BF16),
        compiler_params=_params("parallel"),
    )(x, x, gain)


def _pool_bwd(name, dpooled, x, gain, dres, tm=256):
    s_n, d_n = x.shape
    tm = _pick(s_n, tm, POOL_HALO)
    per = tm // POOL_HALO
    nblk = s_n // tm
    last_halo = s_n // POOL_HALO - 1

    def body(dp_ref, halo_ref, x_ref, g_ref, dres_ref, dx_ref, dg_ref, acc):
        i = pl.program_id(0)

        @pl.when(i == 0)
        def _():
            acc[...] = jnp.zeros_like(acc)

        halo = jnp.where(i == nblk - 1, 0.0, halo_ref[...])
        rows = jnp.concatenate([dp_ref[...], halo], axis=0)
        t_idx = i * tm + lax.broadcasted_iota(jnp.int32, (tm + POOL_HALO, 1), 0)
        cnt, lane = _pool_counts(t_idx, d_n)
        dh = (_window_sums(rows / cnt, lane, True) - rows)[:tm, :]
        xv = x_ref[...]
        dxv, dgt = _rms_bwd_rows(dh, xv, g_ref[...], _rstd(xv))
        dx_ref[...] = dres_ref[...] + dxv
        acc[...] += _fold8(dgt)

        @pl.when(i == nblk - 1)
        def _():
            dg_ref[...] = jnp.sum(acc[...], axis=0, keepdims=True)

    row = pl.BlockSpec((tm, d_n), lambda i: (i, 0))
    vec = pl.BlockSpec((1, d_n), lambda i: (0, 0))
    return pl.pallas_call(
        body, name=name, grid=(nblk,),
        in_specs=[row, pl.BlockSpec((POOL_HALO, d_n), lambda i: (jnp.minimum((i + 1) * per, last_halo), 0)),
                  row, vec, row],
        out_specs=[row, vec],
        out_shape=[jax.ShapeDtypeStruct((s_n, d_n), F32), jax.ShapeDtypeStruct((1, d_n), F32)],
        scratch_shapes=[pltpu.VMEM((8, d_n), F32)], compiler_params=_params("arbitrary"),
    )(dpooled, dpooled, x, gain, dres)


def _scale_bwd(name, dx, mixed, scale, tm=256):
    s_n, d_n = dx.shape
    tm = _pick(s_n, tm, 8)
    nblk = s_n // tm

    def body(dx_ref, mx_ref, sc_ref, dm_ref, ds_ref, acc):
        i = pl.program_id(0)

        @pl.when(i == 0)
        def _():
            acc[...] = jnp.zeros_like(acc)

        dxv = dx_ref[...]
        dm_ref[...] = (dxv * sc_ref[...]).astype(dm_ref.dtype)
        acc[...] += _fold8(dxv * mx_ref[...])

        @pl.when(i == nblk - 1)
        def _():
            ds_ref[...] = jnp.sum(acc[...], axis=0, keepdims=True)

    row = pl.BlockSpec((tm, d_n), lambda i: (i, 0))
    vec = pl.BlockSpec((1, d_n), lambda i: (0, 0))
    return pl.pallas_call(
        body, name=name, grid=(nblk,), in_specs=[row, row, vec], out_specs=[row, vec],
        out_shape=[jax.ShapeDtypeStruct((s_n, d_n), BF16), jax.ShapeDtypeStruct((1, d_n), F32)],
        scratch_shapes=[pltpu.VMEM((8, d_n), F32)], compiler_params=_params("arbitrary"),
    )(dx, mixed, scale)


def _hg_gates(qa, fa, lb):
    sig = _sigmoid(fa)
    f = lb + (1.0 - lb) * sig
    sq = _sigmoid(qa)
    return sig, f, jnp.log(f), 1.0 - f, sq, qa * sq


def _hg_chunk_terms(q, k, g, lincl):
    gc = jnp.dot(lincl, g, precision=HIGHEST, preferred_element_type=F32)
    glast = gc[HG_CHUNK - 1:HG_CHUNK, :]
    gm = gc[HG_MID:HG_MID + 1, :]
    e_q, e_l = jnp.exp(gc), jnp.exp(glast - gc)
    e_m, e_mi = jnp.exp(gc - gm), jnp.exp(gm - gc)
    return glast, (e_q, e_l, e_m, e_mi), (q * e_q, k * e_l, q * e_m, k * e_mi)


def _hg_setup(s_n, n_heads):
    tb = _pick(s_n, HG_BLOCK, HG_CHUNK)
    par = HG_PAR if n_heads % HG_PAR == 0 else 1
    cols = [slice(p * HEAD, (p + 1) * HEAD) for p in range(par)]
    return tb, s_n // tb, tb // HG_CHUNK, par, cols


def _hg_fwd(name, proj, lb_logits, hgain, n_heads):
    s_n = proj.shape[0]
    h_n = n_heads
    tb, nblk, ncb, par, cols = _hg_setup(s_n, h_n)
    c_n = HG_CHUNK
    heads = range(par)

    def body(qa_ref, fa_ref, ia_ref, ga_ref, l_ref, gn_ref, oraw_ref, oa_ref, st_ref, state):
        @pl.when(pl.program_id(1) == 0)
        def _():
            state[...] = jnp.zeros_like(state)

        lv = l_ref[...]
        lbs = [_sigmoid(lv[0:1, c] - lv[1:2, c]) for c in cols]
        row = lax.broadcasted_iota(jnp.int32, (c_n, c_n), 0)
        col = lax.broadcasted_iota(jnp.int32, (c_n, c_n), 1)
        causal = col <= row
        lincl = causal.astype(F32)
        gn = gn_ref[...]

        def chunk(ci, carry):
            sl = pl.ds(pl.multiple_of(ci * c_n, c_n), c_n)
            gates = [_hg_gates(qa_ref[sl, cols[p]], fa_ref[sl, cols[p]], lbs[p]) for p in heads]
            terms = [_hg_chunk_terms(gates[p][5], gates[p][3], gates[p][2], lincl) for p in heads]
            vbs = [ia_ref[sl, cols[p]].astype(BF16) for p in heads]
            sts = [state[p] for p in heads]
            atts = [lax.dot_general(terms[p][2][2].astype(BF16), terms[p][2][3].astype(BF16), NT_DIMS,
                                    preferred_element_type=F32) for p in heads]
            inter = [lax.dot_general(terms[p][2][0].astype(BF16), sts[p].astype(BF16), NT_DIMS,
                                     preferred_element_type=F32) for p in heads]
            grown = [lax.dot_general(vbs[p], terms[p][2][1].astype(BF16), TN_DIMS, preferred_element_type=F32)
                     for p in heads]
            attb = [jnp.where(causal, atts[p], 0.0).astype(BF16) for p in heads]
            outs = [inter[p] + jnp.dot(attb[p], vbs[p], preferred_element_type=F32) for p in heads]
            for p in heads:
                st_ref[p, ci] = sts[p]
                state[p] = sts[p] * jnp.exp(terms[p][0]) + grown[p]
                o = outs[p]
                oraw_ref[sl, cols[p]] = o
                ga = ga_ref[sl, cols[p]]
                oa_ref[sl, cols[p]] = (o * _rstd(o) * gn * (ga * _sigmoid(ga))).astype(oa_ref.dtype)
            return carry

        lax.fori_loop(0, ncb, chunk, 0)

    wide = par * HEAD
    blk = lambda off: pl.BlockSpec((tb, wide), lambda h, c: (c, off // par + h))
    return pl.pallas_call(
        body, name=name, grid=(h_n // par, nblk),
        in_specs=[blk(0), blk(h_n), blk(2 * h_n), blk(3 * h_n),
                  pl.BlockSpec((2, wide), lambda h, c: (0, h)), pl.BlockSpec((1, HEAD), lambda h, c: (0, 0))],
        out_specs=[blk(0), blk(0), pl.BlockSpec((par, ncb, HEAD, HEAD), lambda h, c: (h, c, 0, 0))],
        out_shape=[jax.ShapeDtypeStruct((s_n, h_n * HEAD), F32), jax.ShapeDtypeStruct((s_n, h_n * HEAD), BF16),
                   jax.ShapeDtypeStruct((h_n, s_n // c_n, HEAD, HEAD), F32)],
        scratch_shapes=[pltpu.VMEM((par, HEAD, HEAD), F32)], compiler_params=_params("parallel", "arbitrary"),
    )(proj, proj, proj, proj, lb_logits, hgain)


def _hg_bwd(name, proj, dcat, oraw, states, lb_logits, hgain, n_heads, exchange):
    s_n = proj.shape[0]
    h_n = n_heads
    tb, nblk, ncb, par, cols = _hg_setup(s_n, h_n)
    c_n = HG_CHUNK
    n_steps = h_n // par
    nx = len(exchange)

    def body(*refs):
        qa_ref, fa_ref, ia_ref, ga_ref, doa_ref, oraw_ref, st_ref, l_ref, gn_ref = refs[:9]
        dqa_ref, dfa_ref, dia_ref, dga_ref, dl_ref, dgn_ref = refs[9 + nx:15 + nx]
        dstate, dlb_acc, dgn_acc = refs[15 + 2 * nx:18 + 2 * nx]
        h, c = pl.program_id(0), pl.program_id(1)
        plan = _SiblingPlan(refs[9:9 + nx], refs[15 + nx:15 + 2 * nx], refs[18 + 2 * nx:])
        pl.when((h == 0) & (c == 0))(plan.start)
        pl.when((h == n_steps - 1) & (c == nblk - 1))(plan.finish)

        @pl.when(c == 0)
        def _():
            dstate[...] = jnp.zeros_like(dstate)
            dlb_acc[...] = jnp.zeros_like(dlb_acc)

        @pl.when((c == 0) & (h == 0))
        def _():
            dgn_acc[...] = jnp.zeros_like(dgn_acc)

        lv = l_ref[...]
        lbs = [_sigmoid(lv[0:1, cc] - lv[1:2, cc]) for cc in cols]
        row = lax.broadcasted_iota(jnp.int32, (c_n, c_n), 0)
        col = lax.broadcasted_iota(jnp.int32, (c_n, c_n), 1)
        causal = col <= row
        lincl = causal.astype(F32)
        uincl = (col >= row).astype(F32)
        is_last = lax.broadcasted_iota(jnp.int32, (c_n, 1), 0) == c_n - 1
        gn = gn_ref[...]

        def head_chunk(p, sl, ci):
            cc, lb = cols[p], lbs[p]
            qa = qa_ref[sl, cc]
            sig, f, g, k, sq, q = _hg_gates(qa, fa_ref[sl, cc], lb)
            glast, (e_q, e_l, e_m, e_mi), (qe, kl, qm, km) = _hg_chunk_terms(q, k, g, lincl)
            yield
            v = ia_ref[sl, cc]
            vb = v.astype(BF16)
            qmb, kmb, qeb, klb = qm.astype(BF16), km.astype(BF16), qe.astype(BF16), kl.astype(BF16)
            att = lax.dot_general(qmb, kmb, NT_DIMS, preferred_element_type=F32)

            o = oraw_ref[sl, cc]
            ga = ga_ref[sl, cc]
            sg = _sigmoid(ga)
            r = _rstd(o)
            doa = doa_ref[sl, cc]
            dn = doa * (ga * sg)
            dga_ref[sl, cc] = (doa * (o * r * gn) * _dsilu(ga, sg)).astype(dga_ref.dtype)
            yield
            attb = jnp.where(causal, att, 0.0).astype(BF16)
            do, dgt = _rms_bwd_rows(dn, o, gn, r)
            dgn_acc[...] += dgt
            dob = do.astype(BF16)

            st0 = st_ref[p, ci]
            ds1 = dstate[p]
            st0b, ds1b = st0.astype(BF16), ds1.astype(BF16)
            datt = lax.dot_general(dob, vb, NT_DIMS, preferred_element_type=F32)
            dv = lax.dot_general(attb, dob, TN_DIMS, preferred_element_type=F32)
            dv = dv + lax.dot_general(klb, ds1b, NT_DIMS, preferred_element_type=F32)
            dqe = jnp.dot(dob, st0b, preferred_element_type=F32)
            dkl = jnp.dot(vb, ds1b, preferred_element_type=F32)
            eg = jnp.exp(glast)
            dstate[p] = ds1 * eg + lax.dot_general(dob, qeb, TN_DIMS, preferred_element_type=F32)
            yield
            dattb = jnp.where(causal, datt, 0.0).astype(BF16)
            dqm = jnp.dot(dattb, kmb, preferred_element_type=F32)
            dkm = lax.dot_general(dattb, qmb, TN_DIMS, preferred_element_type=F32)
            dia_ref[sl, cc] = dv.astype(dia_ref.dtype)
            yield
            dq = dqm * e_m + dqe * e_q
            dk = dkm * e_mi + dkl * e_l
            dgc = dqm * qmb.astype(F32) - dkm * kmb.astype(F32) + dqe * qe - dkl * kl
            dglast = jnp.sum(dkl * kl, axis=0, keepdims=True) + eg * jnp.sum(ds1 * st0, axis=0, keepdims=True)
            dgc = dgc + jnp.where(is_last, dglast, 0.0)
            dg = jnp.dot(uincl, dgc, precision=HIGHEST, preferred_element_type=F32)
            dqa_ref[sl, cc] = (dq * _dsilu(qa, sq)).astype(dqa_ref.dtype)
            yield
            df = dg / f - dk
            dfa_ref[sl, cc] = (df * (1.0 - lb) * sig * (1.0 - sig)).astype(dfa_ref.dtype)
            dlb_acc[p] += df * (1.0 - sig)

        def chunk(idx, carry):
            ci = ncb - 1 - idx
            sl = pl.ds(pl.multiple_of(ci * c_n, c_n), c_n)
            running = [head_chunk(p, sl, ci) for p in range(par)]
            while running:
                running = [g for g in running if next(g, True) is None]
            return carry

        lax.fori_loop(0, ncb, chunk, 0)

        @pl.when(c == nblk - 1)
        def _():
            first = lax.broadcasted_iota(jnp.int32, (2, HEAD), 0) == 0
            for p, cc in enumerate(cols):
                dl0 = jnp.sum(dlb_acc[p], axis=0, keepdims=True) * lbs[p] * (1.0 - lbs[p])
                dl_ref[:, cc] = jnp.where(first, dl0, -dl0)

        @pl.when((c == nblk - 1) & (h == n_steps - 1))
        def _():
            dgn_ref[...] = jnp.sum(dgn_acc[...], axis=0, keepdims=True)

    wide = par * HEAD
    blk = lambda off: pl.BlockSpec((tb, wide), lambda h, c: (nblk - 1 - c, off // par + h))
    out_act = jax.ShapeDtypeStruct((s_n, h_n * HEAD), BF16)
    outs = pl.pallas_call(
        body, name=name, grid=(n_steps, nblk),
        in_specs=[blk(0), blk(h_n), blk(2 * h_n), blk(3 * h_n), blk(0), blk(0),
                  pl.BlockSpec((par, ncb, HEAD, HEAD), lambda h, c: (h, nblk - 1 - c, 0, 0)),
                  pl.BlockSpec((2, wide), lambda h, c: (0, h)), pl.BlockSpec((1, HEAD), lambda h, c: (0, 0))]
        + [ANY] * nx,
        out_specs=[blk(0), blk(0), blk(0), blk(0), pl.BlockSpec((2, wide), lambda h, c: (0, h)),
                   pl.BlockSpec((1, HEAD), lambda h, c: (0, 0))] + [ANY] * nx,
        out_shape=[out_act, out_act, out_act, out_act, jax.ShapeDtypeStruct((2, h_n * HEAD), F32),
                   jax.ShapeDtypeStruct((1, HEAD), F32)] + _SiblingPlan.out_shapes(exchange),
        scratch_shapes=[pltpu.VMEM((par, HEAD, HEAD), F32), pltpu.VMEM((par, c_n, HEAD), F32),
                        pltpu.VMEM((c_n, HEAD), F32)] + _SiblingPlan.scratch(nx),
        compiler_params=_params("arbitrary", "arbitrary"),
    )(proj, proj, proj, proj, dcat, oraw, states, lb_logits, hgain, *exchange)
    return outs[:6], outs[6:]


def _split_dot(t, ones_b):
    return jnp.dot(t.astype(BF16), ones_b, preferred_element_type=F32)


def _softplus(z):
    return jnp.maximum(z, 0.0) + jnp.log(1.0 + jnp.exp2(jnp.abs(z) * (-LOG2_E)))


def _att_setup(projb, n_heads, want):
    s_n = projb.shape[0]
    t_n = _pick(s_n, ATT_BLOCK, 8)
    par = want if n_heads % want == 0 else 1
    cols = [slice(p * HEAD, (p + 1) * HEAD) for p in range(par)]
    wide = par * HEAD
    full = lambda off: pl.BlockSpec((s_n, wide), lambda h, i: (0, off // par + h))
    tile = lambda off: pl.BlockSpec((t_n, wide), lambda h, i: (i, off // par + h))
    return s_n, t_n, par, cols, full, tile


def _att_fwd(name, projb, n_heads, gather_units, gather_fulls):
    h_n = n_heads
    s_n, t_n, par, cols, full, tile_spec = _att_setup(projb, h_n, ATT_PAR_FWD)
    scale = 1.0 / math.sqrt(HEAD)
    ng = len(gather_fulls)
    h_steps, i_steps = h_n // par, s_n // t_n

    def body(*refs):
        q_ref, k_ref, v_ref = refs[:3]
        o_ref, lt_ref = refs[3 + ng:5 + ng]
        plan = _GatherPlan(gather_units, refs[5 + ng:5 + 2 * ng], refs[5 + 2 * ng:])
        h, i = pl.program_id(0), pl.program_id(1)
        pl.when((h == 0) & (i == 0))(plan.start)
        pl.when((h == h_steps - 1) & (i == (5 * i_steps) // 8))(plan.forward)
        pl.when((h == h_steps - 1) & (i == i_steps - 1))(plan.finish)
        row = lax.broadcasted_iota(jnp.int32, (t_n, t_n), 0)
        col = lax.broadcasted_iota(jnp.int32, (t_n, t_n), 1)
        from_here = (row >= col).astype(BF16)
        tri = col < row
        qs = [(q_ref[:, c].astype(F32) * scale).astype(BF16) for c in cols]

        def tile(j, carry, diagonal):
            sl = pl.ds(pl.multiple_of(j * t_n, t_n), t_n)
            zs = [lax.dot_general(qs[p], k_ref[sl, c], NT_DIMS, preferred_element_type=F32)
                  for p, c in enumerate(cols)]
            mid = []
            for p in range(par):
                z = zs[p]
                sp = _softplus(z)
                if diagonal:
                    sp = jnp.where(tri, sp, 0.0)
                mid.append((z - carry[p][1], _split_dot(sp, from_here)))
            out = []
            for p, c in enumerate(cols):
                zr, spent = mid[p]
                w = jnp.exp(zr - spent)
                if diagonal:
                    w = jnp.where(tri, w, 0.0)
                acc = carry[p][0] + jnp.dot(w.astype(BF16), v_ref[sl, c], preferred_element_type=F32)
                out.append((acc, carry[p][1] + spent[:, 0:1]))
            return tuple(out)

        init = tuple((jnp.zeros((t_n, HEAD), F32), jnp.zeros((t_n, 1), F32)) for _ in cols)
        carry = tile(i, init, True)
        carry = lax.fori_loop(0, i, lambda jj, cr: tile(i - 1 - jj, cr, False), carry)
        for p, c in enumerate(cols):
            o_ref[:, c] = carry[p][0].astype(o_ref.dtype)
            lt_ref[:, c] = jnp.broadcast_to(carry[p][1], (t_n, HEAD))

    outs = pl.pallas_call(
        body, name=name, grid=(h_steps, i_steps),
        in_specs=[tile_spec(4 * h_n), full(5 * h_n), full(6 * h_n)] + [ANY] * ng,
        out_specs=[tile_spec(0), tile_spec(0)] + [ANY] * ng,
        out_shape=[jax.ShapeDtypeStruct((s_n, h_n * HEAD), BF16), jax.ShapeDtypeStruct((s_n, h_n * HEAD), F32)]
        + [jax.ShapeDtypeStruct(f.shape, f.dtype) for f in gather_fulls],
        input_output_aliases={3 + u: 2 + u for u in range(ng)},
        scratch_shapes=_GatherPlan.scratch(ng), compiler_params=_params("arbitrary", "arbitrary"),
    )(projb, projb, projb, *gather_fulls)
    return outs[0], outs[1], outs[2:]


def _att_bwd(name, projb, dcat, spent_all, n_heads, partials):
    h_n = n_heads
    s_n, t_n, par, cols, full, tile_spec = _att_setup(projb, h_n, ATT_PAR_BWD)
    scale = 1.0 / math.sqrt(HEAD)
    npart = len(partials)
    h_steps, i_steps = h_n // par, s_n // t_n

    def body(*refs):
        q_ref, k_ref, v_ref, do_ref, lt_ref = refs[:5]
        dq_ref, dk_ref, dv_ref = refs[5 + npart:8 + npart]
        plan = _OwnersPlan(refs[5:5 + npart], refs[8 + npart:8 + 2 * npart], refs[8 + 2 * npart:10 + 2 * npart])
        dkt, dvt = refs[10 + 2 * npart:]
        h, i = pl.program_id(0), pl.program_id(1)
        pl.when((h == 0) & (i == 0))(plan.start)
        pl.when((h == h_steps - 1) & (i == i_steps - 1))(plan.finish)

        @pl.when(i == 0)
        def _():
            dkt[...] = jnp.zeros_like(dkt)
            dvt[...] = jnp.zeros_like(dvt)

        row = lax.broadcasted_iota(jnp.int32, (t_n, t_n), 0)
        col = lax.broadcasted_iota(jnp.int32, (t_n, t_n), 1)
        before = (row < col).astype(BF16)
        upto = (row <= col).astype(BF16)
        tri = col < row
        q32 = [q_ref[:, c].astype(F32) * scale for c in cols]
        qs = [t.astype(BF16) for t in q32]
        qts = [t.T.astype(BF16) for t in q32]
        do32 = [do_ref[:, c] for c in cols]
        dos = [t.astype(BF16) for t in do32]
        dots = [t.T.astype(BF16) for t in do32]
        last = slice(t_n - 1, t_n)

        def tile(j, carry, diagonal):
            sl = pl.ds(pl.multiple_of(j * t_n, t_n), t_n)
            zs = [lax.dot_general(qs[p], k_ref[sl, c], NT_DIMS, preferred_element_type=F32)
                  for p, c in enumerate(cols)]
            dws = [lax.dot_general(dos[p], v_ref[sl, c], NT_DIMS, preferred_element_type=F32)
                   for p, c in enumerate(cols)]
            mid1 = []
            for p in range(par):
                z = zs[p]
                sp = _softplus(z)
                sg = jnp.exp(z - sp)
                if diagonal:
                    sp = jnp.where(tri, sp, 0.0)
                prior = _split_dot(sp, before)
                mid1.append((z - carry[p][1], sg, prior, prior[:, last] + sp[:, last]))
            mid2 = []
            for p in range(par):
                zb, sg, prior, sp_sum = mid1[p]
                w = jnp.exp(zb + prior)
                if diagonal:
                    w = jnp.where(tri, w, 0.0)
                e = dws[p] * w
                mid2.append((w.astype(BF16), e, sg, _split_dot(e, upto), sp_sum))
            out = []
            for p, c in enumerate(cols):
                wb, e, sg, e_upto, sp_sum = mid2[p]
                dz = e - sg * (carry[p][2] + e_upto)
                if diagonal:
                    dz = jnp.where(tri, dz, 0.0)
                dz = dz.astype(BF16)
                dq = carry[p][0] + jnp.dot(dz, k_ref[sl, c], preferred_element_type=F32)
                dkt[p, j] += jnp.dot(qts[p], dz, preferred_element_type=F32)
                dvt[p, j] += jnp.dot(dots[p], wb, preferred_element_type=F32)
                out.append((dq, carry[p][1] - sp_sum, carry[p][2] + e_upto[:, last]))
            return tuple(out)

        init = tuple((jnp.zeros((t_n, HEAD), F32), lt_ref[:, c][:, 0:1], jnp.zeros((t_n, 1), F32)) for c in cols)
        carry = lax.fori_loop(0, i, lambda j, cr: tile(j, cr, False), init)
        carry = tile(i, carry, True)
        for p, c in enumerate(cols):
            dq_ref[:, c] = (carry[p][0] * scale).astype(dq_ref.dtype)

        @pl.when(i == i_steps - 1)
        def _():
            def put(j, _):
                sl = pl.ds(pl.multiple_of(j * t_n, t_n), t_n)
                for p, c in enumerate(cols):
                    dk_ref[sl, c] = dkt[p, j].T.astype(dk_ref.dtype)
                    dv_ref[sl, c] = dvt[p, j].T.astype(dv_ref.dtype)
                return 0

            lax.fori_loop(0, i_steps, put, 0)

    act = jax.ShapeDtypeStruct((s_n, h_n * HEAD), BF16)
    acc_t = pltpu.VMEM((par, i_steps, HEAD, t_n), F32)
    outs = pl.pallas_call(
        body, name=name, grid=(h_steps, i_steps),
        in_specs=[tile_spec(4 * h_n), full(5 * h_n), full(6 * h_n), tile_spec(h_n), tile_spec(0)] + [ANY] * npart,
        out_specs=[tile_spec(0), full(0), full(0)] + [ANY] * npart,
        out_shape=[act, act, act] + _OwnersPlan.out_shapes(partials),
        scratch_shapes=_OwnersPlan.scratch(npart) + [acc_t, acc_t],
        compiler_params=_params("arbitrary", "arbitrary"),
    )(projb, projb, projb, dcat, spent_all, *partials)
    return outs[0], outs[1], outs[2], outs[3:]


def _adamw(name, w, g, m, v, tr=256):
    r_n, c_n = w.shape
    tr = _pick(r_n, tr, 8)
    c1 = 1.0 - ADAM_B1 ** ADAM_STEP
    c2 = 1.0 - ADAM_B2 ** ADAM_STEP

    def body(w_ref, g_ref, m_ref, v_ref, d_ref, nm_ref, nv_ref):
        gv = g_ref[...]
        nm = ADAM_B1 * m_ref[...] + (1.0 - ADAM_B1) * gv
        nv = ADAM_B2 * v_ref[...] + (1.0 - ADAM_B2) * (gv * gv)
        d_ref[...] = -ADAM_LR * ((nm / c1) / (jnp.sqrt(nv / c2) + ADAM_EPS) + ADAM_WD * w_ref[...])
        nm_ref[...] = nm
        nv_ref[...] = nv

    blk = pl.BlockSpec((tr, c_n), lambda i: (i, 0))
    sds = jax.ShapeDtypeStruct((r_n, c_n), F32)
    return pl.pallas_call(
        body, name=name, grid=(r_n // tr,), in_specs=[blk] * 4, out_specs=[blk] * 3, out_shape=[sds] * 3,
        compiler_params=_params("parallel"),
    )(w, g, m, v)


def _adamw_nd(name, w, g, m, v):
    shape = w.shape
    flat = lambda t: t.reshape(-1, shape[-1])
    return tuple(t.reshape(shape) for t in _adamw(name, flat(w), flat(g.reshape(shape)), flat(m), flat(v)))


def _mesh_pos():
    x, y, c = lax.axis_index("x"), lax.axis_index("y"), lax.axis_index("c")
    chips = [(1 - x, y), (x, 1 - y), (1 - x, 1 - y)]
    return x, y, c, chips, 2 * x + y, [2 * cx + cy for cx, cy in chips]


class _Unit:
    def __init__(self, shard_shape, axis, half_axis):
        self.shard_shape = tuple(shard_shape)
        self.axis = axis
        self.half_axis = half_axis
        self.full_shape = tuple(n * N_CHIPS if a == axis else n for a, n in enumerate(shard_shape))
        self.half_shape = tuple(n // 2 if a == half_axis else n for a, n in enumerate(shard_shape))

    def _window(self, ref, k, c, with_slab):
        idx = []
        for a, n in enumerate(self.shard_shape):
            start, size = 0, n
            if a == self.half_axis:
                size = n // 2
                start = c * size
            if with_slab and a == self.axis:
                start = start + k * n
            idx.append(pl.ds(start, size))
        return ref.at[tuple(idx)]

    def full_half(self, ref, k, c):
        return self._window(ref, k, c, True)

    def place_view(self):
        s = self.shard_shape
        if self.axis == len(s) - 1:
            return math.prod(s[:-2]), s[-2], s[-1], True
        assert self.axis == len(s) - 2
        return math.prod(s[:self.axis]), s[self.axis], s[-1], False

    def half_view(self):
        s, h = self.shard_shape, self.half_axis
        if h == len(s) - 1:
            return math.prod(s[:-2]), s[-2], s[-1] // 2, True
        return math.prod(s[:h]), (s[h] // 2) * math.prod(s[h + 1:-1]), s[-1], False


def _place_shard(name, shard, unit, chip_idx):
    l_n, r_n, c_n, by_cols = unit.place_view()
    tr = _pick(r_n, 256, 16)
    per = r_n // tr

    def body(k_ref, s_ref, o_ref):
        o_ref[...] = s_ref[...].astype(o_ref.dtype)

    if by_cols:
        full3, out_index = (l_n, r_n, N_CHIPS * c_n), (lambda l, i, k_ref: (l, i, k_ref[0]))
    else:
        full3, out_index = (l_n, N_CHIPS * r_n, c_n), (lambda l, i, k_ref: (l, k_ref[0] * per + i, 0))
    out = pl.pallas_call(
        body, name=name,
        grid_spec=pltpu.PrefetchScalarGridSpec(
            num_scalar_prefetch=1, grid=(l_n, per),
            in_specs=[pl.BlockSpec((None, tr, c_n), lambda l, i, k_ref: (l, i, 0))],
            out_specs=pl.BlockSpec((None, tr, c_n), out_index)),
        out_shape=jax.ShapeDtypeStruct(full3, BF16), compiler_params=_params("parallel", "parallel"),
    )(chip_idx, shard.reshape(l_n, r_n, c_n))
    return out.reshape(unit.full_shape)


def _gather_weights(units, fulls, scale_shard):
    nu = len(units)
    ps = scale_shard.shape[1]

    def body(*refs):
        sc_in = refs[nu]
        outs, sc_out = refs[nu + 1:2 * nu + 1], refs[2 * nu + 1]
        send3, recv3, lsem = refs[2 * nu + 2:2 * nu + 5]
        plan = _GatherPlan(units, outs, refs[2 * nu + 5:])
        x, y, c, chips, me, others = _mesh_pos()
        local = pltpu.make_async_copy(sc_in, sc_out.at[:, pl.ds(me * ps, ps)], lsem.at[0])
        local.start()
        plan.start()
        sends = [pltpu.make_async_remote_copy(
            src_ref=sc_in, dst_ref=sc_out.at[:, pl.ds(me * ps, ps)], send_sem=send3.at[j], recv_sem=recv3.at[j],
            device_id=(*chip, c), device_id_type=MESH) for j, chip in enumerate(chips)]
        for cp in sends:
            cp.start()
        plan.forward()
        plan.finish()
        for j in range(3):
            dst = sc_out.at[:, pl.ds(others[j] * ps, ps)]
            pltpu.make_async_remote_copy(src_ref=dst, dst_ref=dst, send_sem=send3.at[j], recv_sem=recv3.at[j],
                                         device_id=(x, y, c), device_id_type=MESH).wait_recv()
        for cp in sends:
            cp.wait_send()
        local.wait()

    out_shape = [jax.ShapeDtypeStruct(f.shape, f.dtype) for f in fulls]
    out_shape.append(jax.ShapeDtypeStruct((1, N_CHIPS * ps), scale_shard.dtype))
    dma = pltpu.SemaphoreType.DMA
    return pl.pallas_call(
        body, name="gather_weights", in_specs=[ANY] * (nu + 1), out_specs=[ANY] * (nu + 1), out_shape=out_shape,
        input_output_aliases={u: u for u in range(nu)},
        scratch_shapes=[dma((3,)), dma((3,)), dma((1,))] + _GatherPlan.scratch(nu),
    )(*fulls, scale_shard)


class _GatherPlan:
    def __init__(self, units, outs, sems):
        self.units, self.outs = units, outs
        self.ici, self.d2d = (sems[0], sems[1]), (sems[2], sems[3])
        self.x, self.y, self.c, self.chips, self.me, self.others = _mesh_pos()
        self.pairs = [(u, j) for u in range(len(units)) for j in range(3)]

    @staticmethod
    def scratch(nu):
        return [pltpu.SemaphoreType.DMA((3 * nu,)) for _ in range(4)]

    def _copy(self, window, sems, u, j, to):
        return pltpu.make_async_remote_copy(src_ref=window, dst_ref=window, send_sem=sems[0].at[3 * u + j],
                                            recv_sem=sems[1].at[3 * u + j], device_id=to, device_id_type=MESH)

    def _half(self, u, chip, core):
        return self.units[u].full_half(self.outs[u], chip, core)

    def start(self):
        for u, j in self.pairs:
            self._copy(self._half(u, self.me, self.c), self.ici, u, j, (*self.chips[j], self.c)).start()

    def forward(self):
        here, sibling = (self.x, self.y, self.c), (self.x, self.y, 1 - self.c)
        for u, j in self.pairs:
            landed = self._half(u, self.others[j], self.c)
            self._copy(landed, self.ici, u, j, here).wait_recv()
            self._copy(landed, self.d2d, u, j, sibling).start()

    def finish(self):
        here = (self.x, self.y, self.c)
        for u, j in self.pairs:
            self._copy(self._half(u, self.others[j], 1 - self.c), self.d2d, u, j, here).wait_recv()
        for u, j in self.pairs:
            self._copy(self._half(u, self.me, self.c), self.ici, u, j, here).wait_send()
            self._copy(self._half(u, self.others[j], self.c), self.d2d, u, j, here).wait_send()


class _OwnersPlan:
    def __init__(self, ins, outs, sems):
        self.ins, self.outs, self.send, self.recv = ins, outs, sems[0], sems[1]
        _, _, self.c, self.chips, _, self.others = _mesh_pos()

    @staticmethod
    def scratch(nu):
        return [pltpu.SemaphoreType.DMA((3 * nu,)) for _ in range(2)]

    @staticmethod
    def out_shapes(partials):
        return [jax.ShapeDtypeStruct((3,) + p.shape[1:], p.dtype) for p in partials]

    def _copies(self):
        return [pltpu.make_async_remote_copy(
            src_ref=self.ins[u].at[self.others[j]], dst_ref=self.outs[u].at[j], send_sem=self.send.at[3 * u + j],
            recv_sem=self.recv.at[3 * u + j], device_id=(*self.chips[j], self.c), device_id_type=MESH)
            for u in range(len(self.ins)) for j in range(3)]

    def start(self):
        for cp in self._copies():
            cp.start()

    def finish(self):
        for cp in self._copies():
            cp.wait()


class _SiblingPlan:
    def __init__(self, ins, outs, sems):
        self.ins, self.outs, self.send, self.recv = ins, outs, sems[0], sems[1]
        self.x, self.y, self.c, _, _, _ = _mesh_pos()

    @staticmethod
    def scratch(nu):
        return [pltpu.SemaphoreType.DMA((nu,)) for _ in range(2)]

    @staticmethod
    def out_shapes(grads):
        return [jax.ShapeDtypeStruct((g.shape[0],) + g.shape[2:], g.dtype) for g in grads]

    def _copies(self):
        return [pltpu.make_async_remote_copy(
            src_ref=self.ins[u].at[:, 1 - self.c], dst_ref=self.outs[u], send_sem=self.send.at[u],
            recv_sem=self.recv.at[u], device_id=(self.x, self.y, 1 - self.c), device_id_type=MESH)
            for u in range(len(self.ins))]

    def start(self):
        for cp in self._copies():
            cp.start()

    def finish(self):
        for cp in self._copies():
            cp.wait()


def _to_sibling(name, grads):
    nu = len(grads)

    def body(*refs):
        plan = _SiblingPlan(refs[:nu], refs[nu:2 * nu], refs[2 * nu:])
        plan.start()
        plan.finish()

    return pl.pallas_call(
        body, name=name, in_specs=[ANY] * nu, out_specs=[ANY] * nu, out_shape=_SiblingPlan.out_shapes(grads),
        scratch_shapes=_SiblingPlan.scratch(nu),
    )(*grads)


def _share_halves(halves):
    nu = len(halves)

    def body(*refs):
        ins, outs = refs[:nu], refs[nu:2 * nu]
        send, recv = refs[2 * nu:]
        x, y, c, _, _, _ = _mesh_pos()
        cps = [pltpu.make_async_remote_copy(
            src_ref=ins[u], dst_ref=outs[u], send_sem=send.at[u], recv_sem=recv.at[u],
            device_id=(x, y, 1 - c), device_id_type=MESH) for u in range(nu)]
        for cp in cps:
            cp.start()
        for cp in cps:
            cp.wait()

    out_shape = [jax.ShapeDtypeStruct(h.shape, h.dtype) for h in halves]
    dma = pltpu.SemaphoreType.DMA
    return pl.pallas_call(
        body, name="share_halves", in_specs=[ANY] * nu, out_specs=[ANY] * nu, out_shape=out_shape,
        scratch_shapes=[dma((nu,)), dma((nu,))],
    )(*halves)


def _add_mine(name, grad, recv, c_idx):
    _, _, r_n, c_n = grad.shape
    tr = _pick(r_n, 256, 16)

    def body(c_ref, g_ref, r_ref, o_ref):
        o_ref[...] = (g_ref[...] + r_ref[...]).astype(o_ref.dtype)

    return pl.pallas_call(
        body, name=name,
        grid_spec=pltpu.PrefetchScalarGridSpec(
            num_scalar_prefetch=1, grid=(N_CHIPS, r_n // tr),
            in_specs=[pl.BlockSpec((None, None, tr, c_n), lambda k, i, c_ref: (k, c_ref[0], i, 0)),
                      pl.BlockSpec((None, tr, c_n), lambda k, i, c_ref: (k, i, 0))],
            out_specs=pl.BlockSpec((None, tr, c_n), lambda k, i, c_ref: (k, i, 0))),
        out_shape=jax.ShapeDtypeStruct(recv.shape, BF16), compiler_params=_params("parallel", "parallel"),
    )(c_idx, grad, recv)


def _add_slots(name, partial, slots, chip_idx):
    _, r_n, c_n = slots.shape
    tr = _pick(r_n, 256, 16)

    def body(k_ref, p_ref, s_ref, o_ref):
        own = p_ref[...].astype(F32)
        o_ref[...] = ((own + s_ref[0].astype(F32)) + s_ref[1].astype(F32)) + s_ref[2].astype(F32)

    return pl.pallas_call(
        body, name=name,
        grid_spec=pltpu.PrefetchScalarGridSpec(
            num_scalar_prefetch=1, grid=(r_n // tr,),
            in_specs=[pl.BlockSpec((None, tr, c_n), lambda i, k_ref: (k_ref[0], i, 0)),
                      pl.BlockSpec((3, tr, c_n), lambda i, k_ref: (0, i, 0))],
            out_specs=pl.BlockSpec((tr, c_n), lambda i, k_ref: (i, 0))),
        out_shape=jax.ShapeDtypeStruct((r_n, c_n), F32), compiler_params=_params("parallel"),
    )(chip_idx, partial, slots)


def _adamw_halves(name, unit, w, m, v, mine, theirs, c_idx, tr=256):
    l_n, r_n, c_n, by_cols = unit.half_view()
    tr = _pick(r_n, tr, 8)
    c1 = 1.0 - ADAM_B1 ** ADAM_STEP
    c2 = 1.0 - ADAM_B2 ** ADAM_STEP

    def body(c_ref, w_ref, m_ref, v_ref, mine_ref, theirs_ref, g_ref, d_ref, nm_ref, nv_ref):
        gv = jnp.where(pl.program_id(1) == c_ref[0], mine_ref[...], theirs_ref[...])
        nm = ADAM_B1 * m_ref[...] + (1.0 - ADAM_B1) * gv
        nv = ADAM_B2 * v_ref[...] + (1.0 - ADAM_B2) * (gv * gv)
        d_ref[...] = -ADAM_LR * ((nm / c1) / (jnp.sqrt(nv / c2) + ADAM_EPS) + ADAM_WD * w_ref[...])
        g_ref[...] = gv
        nm_ref[...] = nm
        nv_ref[...] = nv

    if by_cols:
        view = (l_n, r_n, 2 * c_n)
        whole = pl.BlockSpec((None, tr, c_n), lambda l, h, i, c_ref: (l, i, h))
    else:
        view = (l_n, 2, r_n, c_n)
        whole = pl.BlockSpec((None, None, tr, c_n), lambda l, h, i, c_ref: (l, h, i, 0))
    mine_spec = pl.BlockSpec((None, tr, c_n), lambda l, h, i, c_ref: (l, jnp.where(h == c_ref[0], i, 0), 0))
    theirs_spec = pl.BlockSpec((None, tr, c_n), lambda l, h, i, c_ref: (l, jnp.where(h == c_ref[0], 0, i), 0))
    sds = jax.ShapeDtypeStruct(view, F32)
    outs = pl.pallas_call(
        body, name=name,
        grid_spec=pltpu.PrefetchScalarGridSpec(
            num_scalar_prefetch=1, grid=(l_n, 2, r_n // tr),
            in_specs=[whole, whole, whole, mine_spec, theirs_spec], out_specs=[whole] * 4),
        out_shape=[sds] * 4, compiler_params=_params("parallel", "parallel", "parallel"),
    )(c_idx, w.reshape(view), m.reshape(view), v.reshape(view),
      mine.reshape(l_n, r_n, c_n), theirs.reshape(l_n, r_n, c_n))
    return tuple(t.reshape(w.shape) for t in outs)


def _allreduce_small(block):
    r_n, c_n = block.shape

    def body(in_ref, out_ref, slots, send, recv):
        x, y, c = lax.axis_index("x"), lax.axis_index("y"), lax.axis_index("c")
        me = 4 * x + 2 * y + c
        slots[me] = in_ref[...]
        flips = [(fx, fy, fc) for fx in (0, 1) for fy in (0, 1) for fc in (0, 1)][1:]
        peers = [(x ^ fx, y ^ fy, c ^ fc) for fx, fy, fc in flips]
        cps = [pltpu.make_async_remote_copy(src_ref=in_ref, dst_ref=slots.at[me], send_sem=send.at[j],
                                            recv_sem=recv.at[j], device_id=peers[j], device_id_type=MESH)
               for j in range(7)]
        for cp in cps:
            cp.start()
        for j, (px, py, pc) in enumerate(peers):
            slot = slots.at[4 * px + 2 * py + pc]
            pltpu.make_async_remote_copy(src_ref=slot, dst_ref=slot, send_sem=send.at[j], recv_sem=recv.at[j],
                                         device_id=(x, y, c), device_id_type=MESH).wait_recv()
        for cp in cps:
            cp.wait_send()
        total = slots[0]
        for d in range(1, 8):
            total = total + slots[d]
        out_ref[...] = total

    vmem = pl.BlockSpec(memory_space=pltpu.VMEM)
    return pl.pallas_call(
        body, name="allreduce_small", in_specs=[vmem], out_specs=vmem,
        out_shape=jax.ShapeDtypeStruct((r_n, c_n), F32),
        scratch_shapes=[pltpu.VMEM((8, r_n, c_n), F32), pltpu.SemaphoreType.DMA((7,)), pltpu.SemaphoreType.DMA((7,))],
    )(block)


def _first(accs, extras):
    return [accs[0]] if isinstance(accs, list) else [accs]


def _ffn_fwd(tag, x_in, h, wg, wu, wd):
    def act(accs, extras):
        a, b = accs
        return [a, b, a * _sigmoid(a) * b]

    a, b, s = _mm_nn(f"ffn_up_{tag}", h, [wg, wu], [], act, [BF16, BF16, BF16])
    x_out, = _mm_nn(f"ffn_down_{tag}", s, [wd], [x_in], lambda accs, ex: [ex[0] + accs[0]], [F32], tk=8192)
    return x_out, a, b, s


def _ffn_bwd(tag, layer, dx_out, h, a, b, s, wg, wu, wd, into):
    def mid(acc, extras):
        av, bv = extras[0].astype(F32), extras[1].astype(F32)
        sg = _sigmoid(av)
        return [acc * bv * _dsilu(av, sg), acc * (av * sg)]

    dxb = dx_out.astype(BF16)
    da, db = _mm_nt(f"ffn_dact_{tag}", [(dxb, wd)], [a, b], mid, [BF16, BF16], tm=512, to=wd.shape[1] // N_CHIPS,
                    weights_stay=True)
    dh, = _mm_nt(f"ffn_dh_{tag}", [(da, wg), (db, wu)], [], _first, [F32], tm=1024, tr=2816)
    d_n, f_n = wg.shape[1], wg.shape[2]
    ns = f_n // N_CHIPS
    tki = _pick(d_n // 2, 512)
    ih = (d_n // 2) // tki
    col_shape = (N_CHIPS, 2, 2, d_n // 2, ns)
    col_block = (None, None, None, tki, ns)
    col_index = lambda g, i, j: (j, i // ih, layer, i % ih, 0)
    dwg = _mm_tn(f"ffn_dwg_{tag}", h, da, 1, col_shape, col_block, col_index, tki=tki, tn=ns, into=into[0])
    dwu = _mm_tn(f"ffn_dwu_{tag}", h, db, 1, col_shape, col_block, col_index, tki=tki, tn=ns, into=into[1])
    tn = _pick(d_n // 2, 512)
    jh = (d_n // 2) // tn
    dwd = _mm_tn(f"ffn_dwd_{tag}", s, dxb, 1, (N_CHIPS, 2, 2, ns, d_n // 2), (None, None, None, ns, tn),
                 lambda g, i, j: (i, j // jh, layer, 0, j % jh), tki=ns, tn=tn, into=into[2])
    return dh, (dwg, dwu, dwd)


def kernel(x, mix_norm, ffn_norm, final_norm, ab_w_in, lb_logits, hg_out_norm, ab_w_out, pool_w, pool_scale, ffn_w_gate, ffn_w_up, ffn_w_down, loss_target, m_mix_norm, m_ffn_norm, m_final_norm, m_ab_w_in, m_lb_logits, m_hg_out_norm, m_ab_w_out, m_pool_w, m_pool_scale, m_ffn_w_gate, m_ffn_w_up, m_ffn_w_down, v_mix_norm, v_ffn_norm, v_final_norm, v_ab_w_in, v_lb_logits, v_hg_out_norm, v_ab_w_out, v_pool_w, v_pool_scale, v_ffn_w_gate, v_ffn_w_up, v_ffn_w_down):
    xs, target = x[0], loss_target[0]
    s_n, d_n = xs.shape
    h_n = d_n // 2 // HEAD
    hw = h_n * HEAD
    n_grp = len(POOL_WINDOWS)
    grp = d_n // n_grp
    c_idx = lax.axis_index("c").astype(jnp.int32).reshape(1)
    chip = 2 * lax.axis_index("x") + lax.axis_index("y")

    units = [
        _Unit(ab_w_in.shape[1:], 1, 0),
        _Unit(ab_w_out.shape[1:], 0, 0),
        _Unit(pool_w.shape[1:], 1, 0),
        _Unit(ffn_w_gate.shape, 2, 1),
        _Unit(ffn_w_up.shape, 2, 1),
        _Unit(ffn_w_down.shape, 1, 2),
    ]
    chip_idx = chip.astype(jnp.int32).reshape(1)
    shards = [ab_w_in[0], ab_w_out[0], pool_w[0], ffn_w_gate, ffn_w_up, ffn_w_down]
    placed = [_place_shard(f"place_{n}", t, u, chip_idx) for n, (t, u) in enumerate(zip(shards, units))]
    w_in, w_out, scale_full = _gather_weights(units[:2], placed[:2], pool_scale)
    w_in3, w_out3 = w_in[None], w_out[None]
    row = lambda t: t.reshape(1, -1)

    h0 = _rms_fwd("norm_mix0", xs, row(mix_norm[0]))
    proj, projb = _mm_nn("proj_in", h0, [w_in3], [], lambda accs, ex: [accs[0], accs[0]], [F32, BF16],
                         tm=512, tn=7 * hw // N_CHIPS, weights_stay=True)
    oraw, o_a, states = _hg_fwd("hgrn_fwd", proj, lb_logits, hg_out_norm, h_n)
    o_b, ltot, (w_pool, w_gate, w_up, w_down) = _att_fwd("attn_fwd", projb, h_n, units[2:], placed[2:])
    cat = jnp.concatenate([o_a, o_b], axis=1)
    x1, = _mm_nn("proj_out", cat, [w_out3], [xs], lambda accs, ex: [ex[0] + accs[0]], [F32])
    h1 = _rms_fwd("norm_ffn0", x1, row(ffn_norm[0]))
    x2, a0, b0, s0 = _ffn_fwd("l0", x1, h1, w_gate[0:1], w_up[0:1], w_down[0:1])
    pooled = _pool_fwd("pool_fwd", x2, row(mix_norm[1]))
    x3, mixed = _mm_nn("pool_mix", pooled, [w_pool], [x2, scale_full],
                       lambda accs, ex: [ex[0] + accs[0] * ex[1], accs[0]], [F32, F32], tk=grp, tn=grp)
    h3 = _rms_fwd("norm_ffn1", x3, row(ffn_norm[1]))
    x4, a1, b1, s1 = _ffn_fwd("l1", x3, h3, w_gate[1:2], w_up[1:2], w_down[1:2])

    dx4, d_final, loss = _loss_bwd("loss_bwd", x4, target, row(final_norm))
    dh3, ffn_grads = _ffn_bwd("l1", 1, dx4, h3, a1, b1, s1, w_gate[1:2], w_up[1:2], w_down[1:2], (None, None, None))
    dx3, d_ffn1 = _rms_bwd("norm_ffn1_bwd", dh3, x3, row(ffn_norm[1]), dx4)
    dmixed, d_scale = _scale_bwd("pool_scale_bwd", dx3, mixed, scale_full)
    dpooled, = _mm_nt("pool_dpooled", [(dmixed, w_pool)], [], _first, [F32], to=grp, tr=grp)
    slab_rows = grp // N_CHIPS
    d_pool = _mm_tn("pool_dw", pooled, dmixed, n_grp, (N_CHIPS, 2, n_grp // 2, slab_rows, grp),
                    (None, None, None, slab_rows, grp), lambda g, i, j: (i, g // 2, g % 2, 0, 0),
                    tki=slab_rows, tn=grp)
    dx2, d_mix1 = _pool_bwd("pool_bwd", dpooled, x2, row(mix_norm[1]), dx3)
    dh1, ffn_grads = _ffn_bwd("l0", 0, dx2, h1, a0, b0, s0, w_gate[0:1], w_up[0:1], w_down[0:1], ffn_grads)
    dx1, d_ffn0 = _rms_bwd("norm_ffn0_bwd", dh1, x1, row(ffn_norm[0]), dx2)
    dx1b = dx1.astype(BF16)
    d_wout = _mm_tn("proj_out_dw", cat, dx1b, 1, (1, 2 * hw, d_n), (None, _pick(2 * hw, 512), _pick(d_n, 512)),
                    lambda g, i, j: (g, i, j), tki=_pick(2 * hw, 512), tn=_pick(d_n, 512))
    as4 = lambda g, u: g.reshape(N_CHIPS, 2, -1, u.half_shape[-1])
    early4 = [as4(g, u) for g, u in zip([d_wout, d_pool, *ffn_grads], units[1:])]
    dcat, = _mm_nt("proj_out_dcat", [(dx1b, w_out3)], [], _first, [F32], tm=1024)
    (dqa, dfa, dia, dga, d_lb, d_hgn), early_sib = _hg_bwd("hgrn_bwd", proj, dcat, oraw, states, lb_logits,
                                                            hg_out_norm, h_n, early4)
    early_part = [_add_mine(f"add_sibling_{n + 1}", g, r, c_idx) for n, (g, r) in enumerate(zip(early4, early_sib))]
    dqb, dkb, dvb, early_slots = _att_bwd("attn_bwd", projb, dcat, ltot, h_n, early_part)
    dproj = jnp.concatenate([dqa, dfa, dia, dga, dqb, dkb, dvb], axis=1)
    ns_in = 7 * hw // N_CHIPS
    tki_in, tn_in = _pick(d_n // 2, 512), _pick(ns_in, 1792)
    ih_in, jps_in = (d_n // 2) // tki_in, ns_in // tn_in
    d_win = _mm_tn("proj_in_dw", h0, dproj, 1, (N_CHIPS, 2, d_n // 2, ns_in), (None, None, tki_in, tn_in),
                   lambda g, i, j: (j // jps_in, i // ih_in, i % ih_in, j % jps_in), tki=tki_in, tn=tn_in)
    win4 = as4(d_win, units[0])
    win_part = _add_mine("add_sibling_0", win4, _to_sibling("grad_in_to_sibling", [win4])[0], c_idx)
    dh0, win_slots = _mm_nt("proj_in_dh", [(dproj, w_in3)], [], _first, [F32], tm=1024, tr=3584,
                            exchange=[win_part])
    dx0, d_mix0 = _rms_bwd("norm_mix0_bwd", dh0, xs, row(mix_norm[0]), dx1)

    partials = [win_part, *early_part]
    slots = [win_slots, *early_slots]
    mine = [_add_slots(f"add_chips_{n}", p, s, chip_idx) for n, (p, s) in enumerate(zip(partials, slots))]
    theirs = _share_halves(mine)

    lanes = 2 * d_n
    pad = lambda t: jnp.pad(t.reshape(1, -1), ((0, 0), (0, lanes - t.size)))
    small = jnp.concatenate([
        pad(jnp.concatenate([d_mix0, d_mix1], axis=0)), pad(jnp.concatenate([d_ffn0, d_ffn1], axis=0)),
        pad(d_final), pad(d_lb), pad(d_hgn), pad(d_scale), jnp.zeros((2, lanes), F32)], axis=0)
    small = _allreduce_small(small)
    g_mix = small[0, :2 * d_n].reshape(2, d_n)
    g_ffn = small[1, :2 * d_n].reshape(2, d_n)
    g_final = small[2, :d_n]
    g_lb = small[3, :2 * hw].reshape(2, hw)
    g_hgn = small[4, :HEAD].reshape(1, HEAD)
    g_scale = lax.dynamic_slice(small[5, :d_n], (chip * grp,), (grp,)).reshape(1, grp)
    loss = lax.psum(loss[0, 0], ("x", "y", "c"))

    small_grads = {0: g_mix, 1: g_ffn, 2: g_final, 4: g_lb, 5: g_hgn, 8: g_scale}
    unit_of = {3: 0, 6: 1, 7: 2, 9: 3, 10: 4, 11: 5}
    weights = [mix_norm, ffn_norm, final_norm, ab_w_in, lb_logits, hg_out_norm, ab_w_out, pool_w, pool_scale,
               ffn_w_gate, ffn_w_up, ffn_w_down]
    ms = [m_mix_norm, m_ffn_norm, m_final_norm, m_ab_w_in, m_lb_logits, m_hg_out_norm, m_ab_w_out, m_pool_w,
          m_pool_scale, m_ffn_w_gate, m_ffn_w_up, m_ffn_w_down]
    vs = [v_mix_norm, v_ffn_norm, v_final_norm, v_ab_w_in, v_lb_logits, v_hg_out_norm, v_ab_w_out, v_pool_w,
          v_pool_scale, v_ffn_w_gate, v_ffn_w_up, v_ffn_w_down]
    grads, deltas, new_ms, new_vs = [], [], [], []
    for n, (w, m, v) in enumerate(zip(weights, ms, vs)):
        if n in unit_of:
            u = unit_of[n]
            g, d, nm, nv = _adamw_halves(f"adamw_{n}", units[u], w, m, v, mine[u], theirs[u], c_idx)
        else:
            w2 = w.reshape(1, -1) if w.ndim == 1 else w
            g = small_grads[n].reshape(w2.shape)
            d, nm, nv = _adamw_nd(f"adamw_{n}", w2, g, m.reshape(w2.shape), v.reshape(w2.shape))
        grads.append(g.reshape(w.shape))
        deltas.append(d.reshape(w.shape))
        new_ms.append(nm.reshape(w.shape))
        new_vs.append(nv.reshape(w.shape))
    return (loss, dx0[None], *grads, *deltas, *new_ms, *new_vs)
```
